```python
import math, functools
import jax, jax.numpy as jnp
from jax import lax
import numpy as np

D_MODEL = 2048
BATCH = 2
SEQ = 4096
DEPTH = 1
DEC_BATCH = 128
DEC_SEQ = 4
PAST_LEN = 2048
PAGE_SIZE = 128

ATT_HEAD_DIM = 64
ATT_WIDTH = D_MODEL // 2
ATT_HEADS = ATT_WIDTH // ATT_HEAD_DIM
DILATED_CONFIGS = ((128, 1), (512, 4), (2048, 16))
MAX_WINDOW = 2048
BAND_BLOCK = 128
ROT_DIM = ATT_HEAD_DIM // 4
ROPE_THETA = 500000.0
GLA_HEADS = 4
GLA_VALUE_WIDTH = D_MODEL // 2
GLA_KEY_WIDTH = GLA_VALUE_WIDTH // 2
GLA_HEAD_K = GLA_KEY_WIDTH // GLA_HEADS
GLA_HEAD_V = GLA_VALUE_WIDTH // GLA_HEADS
GLA_GATE_RANK = 16
GLA_GATE_TAU = 16.0
GLA_CHUNK = 64
IN_SPLIT_SIZES = (ATT_WIDTH, ATT_WIDTH, ATT_WIDTH, GLA_KEY_WIDTH, GLA_KEY_WIDTH, GLA_VALUE_WIDTH, GLA_GATE_RANK, GLA_VALUE_WIDTH)
IN_COLS = 3 * ATT_WIDTH + 2 * GLA_KEY_WIDTH + 2 * GLA_VALUE_WIDTH + GLA_GATE_RANK
MIX_WIDTH = ATT_WIDTH + GLA_VALUE_WIDTH
MEM_LEN = 256
MEM_HEADS = 4
MEM_HEAD_DIM = 128
MEM_WIDTH = MEM_HEADS * MEM_HEAD_DIM
D_FF = 5632
EPS = 1e-6

kernel_name = 'hybrid_dilated_gla_macaron_decode_step'


def rms_norm(x, g):
    xf = x.astype(jnp.float32)
    y = xf * lax.rsqrt(jnp.mean(xf * xf, axis=-1, keepdims=True) + EPS) * g.astype(jnp.float32)
    return y.astype(x.dtype)


def swiglu(x, w_gate, w_up, w_down):
    return (jax.nn.silu(x @ w_gate) * (x @ w_up)) @ w_down


def rope_partial(x, pos):
    half = ROT_DIM // 2
    inv = ROPE_THETA ** (-jnp.arange(half, dtype=jnp.float32) * 2.0 / ROT_DIM)
    ang = pos.astype(jnp.float32)[:, None] * inv[None, :]
    cos, sin = jnp.cos(ang)[:, None, :], jnp.sin(ang)[:, None, :]
    xr = x[..., :ROT_DIM].astype(jnp.float32)
    x1, x2 = xr[..., :half], xr[..., half:]
    rot = jnp.concatenate([x1 * cos - x2 * sin, x2 * cos + x1 * sin], axis=-1).astype(x.dtype)
    return jnp.concatenate([rot, x[..., ROT_DIM:]], axis=-1)


def dilated_group_prompt(q, k, v, dil, n_back):
    B, T, H, E = q.shape
    Ls = T // dil
    nb = -(-Ls // BAND_BLOCK)
    Lp = nb * BAND_BLOCK

    def to_sub(a):
        a = a.reshape(B, Ls, dil, H, E).transpose(0, 2, 1, 3, 4)
        a = jnp.pad(a, ((0, 0), (0, 0), (0, Lp - Ls), (0, 0), (0, 0)))
        return a.reshape(B, dil, nb, BAND_BLOCK, H, E)

    def with_prev(a):
        prev = jnp.pad(a, ((0, 0), (0, 0), (1, 0), (0, 0), (0, 0), (0, 0)))[:, :, :-1]
        return jnp.concatenate([prev, a], axis=3)

    qs = to_sub(q)
    kk, vv = with_prev(to_sub(k)), with_prev(to_sub(v))
    s = jnp.einsum('brnqhe,brnkhe->brnhqk', qs, kk, preferred_element_type=jnp.float32)
    i = jnp.arange(BAND_BLOCK)[:, None]
    j = jnp.arange(2 * BAND_BLOCK)[None, :]
    dist = BAND_BLOCK + i - j
    blk = jnp.arange(nb)[:, None, None]
    valid = (dist >= 0) & (dist <= n_back) & ((blk > 0) | (j >= BAND_BLOCK))
    s = jnp.where(valid[:, None], s, -jnp.inf)
    m = jnp.max(s, axis=-1)
    p = jnp.exp(s - m[..., None])
    den = jnp.sum(p, axis=-1)
    o = jnp.einsum('brnhqk,brnkhe->brnqhe', p, vv.astype(jnp.float32))
    m, den = jnp.swapaxes(m, 3, 4), jnp.swapaxes(den, 3, 4)
    o = o / den[..., None]
    o = o.reshape(B, dil, Lp, H, E)[:, :, :Ls].transpose(0, 2, 1, 3, 4).reshape(B, T, H, E)
    back = lambda a: a.reshape(B, dil, Lp, H)[:, :, :Ls].transpose(0, 2, 1, 3).reshape(B, T, H)
    return o, back(m), back(den)


def dilated_group_sample(q, k_all, v_all, n_buf, dil, n_back):
    Tn = q.shape[1]
    idx = n_buf + jnp.arange(Tn)[:, None] - dil * jnp.arange(n_back + 1)[None, :]
    valid = idx >= 0
    idc = jnp.clip(idx, 0)
    kg = k_all[:, idc]
    vg = v_all[:, idc]
    s = jnp.einsum('nthe,ntkhe->nthk', q, kg, preferred_element_type=jnp.float32)
    s = jnp.where(valid[None, :, None, :], s, -jnp.inf)
    m = jnp.max(s, axis=-1)
    p = jnp.exp(s - m[..., None])
    den = jnp.sum(p, axis=-1)
    o = jnp.einsum('nthk,ntkhe->nthe', p, vg.astype(jnp.float32)) / den[..., None]
    return o, m, den


def merge_by_denominator(outs, ms, dens):
    outs, ms, dens = jnp.stack(outs), jnp.stack(ms), jnp.stack(dens)
    w = dens * jnp.exp(ms - jnp.max(ms, axis=0, keepdims=True))
    return jnp.sum(w[..., None] * outs, axis=0) / jnp.sum(w, axis=0)[..., None]


def dilated_attention_prompt(q, k, v):
    res = [dilated_group_prompt(q, k, v, dil, win // dil) for win, dil in DILATED_CONFIGS]
    return merge_by_denominator([r[0] for r in res], [r[1] for r in res], [r[2] for r in res])


def dilated_attention_sample(q, k, v, buf_k, buf_v):
    n_buf = buf_k.shape[1]
    k_all = jnp.concatenate([buf_k.astype(k.dtype), k], axis=1)
    v_all = jnp.concatenate([buf_v.astype(v.dtype), v], axis=1)
    res = [dilated_group_sample(q, k_all, v_all, n_buf, dil, win // dil) for win, dil in DILATED_CONFIGS]
    return merge_by_denominator([r[0] for r in res], [r[1] for r in res], [r[2] for r in res])


def gla_chunked(q, k, v, log_a, S0):
    N, T, H, K = q.shape
    C = math.gcd(T, GLA_CHUNK)
    n = T // C
    chunks = lambda a: a.reshape(N, n, C, H, a.shape[-1]).transpose(1, 0, 3, 2, 4)
    causal = jnp.tril(jnp.ones((C, C), dtype=bool))

    def step(S, inp):
        qi, ki, vi, li = inp
        b = jnp.cumsum(li, axis=2)
        b_end = b[:, :, -1:]
        q_t = qi * jnp.exp(b)
        A = jnp.einsum('nhik,nhjk->nhij', q_t, ki * jnp.exp(-b))
        A = jnp.where(causal, A, 0.0)
        o = jnp.einsum('nhij,nhjv->nhiv', A, vi) + jnp.einsum('nhik,nhkv->nhiv', q_t, S)
        S_new = jnp.exp(b_end[:, :, 0])[..., None] * S + jnp.einsum('nhjk,nhjv->nhkv', ki * jnp.exp(b_end - b), vi)
        return S_new, o

    S, o = lax.scan(step, S0, (chunks(q), chunks(k), chunks(v), chunks(log_a)))
    o = o.transpose(1, 0, 3, 2, 4).reshape(N, T, H, v.shape[-1])
    return o, S


def memory_kv(mem, g_mem, w_k, w_v):
    N = mem.shape[0]
    hm = rms_norm(mem, g_mem)
    mk = (hm @ w_k).reshape(N, MEM_LEN, MEM_HEADS, MEM_HEAD_DIM)
    mv = (hm @ w_v).reshape(N, MEM_LEN, MEM_HEADS, MEM_HEAD_DIM)
    return mk, mv


def cross_attend(h, mem_k, mem_v, w_q, w_o):
    N, T, _ = h.shape
    q = (h @ w_q).reshape(N, T, MEM_HEADS, MEM_HEAD_DIM) * MEM_HEAD_DIM ** -0.5
    s = jnp.einsum('nthe,nmhe->nhtm', q, mem_k.astype(q.dtype), preferred_element_type=jnp.float32)
    p = jax.nn.softmax(s, axis=-1)
    o = jnp.einsum('nhtm,nmhe->nthe', p, mem_v.astype(jnp.float32)).astype(h.dtype)
    return o.reshape(N, T, MEM_WIDTH) @ w_o


def decoder_layer(x, pos, attend, gla_S0, mem_k, mem_v,
                  g_ffn1, w1_gate, w1_up, w1_down, g_mix, w_in, w_gla_a2, b_gla_a, g_gla_out, w_out,
                  g_cross, w_cross_q, w_cross_o, g_ffn2, w2_gate, w2_up, w2_down):
    N, T, _ = x.shape
    f32 = jnp.float32
    x = x + 0.5 * swiglu(rms_norm(x, g_ffn1), w1_gate, w1_up, w1_down)
    h = rms_norm(x, g_mix)
    z = h @ w_in
    offsets = np.cumsum(IN_SPLIT_SIZES)[:-1].tolist()
    qa, ka, va, qg, kg, vg, g_lr, rg = jnp.split(z, offsets, axis=-1)
    qa = rope_partial(qa.reshape(N, T, ATT_HEADS, ATT_HEAD_DIM), pos) * ATT_HEAD_DIM ** -0.5
    ka = rope_partial(ka.reshape(N, T, ATT_HEADS, ATT_HEAD_DIM), pos)
    va = va.reshape(N, T, ATT_HEADS, ATT_HEAD_DIM)
    o_att = attend(qa, ka, va)
    log_a = jax.nn.log_sigmoid((g_lr @ w_gla_a2 + b_gla_a).astype(f32)) / GLA_GATE_TAU
    o_gla, S = gla_chunked(
        qg.reshape(N, T, GLA_HEADS, GLA_HEAD_K).astype(f32) * GLA_HEAD_K ** -0.5,
        kg.reshape(N, T, GLA_HEADS, GLA_HEAD_K).astype(f32),
        vg.reshape(N, T, GLA_HEADS, GLA_HEAD_V).astype(f32),
        log_a.reshape(N, T, GLA_HEADS, GLA_HEAD_K),
        gla_S0.astype(f32))
    o_gla = o_gla * lax.rsqrt(jnp.mean(o_gla * o_gla, axis=-1, keepdims=True) + EPS)
    o_gla = (o_gla.reshape(N, T, GLA_VALUE_WIDTH) * g_gla_out.astype(f32)).astype(x.dtype) * jax.nn.silu(rg)
    mix = jnp.concatenate([o_att.reshape(N, T, ATT_WIDTH).astype(x.dtype), o_gla], axis=-1)
    x = x + mix @ w_out
    x = x + cross_attend(rms_norm(x, g_cross), mem_k, mem_v, w_cross_q, w_cross_o)
    x = x + 0.5 * swiglu(rms_norm(x, g_ffn2), w2_gate, w2_up, w2_down)
    return x, ka, va, S.astype(x.dtype)


def setup_inputs(seed: int = 0) -> dict:
    key = jax.random.key(seed)
    keys = list(jax.random.split(key, 32))
    nrm = lambda shape, scale: jax.random.normal(keys.pop(), shape, jnp.float32) * scale
    gain = lambda n: 1.0 + nrm((DEPTH, n), 0.02)
    n_buf = min(MAX_WINDOW, PAST_LEN)
    return {
        'x_prompt': nrm((BATCH, SEQ, D_MODEL), 1.0),
        'x_sample': nrm((DEC_BATCH, DEC_SEQ, D_MODEL), 1.0),
        'mem_prompt': nrm((BATCH, MEM_LEN, D_MODEL), 1.0),
        'cache_win_k': nrm((DEPTH, DEC_BATCH, n_buf, ATT_HEADS, ATT_HEAD_DIM), 1.0),
        'cache_win_v': nrm((DEPTH, DEC_BATCH, n_buf, ATT_HEADS, ATT_HEAD_DIM), 1.0),
        'state_gla': nrm((DEPTH, DEC_BATCH, GLA_HEADS, GLA_HEAD_K, GLA_HEAD_V), 0.5),
        'cache_mem_k': nrm((DEPTH, DEC_BATCH, MEM_LEN, MEM_HEADS, MEM_HEAD_DIM), 1.0),
        'cache_mem_v': nrm((DEPTH, DEC_BATCH, MEM_LEN, MEM_HEADS, MEM_HEAD_DIM), 1.0),
        'g_ffn1': gain(D_MODEL),
        'w1_gate': nrm((DEPTH, D_MODEL, D_FF), D_MODEL ** -0.5),
        'w1_up': nrm((DEPTH, D_MODEL, D_FF), D_MODEL ** -0.5),
        'w1_down': nrm((DEPTH, D_FF, D_MODEL), D_FF ** -0.5),
        'g_mix': gain(D_MODEL),
        'w_in': nrm((DEPTH, D_MODEL, IN_COLS), D_MODEL ** -0.5),
        'w_gla_a2': nrm((DEPTH, GLA_GATE_RANK, GLA_KEY_WIDTH), GLA_GATE_RANK ** -0.5),
        'b_gla_a': nrm((DEPTH, GLA_KEY_WIDTH), 0.1),
        'g_gla_out': gain(GLA_VALUE_WIDTH),
        'w_out': nrm((DEPTH, MIX_WIDTH, D_MODEL), MIX_WIDTH ** -0.5),
        'g_mem': gain(D_MODEL),
        'w_mem_k': nrm((DEPTH, D_MODEL, MEM_WIDTH), D_MODEL ** -0.5),
        'w_mem_v': nrm((DEPTH, D_MODEL, MEM_WIDTH), D_MODEL ** -0.5),
        'g_cross': gain(D_MODEL),
        'w_cross_q': nrm((DEPTH, D_MODEL, MEM_WIDTH), D_MODEL ** -0.5),
        'w_cross_o': nrm((DEPTH, MEM_WIDTH, D_MODEL), MEM_WIDTH ** -0.5),
        'g_ffn2': gain(D_MODEL),
        'w2_gate': nrm((DEPTH, D_MODEL, D_FF), D_MODEL ** -0.5),
        'w2_up': nrm((DEPTH, D_MODEL, D_FF), D_MODEL ** -0.5),
        'w2_down': nrm((DEPTH, D_FF, D_MODEL), D_FF ** -0.5),
        'g_final': 1.0 + nrm((D_MODEL,), 0.02),
    }


def reference(x_prompt, x_sample, mem_prompt, cache_win_k, cache_win_v, state_gla, cache_mem_k, cache_mem_v,
              g_ffn1, w1_gate, w1_up, w1_down, g_mix, w_in, w_gla_a2, b_gla_a, g_gla_out, w_out,
              g_mem, w_mem_k, w_mem_v, g_cross, w_cross_q, w_cross_o, g_ffn2, w2_gate, w2_up, w2_down, g_final):
    T = x_prompt.shape[1]
    Tn = x_sample.shape[1]
    n_keep = min(MAX_WINDOW, T)
    pos_prompt = jnp.arange(T)
    pos_sample = PAST_LEN + jnp.arange(Tn)
    yp, ys = x_prompt, x_sample
    wk_p, wv_p, st_p, mk_p, mv_p, wk_s, wv_s, st_s = ([] for _ in range(8))
    for l in range(DEPTH):
        lw = (g_ffn1[l], w1_gate[l], w1_up[l], w1_down[l], g_mix[l], w_in[l], w_gla_a2[l], b_gla_a[l],
              g_gla_out[l], w_out[l], g_cross[l], w_cross_q[l], w_cross_o[l],
              g_ffn2[l], w2_gate[l], w2_up[l], w2_down[l])
        mem_k_l, mem_v_l = memory_kv(mem_prompt, g_mem[l], w_mem_k[l], w_mem_v[l])
        s0 = jnp.zeros((yp.shape[0], GLA_HEADS, GLA_HEAD_K, GLA_HEAD_V), jnp.float32)
        yp, kp, vp, sp = decoder_layer(yp, pos_prompt, dilated_attention_prompt, s0, mem_k_l, mem_v_l, *lw)
        attend_s = functools.partial(dilated_attention_sample, buf_k=cache_win_k[l], buf_v=cache_win_v[l])
        ys, ks, vs, ss = decoder_layer(ys, pos_sample, attend_s, state_gla[l], cache_mem_k[l], cache_mem_v[l], *lw)
        wk_p.append(kp[:, T - n_keep:])
        wv_p.append(vp[:, T - n_keep:])
        st_p.append(sp)
        mk_p.append(mem_k_l)
        mv_p.append(mem_v_l)
        wk_s.append(ks)
        wv_s.append(vs)
        st_s.append(ss)
    return (rms_norm(yp, g_final), rms_norm(ys, g_final),
            jnp.stack(wk_p), jnp.stack(wv_p), jnp.stack(st_p), jnp.stack(mk_p), jnp.stack(mv_p),
            jnp.stack(wk_s), jnp.stack(wv_s), jnp.stack(st_s))
```

```python
import functools

import jax
import jax.numpy as jnp
import numpy as np
from jax import lax
from jax.experimental import pallas as pl
from jax.experimental.pallas import tpu as pltpu

F32 = jnp.float32
BF16 = jnp.bfloat16

EPS = 1e-6
ATT_HEADS = 16
ATT_HEAD_DIM = 64
ATT_WIDTH = ATT_HEADS * ATT_HEAD_DIM
DILATIONS = (1, 4, 16)
N_BACK = 128
BAND = 128
ROT_DIM = 16
ROPE_THETA = 500000.0
GLA_HEADS = 4
GLA_HEAD_K = 128
GLA_HEAD_V = 256
GLA_KEY_WIDTH = GLA_HEADS * GLA_HEAD_K
GLA_VALUE_WIDTH = GLA_HEADS * GLA_HEAD_V
GLA_GATE_RANK = 16
GLA_GATE_TAU = 16.0
GLA_CHUNK = 64
MEM_LEN = 256
MEM_HEADS = 4
MEM_HEAD_DIM = 128
MEM_WIDTH = MEM_HEADS * MEM_HEAD_DIM
MAX_WINDOW = 2048
LANES = 128
SAMPLE_GROUP = 8
Z_COLS = 3 * ATT_WIDTH + 2 * GLA_KEY_WIDTH + 2 * GLA_VALUE_WIDTH
Z_TILE = 1024
VMEM_LIMIT_BYTES = 56 * 1024 * 1024


def _params(*semantics):
    return pltpu.CompilerParams(dimension_semantics=semantics, vmem_limit_bytes=VMEM_LIMIT_BYTES)


def _rms(x, g):
    return x * lax.rsqrt(jnp.mean(x * x, axis=-1, keepdims=True) + EPS) * g


def _dot(a, b):
    return jnp.dot(a, b, preferred_element_type=F32)


def _dot_nt(a, b):
    return lax.dot_general(a, b, (((1,), (1,)), ((), ())), preferred_element_type=F32)


def _dot_tn(a, b):
    return lax.dot_general(a, b, (((0,), (0,)), ((), ())), preferred_element_type=F32)


def _split_bf16(x):
    hi = x.astype(BF16)
    lo = (x - hi.astype(F32)).astype(BF16)
    return hi, lo


def _ffn_body(*refs, nf, final):
    if final:
        x_ref, g_ref, wg_ref, wu_ref, wd_ref, gf_ref, o_ref, h_ref = refs
    else:
        x_ref, g_ref, wg_ref, wu_ref, wd_ref, o_ref, h_ref = refs
    f = pl.program_id(1)

    @pl.when(f == 0)
    def _():
        h_ref[...] = _rms(x_ref[...], g_ref[...]).astype(BF16)
        o_ref[...] = jnp.zeros_like(o_ref)

    h = h_ref[...]
    gate = _dot(h, wg_ref[...])
    up = _dot(h, wu_ref[...])
    act = (gate * jax.nn.sigmoid(gate) * up).astype(BF16)
    o_ref[...] += _dot(act, wd_ref[...])

    @pl.when(f == nf - 1)
    def _():
        y = x_ref[...] + 0.5 * o_ref[...]
        if final:
            y = _rms(y, gf_ref[...])
        o_ref[...] = y


def _ffn(x, g, wg, wu, wd, g_final=None, *, tm=512, tf=512):
    m, d = x.shape
    dff = wg.shape[1]
    tm = min(tm, m)
    nf = dff // tf
    final = g_final is not None
    row = pl.BlockSpec((1, d), lambda i, f: (0, 0))
    in_specs = [
        pl.BlockSpec((tm, d), lambda i, f: (i, 0)),
        row,
        pl.BlockSpec((d, tf), lambda i, f: (0, f)),
        pl.BlockSpec((d, tf), lambda i, f: (0, f)),
        pl.BlockSpec((tf, d), lambda i, f: (f, 0)),
    ]
    args = [x, g.reshape(1, d), wg, wu, wd]
    if final:
        in_specs.append(row)
        args.append(g_final.reshape(1, d))
    return pl.pallas_call(
        functools.partial(_ffn_body, nf=nf, final=final),
        grid=(m // tm, nf),
        in_specs=in_specs,
        out_specs=pl.BlockSpec((tm, d), lambda i, f: (i, 0)),
        out_shape=jax.ShapeDtypeStruct((m, d), F32),
        scratch_shapes=[pltpu.VMEM((tm, d), BF16)],
        compiler_params=_params("parallel", "arbitrary"),
        name="ffn_final" if final else "ffn",
    )(*args)


def _mix_in_body(x_ref, g_ref, w_ref, wlr_ref, wa2_ref, ba_ref, rope_ref, z_ref, la_ref, h_ref):
    n = pl.program_id(1)

    @pl.when(n == 0)
    def _():
        h = _rms(x_ref[...], g_ref[...]).astype(BF16)
        h_ref[...] = h
        g_lr = _dot(h, wlr_ref[...])
        pre = _dot(g_lr.astype(BF16), wa2_ref[...]) + ba_ref[...]
        la_ref[...] = (jnp.minimum(pre, 0.0) - jnp.log1p(jnp.exp(-jnp.abs(pre)))) * (1.0 / GLA_GATE_TAU)

    z = _dot(h_ref[...], w_ref[...])

    @pl.when(n >= 2)
    def _():
        z_ref[...] = z

    @pl.when(n < 2)
    def _():
        cos, sin_up, sin_dn = rope_ref[0], rope_ref[1], rope_ref[2]
        scale = jnp.where(n == 0, ATT_HEAD_DIM ** -0.5, 1.0).astype(F32)
        half = ROT_DIM // 2
        for j in range(Z_TILE // LANES):
            zc = z[:, j * LANES:(j + 1) * LANES]
            rot = zc * cos + pltpu.roll(zc, half, axis=1) * sin_up + pltpu.roll(zc, LANES - half, axis=1) * sin_dn
            z_ref[:, j * LANES:(j + 1) * LANES] = rot * scale


def _mix_in(x, g, w_main, w_lr, w_a2, b_a, rope, *, tm=512):
    m, d = x.shape
    tm = min(tm, m)
    n_rope = rope.shape[1] // tm
    return pl.pallas_call(
        _mix_in_body,
        grid=(m // tm, Z_COLS // Z_TILE),
        in_specs=[
            pl.BlockSpec((tm, d), lambda i, n: (i, 0)),
            pl.BlockSpec((1, d), lambda i, n: (0, 0)),
            pl.BlockSpec((d, Z_TILE), lambda i, n: (0, n)),
            pl.BlockSpec((d, LANES), lambda i, n: (0, 0)),
            pl.BlockSpec((LANES, GLA_KEY_WIDTH), lambda i, n: (0, 0)),
            pl.BlockSpec((1, GLA_KEY_WIDTH), lambda i, n: (0, 0)),
            pl.BlockSpec((3, tm, LANES), lambda i, n: (0, i % n_rope, 0)),
        ],
        out_specs=[
            pl.BlockSpec((tm, Z_TILE), lambda i, n: (i, n)),
            pl.BlockSpec((tm, GLA_KEY_WIDTH), lambda i, n: (i, 0)),
        ],
        out_shape=[
            jax.ShapeDtypeStruct((m, Z_COLS), F32),
            jax.ShapeDtypeStruct((m, GLA_KEY_WIDTH), F32),
        ],
        scratch_shapes=[pltpu.VMEM((tm, d), BF16)],
        compiler_params=_params("parallel", "arbitrary"),
        name="mix_in",
    )(x, g.reshape(1, d), w_main, w_lr, w_a2, b_a.reshape(1, -1), rope)


def _rope_tables(pos):
    half = ROT_DIM // 2
    inv = ROPE_THETA ** (-jnp.arange(half, dtype=F32) * 2.0 / ROT_DIM)
    ang = pos.astype(F32)[:, None] * inv[None, :]
    cos, sin = jnp.cos(ang), jnp.sin(ang)
    t = pos.shape[0]
    zeros = jnp.zeros((t, half), F32)
    rest0 = jnp.zeros((t, ATT_HEAD_DIM - ROT_DIM), F32)
    c = jnp.concatenate([cos, cos, jnp.ones_like(rest0)], axis=1)
    s_up = jnp.concatenate([zeros, sin, rest0], axis=1)
    s_dn = jnp.concatenate([-sin, zeros, rest0], axis=1)
    reps = LANES // ATT_HEAD_DIM
    return jnp.stack([jnp.tile(c, (1, reps)), jnp.tile(s_up, (1, reps)), jnp.tile(s_dn, (1, reps))])


def _norm_proj_body(x_ref, g_ref, w_ref, o_ref):
    h = _rms(x_ref[...], g_ref[...]).astype(BF16)
    o_ref[...] = _dot(h, w_ref[...])


def _norm_proj(x, g, w, *, tm=512):
    m, d = x.shape
    n = w.shape[1]
    tm = min(tm, m)
    return pl.pallas_call(
        _norm_proj_body,
        grid=(m // tm,),
        in_specs=[
            pl.BlockSpec((tm, d), lambda i: (i, 0)),
            pl.BlockSpec((1, d), lambda i: (0, 0)),
            pl.BlockSpec((d, n), lambda i: (0, 0)),
        ],
        out_specs=pl.BlockSpec((tm, n), lambda i: (i, 0)),
        out_shape=jax.ShapeDtypeStruct((m, n), F32),
        compiler_params=_params("parallel"),
        name="memory_kv",
    )(x, g.reshape(1, d), w)


def _attn_prompt_body(q_ref, k_ref, v_ref, o_ref, acc_ref, m_ref, l_ref, *, seq):
    hd = ATT_HEAD_DIM
    d0 = (lax.broadcasted_iota(jnp.int32, (BAND, 2 * BAND), 0)
          - lax.broadcasted_iota(jnp.int32, (BAND, 2 * BAND), 1))

    for ci, dil in enumerate(DILATIONS):
        nb = seq // dil // BAND

        def block(r, n, ci=ci, dil=dil):
            base = jnp.maximum(n - 1, 0) * BAND
            dist = d0 + (n * BAND - base)
            valid = (dist >= 0) & (dist <= N_BACK)
            if dil == 1:
                q_rows = pl.ds(pl.multiple_of(n * BAND, BAND), BAND)
                k_rows = pl.ds(pl.multiple_of(base, BAND), 2 * BAND)
            else:
                q_rows = pl.ds(r + dil * BAND * n, BAND, stride=dil)
                k_rows = pl.ds(r + dil * base, 2 * BAND, stride=dil)
            q = q_ref[0, q_rows, :]
            k = k_ref[0, k_rows, :]
            v = v_ref[0, k_rows, :]
            pv, mx, den = [], [], []
            for hh in range(LANES // hd):
                sl = slice(hh * hd, (hh + 1) * hd)
                s = _dot_nt(q[:, sl].astype(BF16), k[:, sl].astype(BF16))
                s = jnp.where(valid, s, -jnp.inf)
                m_c = jnp.max(s, axis=1, keepdims=True)
                p = jnp.exp(s - m_c)
                den.append(jnp.broadcast_to(jnp.sum(p, axis=1, keepdims=True), (BAND, hd)))
                mx.append(jnp.broadcast_to(m_c, (BAND, hd)))
                pv.append(_dot(p.astype(BF16), v[:, sl].astype(BF16)))
            pv = jnp.concatenate(pv, axis=1)
            mx = jnp.concatenate(mx, axis=1)
            den = jnp.concatenate(den, axis=1)
            if ci == 0:
                acc_ref[q_rows, :] = pv
                m_ref[q_rows, :] = mx
                l_ref[q_rows, :] = den
            else:
                m_old = m_ref[q_rows, :]
                m_new = jnp.maximum(m_old, mx)
                a_old = jnp.exp(m_old - m_new)
                a_new = jnp.exp(mx - m_new)
                acc_ref[q_rows, :] = acc_ref[q_rows, :] * a_old + pv * a_new
                l_ref[q_rows, :] = l_ref[q_rows, :] * a_old + den * a_new
                m_ref[q_rows, :] = m_new

        def residue(r, carry, nb=nb, block=block):
            def step(n, c):
                block(r, n)
                return c
            return lax.fori_loop(0, nb, step, carry)

        lax.fori_loop(0, dil, residue, 0)

    o_ref[0] = (acc_ref[...] / l_ref[...]).astype(o_ref.dtype)


def _attn_prompt(z3):
    b, seq, _ = z3.shape
    pairs = ATT_WIDTH // LANES
    assert seq % (DILATIONS[-1] * BAND) == 0 and seq // DILATIONS[-1] >= 2 * BAND
    blk = lambda off: pl.BlockSpec((1, seq, LANES), lambda i, p: (i, 0, off + p))
    return pl.pallas_call(
        functools.partial(_attn_prompt_body, seq=seq),
        grid=(b, pairs),
        in_specs=[blk(0), blk(pairs), blk(2 * pairs)],
        out_specs=pl.BlockSpec((1, seq, LANES), lambda i, p: (i, 0, p)),
        out_shape=jax.ShapeDtypeStruct((b, seq, ATT_WIDTH), BF16),
        scratch_shapes=[pltpu.VMEM((seq, LANES), F32)] * 3,
        compiler_params=_params("parallel", "parallel"),
        name="attn_prompt",
    )(z3, z3, z3)


def _attn_sample_body(q_ref, kn_ref, vn_ref, kt_ref, k16_ref, vt_ref, v16_ref, o_ref, *, n_new, n_tail):
    width = ATT_WIDTH
    th = n_new * ATT_HEADS
    q = q_ref[0]
    row = lax.broadcasted_iota(jnp.int32, (th, width), 0)
    lane = lax.broadcasted_iota(jnp.int32, (th, width), 1)
    head_mask = (row % ATT_HEADS) == (lane // ATT_HEAD_DIM)
    q_rep = jnp.concatenate([jnp.broadcast_to(q[t:t + 1], (ATT_HEADS, width)) for t in range(n_new)], axis=0)
    q_blk = jnp.where(head_mask, q_rep, 0.0).astype(BF16)
    col_t = lax.broadcasted_iota(jnp.int32, (1, th), 1) // ATT_HEADS

    pad = 16 - n_new
    zeros_pad = jnp.zeros((pad, width), F32)
    i_t = lax.broadcasted_iota(jnp.int32, (n_tail, 1), 0)
    i_n = lax.broadcasted_iota(jnp.int32, (16, 1), 0)
    d16 = DILATIONS[2]
    mult = [((i_t >= n_tail - N_BACK + col_t).astype(F32) + (((i_t - col_t) & (DILATIONS[1] - 1)) == 0).astype(F32))]
    keys = [kt_ref[0]]
    vals = [vt_ref[0]]
    for t in range(n_new):
        mult.append(jnp.broadcast_to((col_t == t).astype(F32), (k16_ref.shape[1], th)))
        keys.append(k16_ref[0, :, t * width:(t + 1) * width])
        vals.append(v16_ref[0, :, t * width:(t + 1) * width])
    mult.append(jnp.where(i_n < n_new, (i_n <= col_t).astype(F32) + 2.0 * (i_n == col_t).astype(F32), 0.0))
    keys.append(jnp.concatenate([kn_ref[0], zeros_pad], axis=0))
    vals.append(jnp.concatenate([vn_ref[0], zeros_pad], axis=0))

    scores = [jnp.where(mu > 0.0, _dot_nt(kk.astype(BF16), q_blk), -jnp.inf) for kk, mu in zip(keys, mult)]
    m = functools.reduce(jnp.maximum, [jnp.max(s, axis=0, keepdims=True) for s in scores])
    o_full = jnp.zeros((th, width), F32)
    den = jnp.zeros((th, LANES), F32)
    for s, mu, vv in zip(scores, mult, vals):
        w = (mu * jnp.exp(s - m)).astype(BF16)
        o_full = o_full + _dot_tn(w, vv.astype(BF16))
        den = den + _dot_tn(w, jnp.ones((w.shape[0], LANES), BF16))
    o_n = jnp.concatenate([o_full[:, c * LANES:(c + 1) * LANES] / den for c in range(width // LANES)], axis=1)
    o_n = jnp.where(head_mask, o_n, 0.0)
    o_ref[0] = jnp.concatenate(
        [jnp.sum(o_n[t * ATT_HEADS:(t + 1) * ATT_HEADS], axis=0, keepdims=True) for t in range(n_new)], axis=0
    ).astype(o_ref.dtype)


def _attn_sample(z3, cache_k, cache_v):
    n, n_new, _ = z3.shape
    n_buf = cache_k.shape[1]
    w = ATT_WIDTH
    d4, d16 = DILATIONS[1], DILATIONS[2]
    n_tail = d4 * N_BACK
    assert n_buf == d16 * N_BACK and n_new == d4 and n_buf % n_tail == 0
    tail = lambda c: c.reshape(n, n_buf, w)
    strided = lambda c: c.reshape(n, n_buf // d16, d16 * w)
    new = lambda j: pl.BlockSpec((1, n_new, w), lambda i: (i, 0, j))
    tail_spec = pl.BlockSpec((1, n_tail, w), lambda i: (i, n_buf // n_tail - 1, 0))
    strided_spec = pl.BlockSpec((1, n_buf // d16, n_new * w), lambda i: (i, 0, 0))
    return pl.pallas_call(
        functools.partial(_attn_sample_body, n_new=n_new, n_tail=n_tail),
        grid=(n,),
        in_specs=[new(0), new(1), new(2), tail_spec, strided_spec, tail_spec, strided_spec],
        out_specs=pl.BlockSpec((1, n_new, w), lambda i: (i, 0, 0)),
        out_shape=jax.ShapeDtypeStruct((n, n_new, w), F32),
        compiler_params=_params("parallel"),
        name="attn_sample",
    )(z3, z3, z3, tail(cache_k), strided(cache_k), tail(cache_v), strided(cache_v))


def _gla_rows(q, k, v, li, states, causal, same, group_of_row):
    c = q.shape[0]
    hi, lo = _split_bf16(li)
    tri = causal.astype(BF16)
    b = _dot(tri, hi) + _dot(tri, lo)
    if len(states) == 1:
        b_end = b[c - 1:c, :]
    else:
        full = same.astype(BF16)
        b_end = _dot(full, hi) + _dot(full, lo)
    q_t = (q * (GLA_HEAD_K ** -0.5) * jnp.exp(b)).astype(BF16)
    k_n = (k * jnp.exp(-b)).astype(BF16)
    k_w = k * jnp.exp(b_end - b)
    vb = v.astype(BF16)
    a = jnp.where(causal, _dot_nt(q_t, k_n), 0.0)
    o = _dot(a.astype(BF16), vb)
    ones = jnp.ones((c, LANES), BF16)
    new_states = []
    for gi, s in enumerate(states):
        if len(states) == 1:
            o = o + _dot(q_t, s.astype(BF16))
            kw_g, hi_g, lo_g = k_w.astype(BF16), hi, lo
        else:
            mine = group_of_row == gi
            o = o + jnp.where(mine, _dot(q_t, s.astype(BF16)), 0.0)
            kw_g = jnp.where(mine, k_w, 0.0).astype(BF16)
            hi_g = jnp.where(mine, hi, jnp.zeros_like(hi))
            lo_g = jnp.where(mine, lo, jnp.zeros_like(lo))
        decay = jnp.exp(_dot_tn(hi_g, ones) + _dot_tn(lo_g, ones))
        decay = jnp.concatenate([decay] * (s.shape[1] // LANES), axis=1)
        new_states.append(decay * s + _dot_tn(kw_g, vb))
    return o, new_states


def _gla_finish(o, g_out, rg):
    o = o * lax.rsqrt(jnp.mean(o * o, axis=-1, keepdims=True) + EPS)
    return (o * g_out) * (rg * jax.nn.sigmoid(rg))


def _gla_prompt_body(q_ref, k_ref, v_ref, rg_ref, la_ref, g_ref, o_ref, s_ref, *, seq, chunk):
    r = lax.broadcasted_iota(jnp.int32, (chunk, chunk), 0)
    c = lax.broadcasted_iota(jnp.int32, (chunk, chunk), 1)
    causal = c <= r
    s_ref[0, 0] = jnp.zeros(s_ref.shape[2:], F32)

    def step(i, carry):
        rows = pl.ds(pl.multiple_of(i * chunk, chunk), chunk)
        o, (s_new,) = _gla_rows(q_ref[0, rows, :], k_ref[0, rows, :], v_ref[0, rows, :], la_ref[0, rows, :],
                                [s_ref[0, 0]], causal, None, None)
        s_ref[0, 0] = s_new
        o_ref[0, rows, :] = _gla_finish(o, g_ref[...], rg_ref[0, rows, :]).astype(o_ref.dtype)
        return carry

    lax.fori_loop(0, seq // chunk, step, 0)


def _gla_prompt(z3, la3, g_out):
    b, seq, _ = z3.shape
    hk, hv, nh = GLA_HEAD_K, GLA_HEAD_V, GLA_HEADS
    chunk = int(np.gcd(seq, GLA_CHUNK))
    q0 = 3 * ATT_WIDTH // hk
    k0 = q0 + nh
    v0 = (3 * ATT_WIDTH + 2 * GLA_KEY_WIDTH) // hv
    r0 = v0 + nh
    return pl.pallas_call(
        functools.partial(_gla_prompt_body, seq=seq, chunk=chunk),
        grid=(b, nh),
        in_specs=[
            pl.BlockSpec((1, seq, hk), lambda i, h: (i, 0, q0 + h)),
            pl.BlockSpec((1, seq, hk), lambda i, h: (i, 0, k0 + h)),
            pl.BlockSpec((1, seq, hv), lambda i, h: (i, 0, v0 + h)),
            pl.BlockSpec((1, seq, hv), lambda i, h: (i, 0, r0 + h)),
            pl.BlockSpec((1, seq, hk), lambda i, h: (i, 0, h)),
            pl.BlockSpec((1, hv), lambda i, h: (0, h)),
        ],
        out_specs=[
            pl.BlockSpec((1, seq, hv), lambda i, h: (i, 0, h)),
            pl.BlockSpec((1, 1, hk, hv), lambda i, h: (i, h, 0, 0)),
        ],
        out_shape=[
            jax.ShapeDtypeStruct((b, seq, GLA_VALUE_WIDTH), BF16),
            jax.ShapeDtypeStruct((b, nh, hk, hv), F32),
        ],
        compiler_params=_params("parallel", "parallel"),
        name="gla_prompt",
    )(z3, z3, z3, z3, la3, g_out.reshape(1, -1))


def _gla_sample_body(q_ref, k_ref, v_ref, rg_ref, la_ref, g_ref, s0_ref, o_ref, s_ref, *, n_new, groups):
    rows = n_new * groups
    r = lax.broadcasted_iota(jnp.int32, (rows, rows), 0)
    c = lax.broadcasted_iota(jnp.int32, (rows, rows), 1)
    same = (r // n_new) == (c // n_new)
    causal = same & (c <= r)
    group_of_row = lax.broadcasted_iota(jnp.int32, (rows, 1), 0) // n_new
    o, new_states = _gla_rows(q_ref[...], k_ref[...], v_ref[...], la_ref[...],
                              [s0_ref[gi, 0] for gi in range(groups)], causal, same, group_of_row)
    for gi in range(groups):
        s_ref[gi, 0] = new_states[gi]
    o_ref[...] = _gla_finish(o, g_ref[...], rg_ref[...]).astype(o_ref.dtype)


def _gla_sample(z, la, g_out, state, n_new):
    m = z.shape[0]
    n = m // n_new
    hk, hv, nh = GLA_HEAD_K, GLA_HEAD_V, GLA_HEADS
    groups = min(SAMPLE_GROUP, n)
    rows = groups * n_new
    q0 = 3 * ATT_WIDTH // hk
    k0 = q0 + nh
    v0 = (3 * ATT_WIDTH + 2 * GLA_KEY_WIDTH) // hv
    r0 = v0 + nh
    state_spec = pl.BlockSpec((groups, 1, hk, hv), lambda i, h: (i, h, 0, 0))
    return pl.pallas_call(
        functools.partial(_gla_sample_body, n_new=n_new, groups=groups),
        grid=(n // groups, nh),
        in_specs=[
            pl.BlockSpec((rows, hk), lambda i, h: (i, q0 + h)),
            pl.BlockSpec((rows, hk), lambda i, h: (i, k0 + h)),
            pl.BlockSpec((rows, hv), lambda i, h: (i, v0 + h)),
            pl.BlockSpec((rows, hv), lambda i, h: (i, r0 + h)),
            pl.BlockSpec((rows, hk), lambda i, h: (i, h)),
            pl.BlockSpec((1, hv), lambda i, h: (0, h)),
            state_spec,
        ],
        out_specs=[pl.BlockSpec((rows, hv), lambda i, h: (i, h)), state_spec],
        out_shape=[
            jax.ShapeDtypeStruct((m, GLA_VALUE_WIDTH), BF16),
            jax.ShapeDtypeStruct(state.shape, F32),
        ],
        compiler_params=_params("parallel", "parallel"),
        name="gla_sample",
    )(z, z, z, z, la, g_out.reshape(1, -1), state)


def _out_cross_q_body(x_ref, oa_ref, og_ref, wa_ref, wg_ref, g_ref, wq_ref, x2_ref, qc_ref):
    oa = oa_ref[...].astype(BF16)
    x2 = x_ref[...] + _dot(oa, wa_ref[...]) + _dot(og_ref[...], wg_ref[...])
    x2_ref[...] = x2
    h = _rms(x2, g_ref[...]).astype(BF16)
    qc_ref[...] = (_dot(h, wq_ref[...]) * (MEM_HEAD_DIM ** -0.5)).astype(qc_ref.dtype)


def _out_cross_q(x, o_att, o_gla, w_out_att, w_out_gla, g_cross, w_q, *, tm=512):
    m, d = x.shape
    tm = min(tm, m)
    const = lambda shape: pl.BlockSpec(shape, lambda i: (0, 0))
    return pl.pallas_call(
        _out_cross_q_body,
        grid=(m // tm,),
        in_specs=[
            pl.BlockSpec((tm, d), lambda i: (i, 0)),
            pl.BlockSpec((tm, ATT_WIDTH), lambda i: (i, 0)),
            pl.BlockSpec((tm, GLA_VALUE_WIDTH), lambda i: (i, 0)),
            const(w_out_att.shape),
            const(w_out_gla.shape),
            const((1, d)),
            const(w_q.shape),
        ],
        out_specs=[pl.BlockSpec((tm, d), lambda i: (i, 0)), pl.BlockSpec((tm, MEM_WIDTH), lambda i: (i, 0))],
        out_shape=[jax.ShapeDtypeStruct((m, d), F32), jax.ShapeDtypeStruct((m, MEM_WIDTH), BF16)],
        compiler_params=_params("parallel"),
        name="out_cross_q",
    )(x, o_att, o_gla, w_out_att, w_out_gla, g_cross.reshape(1, d), w_q)


def _cross_body(q_ref, mk_ref, mv_ref, x_ref, w_ref, o_ref, *, groups):
    rows = q_ref.shape[0]
    per = rows // groups
    group_of_row = lax.broadcasted_iota(jnp.int32, (rows, 1), 0) // per
    hd = MEM_HEAD_DIM
    heads = []
    for h in range(MEM_HEADS):
        sl = slice(h * hd, (h + 1) * hd)
        qh = q_ref[:, sl]
        oh = None
        for gi in range(groups):
            s = _dot_nt(qh, mk_ref[gi, :, sl].astype(BF16))
            p = jnp.exp(s - jnp.max(s, axis=-1, keepdims=True))
            og = _dot(p.astype(BF16), mv_ref[gi, :, sl].astype(BF16)) / jnp.sum(p, axis=-1, keepdims=True)
            oh = og if oh is None else jnp.where(group_of_row == gi, og, oh)
        heads.append(oh)
    o = jnp.concatenate(heads, axis=1).astype(BF16)
    o_ref[...] = x_ref[...] + _dot(o, w_ref[...])


def _cross(qc, mem_k, mem_v, x, w_o, *, rows, groups, mem_index, k_col, v_col):
    m, d = x.shape
    mem = lambda col: pl.BlockSpec((groups, MEM_LEN, MEM_WIDTH), lambda i: (mem_index(i), 0, col))
    return pl.pallas_call(
        functools.partial(_cross_body, groups=groups),
        grid=(m // rows,),
        in_specs=[
            pl.BlockSpec((rows, MEM_WIDTH), lambda i: (i, 0)),
            mem(k_col),
            mem(v_col),
            pl.BlockSpec((rows, d), lambda i: (i, 0)),
            pl.BlockSpec(w_o.shape, lambda i: (0, 0)),
        ],
        out_specs=pl.BlockSpec((rows, d), lambda i: (i, 0)),
        out_shape=jax.ShapeDtypeStruct((m, d), F32),
        compiler_params=_params("parallel"),
        name="cross",
    )(qc, mem_k, mem_v, x, w_o)


def kernel(x_prompt, x_sample, mem_prompt, cache_win_k, cache_win_v, state_gla, cache_mem_k, cache_mem_v, g_ffn1, w1_gate, w1_up, w1_down, g_mix, w_in, w_gla_a2, b_gla_a, g_gla_out, w_out, g_mem, w_mem_k, w_mem_v, g_cross, w_cross_q, w_cross_o, g_ffn2, w2_gate, w2_up, w2_down, g_final):
    b, seq, d = x_prompt.shape
    n, n_new, _ = x_sample.shape
    depth = g_ffn1.shape[0]
    past_len = cache_win_k.shape[2]
    n_keep = min(MAX_WINDOW, seq)
    bf = lambda a: a.astype(BF16)

    xp = x_prompt.reshape(b * seq, d)
    xs = x_sample.reshape(n * n_new, d)
    rope_p = _rope_tables(jnp.arange(seq))
    rope_s = _rope_tables(jnp.tile(past_len + jnp.arange(n_new), n))
    outs = [[] for _ in range(8)]
    cols = np.cumsum([0, ATT_WIDTH, ATT_WIDTH, ATT_WIDTH, GLA_KEY_WIDTH, GLA_KEY_WIDTH, GLA_VALUE_WIDTH, GLA_GATE_RANK])
    for l in range(depth):
        w_main = bf(jnp.concatenate([w_in[l][:, :cols[6]], w_in[l][:, cols[7]:]], axis=1))
        w_lr = bf(jnp.pad(w_in[l][:, cols[6]:cols[7]], ((0, 0), (0, LANES - GLA_GATE_RANK))))
        w_a2 = bf(jnp.pad(w_gla_a2[l], ((0, LANES - GLA_GATE_RANK), (0, 0))))
        w_out_att, w_out_gla = bf(w_out[l][:ATT_WIDTH]), bf(w_out[l][ATT_WIDTH:])
        w_mem = bf(jnp.concatenate([w_mem_k[l], w_mem_v[l]], axis=1))
        w1 = (bf(w1_gate[l]), bf(w1_up[l]), bf(w1_down[l]))
        w2 = (bf(w2_gate[l]), bf(w2_up[l]), bf(w2_down[l]))
        w_cq, w_co = bf(w_cross_q[l]), bf(w_cross_o[l])
        last = l == depth - 1

        mem_kv = _norm_proj(mem_prompt.reshape(b * MEM_LEN, d), g_mem[l], w_mem).reshape(b, MEM_LEN, 2 * MEM_WIDTH)

        def layer(x, rope, attend, gla, cross):
            x = _ffn(x, g_ffn1[l], *w1)
            z, la = _mix_in(x, g_mix[l], w_main, w_lr, w_a2, b_gla_a[l], rope)
            o_att = attend(z)
            o_gla, state = gla(z, la)
            x, qc = _out_cross_q(x, o_att, o_gla, w_out_att, w_out_gla, g_cross[l], w_cq)
            x = cross(qc, x)
            x = _ffn(x, g_ffn2[l], *w2, g_final=g_final if last else None)
            return x, z, state

        tq = min(512, seq)
        xp, zp, st_p = layer(
            xp, rope_p,
            lambda z: _attn_prompt(z.reshape(b, seq, Z_COLS)).reshape(b * seq, ATT_WIDTH),
            lambda z, la: (lambda o, s: (o.reshape(b * seq, GLA_VALUE_WIDTH), s))(
                *_gla_prompt(z.reshape(b, seq, Z_COLS), la.reshape(b, seq, GLA_KEY_WIDTH), g_gla_out[l])),
            lambda qc, x: _cross(qc, mem_kv, mem_kv, x, w_co, rows=tq, groups=1,
                                 mem_index=lambda i: i // (seq // tq), k_col=0, v_col=1),
        )
        groups = min(SAMPLE_GROUP, n)
        xs, zs, st_s = layer(
            xs, rope_s,
            lambda z: _attn_sample(z.reshape(n, n_new, Z_COLS), cache_win_k[l], cache_win_v[l]).reshape(n * n_new, ATT_WIDTH),
            lambda z, la: _gla_sample(z, la, g_gla_out[l], state_gla[l], n_new),
            lambda qc, x: _cross(qc, cache_mem_k[l].reshape(n, MEM_LEN, MEM_WIDTH),
                                 cache_mem_v[l].reshape(n, MEM_LEN, MEM_WIDTH), x, w_co,
                                 rows=groups * n_new, groups=groups, mem_index=lambda i: i, k_col=0, v_col=0),
        )

        heads = lambda a, t: a.reshape(-1, t, ATT_HEADS, ATT_HEAD_DIM)
        zp3 = zp.reshape(b, seq, Z_COLS)
        outs[0].append(heads(zp3[:, seq - n_keep:, ATT_WIDTH:2 * ATT_WIDTH], n_keep))
        outs[1].append(heads(zp3[:, seq - n_keep:, 2 * ATT_WIDTH:3 * ATT_WIDTH], n_keep))
        outs[2].append(st_p)
        outs[3].append(mem_kv[:, :, :MEM_WIDTH].reshape(b, MEM_LEN, MEM_HEADS, MEM_HEAD_DIM))
        outs[4].append(mem_kv[:, :, MEM_WIDTH:].reshape(b, MEM_LEN, MEM_HEADS, MEM_HEAD_DIM))
        outs[5].append(heads(zs[:, ATT_WIDTH:2 * ATT_WIDTH], n_new))
        outs[6].append(heads(zs[:, 2 * ATT_WIDTH:3 * ATT_WIDTH], n_new))
        outs[7].append(st_s)

    return (xp.reshape(b, seq, d), xs.reshape(n, n_new, d)) + tuple(jnp.stack(o) for o in outs)
```

```python
import functools

import jax
import jax.numpy as jnp
import numpy as np
from jax import lax
from jax.experimental import pallas as pl
from jax.experimental.pallas import tpu as pltpu

F32 = jnp.float32
BF16 = jnp.bfloat16

EPS = 1e-6
ATT_HEADS = 16
ATT_HEAD_DIM = 64
ATT_WIDTH = ATT_HEADS * ATT_HEAD_DIM
DILATIONS = (1, 4, 16)
N_BACK = 128
BAND = 128
ROT_DIM = 16
ROPE_THETA = 500000.0
GLA_HEADS = 4
GLA_HEAD_K = 128
GLA_HEAD_V = 256
GLA_KEY_WIDTH = GLA_HEADS * GLA_HEAD_K
GLA_VALUE_WIDTH = GLA_HEADS * GLA_HEAD_V
GLA_GATE_RANK = 16
GLA_GATE_TAU = 16.0
GLA_CHUNK = 64
MEM_LEN = 256
MEM_HEADS = 4
MEM_HEAD_DIM = 128
MEM_WIDTH = MEM_HEADS * MEM_HEAD_DIM
MAX_WINDOW = 2048
LANES = 128
SAMPLE_GROUP = 8
Z_COLS = 3 * ATT_WIDTH + 2 * GLA_KEY_WIDTH + 2 * GLA_VALUE_WIDTH
Z_TILE = 1024
VMEM_LIMIT_BYTES = 56 * 1024 * 1024


def _params(*semantics):
    return pltpu.CompilerParams(dimension_semantics=semantics, vmem_limit_bytes=VMEM_LIMIT_BYTES)


def _rms(x, g):
    return x * lax.rsqrt(jnp.mean(x * x, axis=-1, keepdims=True) + EPS) * g


def _dot(a, b):
    return jnp.dot(a, b, preferred_element_type=F32)


def _dot_nt(a, b):
    return lax.dot_general(a, b, (((1,), (1,)), ((), ())), preferred_element_type=F32)


def _dot_tn(a, b):
    return lax.dot_general(a, b, (((0,), (0,)), ((), ())), preferred_element_type=F32)


def _split_bf16(x):
    hi = x.astype(BF16)
    lo = (x - hi.astype(F32)).astype(BF16)
    return hi, lo


def _ffn_body(*refs, nf, final):
    if final:
        x_ref, g_ref, wg_ref, wu_ref, wd_ref, gf_ref, o_ref, h_ref = refs
    else:
        x_ref, g_ref, wg_ref, wu_ref, wd_ref, o_ref, h_ref = refs
    f = pl.program_id(1)

    @pl.when(f == 0)
    def _():
        h_ref[...] = _rms(x_ref[...], g_ref[...]).astype(BF16)
        o_ref[...] = jnp.zeros_like(o_ref)

    h = h_ref[...]
    gate = _dot(h, wg_ref[...])
    up = _dot(h, wu_ref[...])
    act = (gate * jax.nn.sigmoid(gate) * up).astype(BF16)
    o_ref[...] += _dot(act, wd_ref[...])

    @pl.when(f == nf - 1)
    def _():
        y = x_ref[...] + 0.5 * o_ref[...]
        if final:
            y = _rms(y, gf_ref[...])
        o_ref[...] = y


def _ffn(x, g, wg, wu, wd, g_final=None, *, tm=512, tf=512):
    m, d = x.shape
    dff = wg.shape[1]
    tm = min(tm, m)
    nf = dff // tf
    final = g_final is not None
    row = pl.BlockSpec((1, d), lambda i, f: (0, 0))
    in_specs = [
        pl.BlockSpec((tm, d), lambda i, f: (i, 0)),
        row,
        pl.BlockSpec((d, tf), lambda i, f: (0, f)),
        pl.BlockSpec((d, tf), lambda i, f: (0, f)),
        pl.BlockSpec((tf, d), lambda i, f: (f, 0)),
    ]
    args = [x, g.reshape(1, d), wg, wu, wd]
    if final:
        in_specs.append(row)
        args.append(g_final.reshape(1, d))
    return pl.pallas_call(
        functools.partial(_ffn_body, nf=nf, final=final),
        grid=(m // tm, nf),
        in_specs=in_specs,
        out_specs=pl.BlockSpec((tm, d), lambda i, f: (i, 0)),
        out_shape=jax.ShapeDtypeStruct((m, d), F32),
        scratch_shapes=[pltpu.VMEM((tm, d), BF16)],
        compiler_params=_params("parallel", "arbitrary"),
        name="ffn_final" if final else "ffn",
    )(*args)


def _mix_in_body(x_ref, g_ref, w_ref, wlr_ref, wa2_ref, ba_ref, rope_ref, z_ref, la_ref, h_ref):
    n = pl.program_id(1)

    @pl.when(n == 0)
    def _():
        h = _rms(x_ref[...], g_ref[...]).astype(BF16)
        h_ref[...] = h
        g_lr = _dot(h, wlr_ref[...])
        pre = _dot(g_lr.astype(BF16), wa2_ref[...]) + ba_ref[...]
        la_ref[...] = (jnp.minimum(pre, 0.0) - jnp.log1p(jnp.exp(-jnp.abs(pre)))) * (1.0 / GLA_GATE_TAU)

    z = _dot(h_ref[...], w_ref[...])

    @pl.when(n >= 2)
    def _():
        z_ref[...] = z

    @pl.when(n < 2)
    def _():
        cos, sin_up, sin_dn = rope_ref[0], rope_ref[1], rope_ref[2]
        scale = jnp.where(n == 0, ATT_HEAD_DIM ** -0.5, 1.0).astype(F32)
        half = ROT_DIM // 2
        for j in range(Z_TILE // LANES):
            zc = z[:, j * LANES:(j + 1) * LANES]
            rot = zc * cos + pltpu.roll(zc, half, axis=1) * sin_up + pltpu.roll(zc, LANES - half, axis=1) * sin_dn
            z_ref[:, j * LANES:(j + 1) * LANES] = rot * scale


def _mix_in(x, g, w_main, w_lr, w_a2, b_a, rope, *, tm=512):
    m, d = x.shape
    tm = min(tm, m)
    n_rope = rope.shape[1] // tm
    return pl.pallas_call(
        _mix_in_body,
        grid=(m // tm, Z_COLS // Z_TILE),
        in_specs=[
            pl.BlockSpec((tm, d), lambda i, n: (i, 0)),
            pl.BlockSpec((1, d), lambda i, n: (0, 0)),
            pl.BlockSpec((d, Z_TILE), lambda i, n: (0, n)),
            pl.BlockSpec((d, LANES), lambda i, n: (0, 0)),
            pl.BlockSpec((LANES, GLA_KEY_WIDTH), lambda i, n: (0, 0)),
            pl.BlockSpec((1, GLA_KEY_WIDTH), lambda i, n: (0, 0)),
            pl.BlockSpec((3, tm, LANES), lambda i, n: (0, i % n_rope, 0)),
        ],
        out_specs=[
            pl.BlockSpec((tm, Z_TILE), lambda i, n: (i, n)),
            pl.BlockSpec((tm, GLA_KEY_WIDTH), lambda i, n: (i, 0)),
        ],
        out_shape=[
            jax.ShapeDtypeStruct((m, Z_COLS), F32),
            jax.ShapeDtypeStruct((m, GLA_KEY_WIDTH), F32),
        ],
        scratch_shapes=[pltpu.VMEM((tm, d), BF16)],
        compiler_params=_params("parallel", "arbitrary"),
        name="mix_in",
    )(x, g.reshape(1, d), w_main, w_lr, w_a2, b_a.reshape(1, -1), rope)


def _rope_tables(pos):
    half = ROT_DIM // 2
    inv = ROPE_THETA ** (-jnp.arange(half, dtype=F32) * 2.0 / ROT_DIM)
    ang = pos.astype(F32)[:, None] * inv[None, :]
    cos, sin = jnp.cos(ang), jnp.sin(ang)
    t = pos.shape[0]
    zeros = jnp.zeros((t, half), F32)
    rest0 = jnp.zeros((t, ATT_HEAD_DIM - ROT_DIM), F32)
    c = jnp.concatenate([cos, cos, jnp.ones_like(rest0)], axis=1)
    s_up = jnp.concatenate([zeros, sin, rest0], axis=1)
    s_dn = jnp.concatenate([-sin, zeros, rest0], axis=1)
    reps = LANES // ATT_HEAD_DIM
    return jnp.stack([jnp.tile(c, (1, reps)), jnp.tile(s_up, (1, reps)), jnp.tile(s_dn, (1, reps))])


def _norm_proj_body(x_ref, g_ref, w_ref, o_ref):
    h = _rms(x_ref[...], g_ref[...]).astype(BF16)
    o_ref[...] = _dot(h, w_ref[...])


def _norm_proj(x, g, w, *, tm=512):
    m, d = x.shape
    n = w.shape[1]
    tm = min(tm, m)
    return pl.pallas_call(
        _norm_proj_body,
        grid=(m // tm,),
        in_specs=[
            pl.BlockSpec((tm, d), lambda i: (i, 0)),
            pl.BlockSpec((1, d), lambda i: (0, 0)),
            pl.BlockSpec((d, n), lambda i: (0, 0)),
        ],
        out_specs=pl.BlockSpec((tm, n), lambda i: (i, 0)),
        out_shape=jax.ShapeDtypeStruct((m, n), F32),
        compiler_params=_params("parallel"),
        name="memory_kv",
    )(x, g.reshape(1, d), w)


ATTN_UNROLL = 4
MERGE_ROWS = 256


def _band_bias():
    i = np.arange(BAND)[:, None]
    j = np.arange(2 * BAND)[None, :]
    out = []
    for off in (0, BAND):
        dist = off + i - j
        out.append(np.where((dist >= 0) & (dist <= N_BACK), 0.0, -np.inf))
    return jnp.asarray(np.stack(out), F32)


def _attn_prompt_body(q_ref, k_ref, v_ref, bias_ref, o_ref, acc_ref, m_ref, l_ref, *, seq):
    hd = ATT_HEAD_DIM
    blocks = seq // BAND
    for ci, dil in enumerate(DILATIONS):
        nb = blocks // dil
        shift = nb.bit_length() - 1
        assert nb == 1 << shift

        def step(it, carry, ci=ci, dil=dil, nb=nb, shift=shift):
            loaded = []
            for u in range(ATTN_UNROLL):
                w = it * ATTN_UNROLL + u
                r = lax.shift_right_logical(w, shift)
                n = w & (nb - 1)
                base = jnp.maximum(n - 1, 0) * BAND
                if dil == 1:
                    q_rows = pl.ds(pl.multiple_of(n * BAND, BAND), BAND)
                    k_rows = pl.ds(pl.multiple_of(base, BAND), 2 * BAND)
                else:
                    q_rows = pl.ds(r + dil * BAND * n, BAND, stride=dil)
                    k_rows = pl.ds(r + dil * base, 2 * BAND, stride=dil)
                bias = bias_ref[jnp.minimum(n, 1)]
                loaded.append((q_rows, q_ref[0, q_rows, :], k_ref[0, k_rows, :], v_ref[0, k_rows, :], bias))
            results = []
            for q_rows, q, k, v, bias in loaded:
                pv, mx, den = [], [], []
                for hh in range(LANES // hd):
                    sl = slice(hh * hd, (hh + 1) * hd)
                    s = _dot_nt(q[:, sl].astype(BF16), k[:, sl].astype(BF16)) + bias
                    m_c = jnp.max(s, axis=1, keepdims=True)
                    p = jnp.exp(s - m_c)
                    den.append(jnp.broadcast_to(jnp.sum(p, axis=1, keepdims=True), (BAND, hd)))
                    mx.append(jnp.broadcast_to(m_c, (BAND, hd)))
                    pv.append(_dot(p.astype(BF16), v[:, sl].astype(BF16)))
                results.append((q_rows, jnp.concatenate(pv, axis=1), jnp.concatenate(mx, axis=1),
                                jnp.concatenate(den, axis=1)))
            for q_rows, pv, mx, den in results:
                acc_ref[ci, q_rows, :] = pv
                m_ref[ci, q_rows, :] = mx
                l_ref[ci, q_rows, :] = den
            return carry

        lax.fori_loop(0, blocks // ATTN_UNROLL, step, 0)

    def merge(i, carry):
        rows = pl.ds(pl.multiple_of(i * MERGE_ROWS, MERGE_ROWS), MERGE_ROWS)
        ms = [m_ref[ci, rows, :] for ci in range(len(DILATIONS))]
        m_all = functools.reduce(jnp.maximum, ms)
        ws = [jnp.exp(m - m_all) for m in ms]
        num = sum(w * acc_ref[ci, rows, :] for ci, w in enumerate(ws))
        den = sum(w * l_ref[ci, rows, :] for ci, w in enumerate(ws))
        o_ref[0, rows, :] = (num / den).astype(o_ref.dtype)
        return carry

    lax.fori_loop(0, seq // MERGE_ROWS, merge, 0)


def _attn_prompt(z3):
    b, seq, _ = z3.shape
    pairs = ATT_WIDTH // LANES
    assert seq % (DILATIONS[-1] * BAND) == 0 and seq // DILATIONS[-1] >= 2 * BAND
    assert (seq // BAND) % ATTN_UNROLL == 0 and seq % MERGE_ROWS == 0
    blk = lambda off: pl.BlockSpec((1, seq, LANES), lambda i, p: (i, 0, off + p))
    return pl.pallas_call(
        functools.partial(_attn_prompt_body, seq=seq),
        grid=(b, pairs),
        in_specs=[blk(0), blk(pairs), blk(2 * pairs),
                  pl.BlockSpec((2, BAND, 2 * BAND), lambda i, p: (0, 0, 0))],
        out_specs=pl.BlockSpec((1, seq, LANES), lambda i, p: (i, 0, p)),
        out_shape=jax.ShapeDtypeStruct((b, seq, ATT_WIDTH), BF16),
        scratch_shapes=[pltpu.VMEM((len(DILATIONS), seq, LANES), F32)] * 3,
        compiler_params=_params("parallel", "parallel"),
        name="attn_prompt",
    )(z3, z3, z3, _band_bias())


SAMPLE_HEADS_PER_STEP = 8
SAMPLE_HEAD_UNROLL = 4
SAMPLE_ROWS = 8


def _sample_multiplicity(n_buf, n_new):
    t = (np.arange(SAMPLE_ROWS) % n_new)[:, None]
    r = np.arange(n_buf)[None, :]
    mult = np.zeros((SAMPLE_ROWS, n_buf), np.float32)
    for dil in DILATIONS:
        back = n_buf + t - r
        mult += (back % dil == 0) & (back >= dil) & (back <= dil * N_BACK)
    return jnp.asarray(mult, F32)


def _attn_sample_body(q_ref, kn_ref, vn_ref, kt_ref, vt_ref, mult_ref, o_ref, *, n_new, heads):
    mult = mult_ref[...]
    live = mult > 0.0
    t_row = lax.broadcasted_iota(jnp.int32, (SAMPLE_ROWS, 1), 0) % n_new
    mult_new = [(t_row >= t).astype(F32) + (len(DILATIONS) - 1.0) * (t_row == t).astype(F32) for t in range(n_new)]

    def head(h, carry):
        q = q_ref[0, h]
        kn = kn_ref[0, h]
        vn = vn_ref[0, h]
        s = jnp.where(live, _dot(q.astype(BF16), kt_ref[0, h].astype(BF16)), -jnp.inf)
        s_new = [jnp.where(mult_new[t] > 0.0, jnp.sum(q * kn[t:t + 1], axis=1, keepdims=True), -jnp.inf)
                 for t in range(n_new)]
        m = functools.reduce(jnp.maximum, s_new, jnp.max(s, axis=1, keepdims=True))
        w = mult * jnp.exp(s - m)
        w_new = [mult_new[t] * jnp.exp(s_new[t] - m) for t in range(n_new)]
        den = jnp.sum(w, axis=1, keepdims=True) + sum(w_new)
        o = _dot_nt(w.astype(BF16), vt_ref[0, h].astype(BF16))
        o = o + sum(w_new[t] * vn[t:t + 1] for t in range(n_new))
        o_ref[0, h] = o / den
        return carry

    lax.fori_loop(0, heads, head, 0, unroll=SAMPLE_HEAD_UNROLL)


def _attn_sample(z3, cache_k, cache_v):
    n, n_new, _ = z3.shape
    n_buf = cache_k.shape[1]
    nh, hd = ATT_HEADS, ATT_HEAD_DIM
    assert n_buf >= DILATIONS[-1] * N_BACK and n_new <= SAMPLE_ROWS
    hs = SAMPLE_HEADS_PER_STEP

    def per_head(col):
        a = z3[:, :, col * ATT_WIDTH:(col + 1) * ATT_WIDTH].reshape(n, n_new, nh, hd).transpose(0, 2, 1, 3)
        return jnp.pad(a, ((0, 0), (0, 0), (0, SAMPLE_ROWS - n_new), (0, 0)))

    rows_on_lanes = lambda c: c.transpose(0, 2, 3, 1)
    small = pl.BlockSpec((1, hs, SAMPLE_ROWS, hd), lambda i, j: (i, j, 0, 0))
    cache = pl.BlockSpec((1, hs, hd, n_buf), lambda i, j: (i, j, 0, 0))
    o = pl.pallas_call(
        functools.partial(_attn_sample_body, n_new=n_new, heads=hs),
        grid=(n, nh // hs),
        in_specs=[small, small, small, cache, cache, pl.BlockSpec((SAMPLE_ROWS, n_buf), lambda i, j: (0, 0))],
        out_specs=small,
        out_shape=jax.ShapeDtypeStruct((n, nh, SAMPLE_ROWS, hd), F32),
        compiler_params=_params("parallel", "parallel"),
        name="attn_sample",
    )(per_head(0), per_head(1), per_head(2), rows_on_lanes(cache_k), rows_on_lanes(cache_v),
      _sample_multiplicity(n_buf, n_new))
    return o[:, :, :n_new].transpose(0, 2, 1, 3).reshape(n, n_new, ATT_WIDTH)


def _gla_rows(q, k, v, li, states, causal, same, group_of_row):
    c = q.shape[0]
    hi, lo = _split_bf16(li)
    tri = causal.astype(BF16)
    b = _dot(tri, hi) + _dot(tri, lo)
    if len(states) == 1:
        b_end = b[c - 1:c, :]
    else:
        full = same.astype(BF16)
        b_end = _dot(full, hi) + _dot(full, lo)
    q_t = (q * (GLA_HEAD_K ** -0.5) * jnp.exp(b)).astype(BF16)
    k_n = (k * jnp.exp(-b)).astype(BF16)
    k_w = k * jnp.exp(b_end - b)
    vb = v.astype(BF16)
    a = jnp.where(causal, _dot_nt(q_t, k_n), 0.0)
    o = _dot(a.astype(BF16), vb)
    ones = jnp.ones((c, LANES), BF16)
    new_states = []
    for gi, s in enumerate(states):
        if len(states) == 1:
            o = o + _dot(q_t, s.astype(BF16))
            kw_g, hi_g, lo_g = k_w.astype(BF16), hi, lo
        else:
            mine = group_of_row == gi
            o = o + jnp.where(mine, _dot(q_t, s.astype(BF16)), 0.0)
            kw_g = jnp.where(mine, k_w, 0.0).astype(BF16)
            hi_g = jnp.where(mine, hi, jnp.zeros_like(hi))
            lo_g = jnp.where(mine, lo, jnp.zeros_like(lo))
        decay = jnp.exp(_dot_tn(hi_g, ones) + _dot_tn(lo_g, ones))
        decay = jnp.concatenate([decay] * (s.shape[1] // LANES), axis=1)
        new_states.append(decay * s + _dot_tn(kw_g, vb))
    return o, new_states


def _gla_finish(o, g_out, rg):
    o = o * lax.rsqrt(jnp.mean(o * o, axis=-1, keepdims=True) + EPS)
    return (o * g_out) * (rg * jax.nn.sigmoid(rg))


GLA_HEADS_PER_STEP = 2
GLA_SEQ_TILE = 1024
GLA_CHUNK_UNROLL = 2


def _gla_prompt_body(q_ref, k_ref, v_ref, rg_ref, la_ref, g_ref, o_ref, s_ref, *, rows, chunk, heads):
    hk, hv = GLA_HEAD_K, GLA_HEAD_V
    r = lax.broadcasted_iota(jnp.int32, (chunk, chunk), 0)
    c = lax.broadcasted_iota(jnp.int32, (chunk, chunk), 1)
    causal = c <= r

    @pl.when(pl.program_id(2) == 0)
    def _():
        s_ref[...] = jnp.zeros_like(s_ref)

    def step(i, carry):
        rr = pl.ds(pl.multiple_of(i * chunk, chunk), chunk)
        for hh in range(heads):
            ks = slice(hh * hk, (hh + 1) * hk)
            vs = slice(hh * hv, (hh + 1) * hv)
            o, (s_new,) = _gla_rows(q_ref[0, rr, ks], k_ref[0, rr, ks], v_ref[0, rr, vs], la_ref[0, rr, ks],
                                    [s_ref[0, hh]], causal, None, None)
            s_ref[0, hh] = s_new
            o_ref[0, rr, vs] = _gla_finish(o, g_ref[:, vs], rg_ref[0, rr, vs]).astype(o_ref.dtype)
        return carry

    lax.fori_loop(0, rows // chunk, step, 0, unroll=GLA_CHUNK_UNROLL)


def _gla_prompt(z3, la3, g_out):
    b, seq, _ = z3.shape
    hk, hv, nh = GLA_HEAD_K, GLA_HEAD_V, GLA_HEADS
    chunk = int(np.gcd(seq, GLA_CHUNK))
    hs = GLA_HEADS_PER_STEP
    rows = min(GLA_SEQ_TILE, seq)
    assert seq % rows == 0 and (rows // chunk) % GLA_CHUNK_UNROLL == 0 and nh % hs == 0
    q0 = 3 * ATT_WIDTH // (hs * hk)
    k0 = q0 + nh // hs
    v0 = (3 * ATT_WIDTH + 2 * GLA_KEY_WIDTH) // (hs * hv)
    r0 = v0 + nh // hs
    return pl.pallas_call(
        functools.partial(_gla_prompt_body, rows=rows, chunk=chunk, heads=hs),
        grid=(b, nh // hs, seq // rows),
        in_specs=[
            pl.BlockSpec((1, rows, hs * hk), lambda i, h, t: (i, t, q0 + h)),
            pl.BlockSpec((1, rows, hs * hk), lambda i, h, t: (i, t, k0 + h)),
            pl.BlockSpec((1, rows, hs * hv), lambda i, h, t: (i, t, v0 + h)),
            pl.BlockSpec((1, rows, hs * hv), lambda i, h, t: (i, t, r0 + h)),
            pl.BlockSpec((1, rows, hs * hk), lambda i, h, t: (i, t, h)),
            pl.BlockSpec((1, hs * hv), lambda i, h, t: (0, h)),
        ],
        out_specs=[
            pl.BlockSpec((1, rows, hs * hv), lambda i, h, t: (i, t, h)),
            pl.BlockSpec((1, hs, hk, hv), lambda i, h, t: (i, h, 0, 0)),
        ],
        out_shape=[
            jax.ShapeDtypeStruct((b, seq, GLA_VALUE_WIDTH), BF16),
            jax.ShapeDtypeStruct((b, nh, hk, hv), F32),
        ],
        compiler_params=_params("parallel", "parallel", "arbitrary"),
        name="gla_prompt",
    )(z3, z3, z3, z3, la3, g_out.reshape(1, -1))


def _gla_sample_body(q_ref, k_ref, v_ref, rg_ref, la_ref, g_ref, s0_ref, o_ref, s_ref, *, n_new, groups):
    rows = n_new * groups
    r = lax.broadcasted_iota(jnp.int32, (rows, rows), 0)
    c = lax.broadcasted_iota(jnp.int32, (rows, rows), 1)
    same = (r // n_new) == (c // n_new)
    causal = same & (c <= r)
    group_of_row = lax.broadcasted_iota(jnp.int32, (rows, 1), 0) // n_new
    o, new_states = _gla_rows(q_ref[...], k_ref[...], v_ref[...], la_ref[...],
                              [s0_ref[gi, 0] for gi in range(groups)], causal, same, group_of_row)
    for gi in range(groups):
        s_ref[gi, 0] = new_states[gi]
    o_ref[...] = _gla_finish(o, g_ref[...], rg_ref[...]).astype(o_ref.dtype)


def _gla_sample(z, la, g_out, state, n_new):
    m = z.shape[0]
    n = m // n_new
    hk, hv, nh = GLA_HEAD_K, GLA_HEAD_V, GLA_HEADS
    groups = min(SAMPLE_GROUP, n)
    rows = groups * n_new
    q0 = 3 * ATT_WIDTH // hk
    k0 = q0 + nh
    v0 = (3 * ATT_WIDTH + 2 * GLA_KEY_WIDTH) // hv
    r0 = v0 + nh
    state_spec = pl.BlockSpec((groups, 1, hk, hv), lambda i, h: (i, h, 0, 0))
    return pl.pallas_call(
        functools.partial(_gla_sample_body, n_new=n_new, groups=groups),
        grid=(n // groups, nh),
        in_specs=[
            pl.BlockSpec((rows, hk), lambda i, h: (i, q0 + h)),
            pl.BlockSpec((rows, hk), lambda i, h: (i, k0 + h)),
            pl.BlockSpec((rows, hv), lambda i, h: (i, v0 + h)),
            pl.BlockSpec((rows, hv), lambda i, h: (i, r0 + h)),
            pl.BlockSpec((rows, hk), lambda i, h: (i, h)),
            pl.BlockSpec((1, hv), lambda i, h: (0, h)),
            state_spec,
        ],
        out_specs=[pl.BlockSpec((rows, hv), lambda i, h: (i, h)), state_spec],
        out_shape=[
            jax.ShapeDtypeStruct((m, GLA_VALUE_WIDTH), BF16),
            jax.ShapeDtypeStruct(state.shape, F32),
        ],
        compiler_params=_params("parallel", "parallel"),
        name="gla_sample",
    )(z, z, z, z, la, g_out.reshape(1, -1), state)


def _out_cross_q_body(x_ref, oa_ref, og_ref, wa_ref, wg_ref, g_ref, wq_ref, x2_ref, qc_ref):
    oa = oa_ref[...].astype(BF16)
    x2 = x_ref[...] + _dot(oa, wa_ref[...]) + _dot(og_ref[...], wg_ref[...])
    x2_ref[...] = x2
    h = _rms(x2, g_ref[...]).astype(BF16)
    qc_ref[...] = (_dot(h, wq_ref[...]) * (MEM_HEAD_DIM ** -0.5)).astype(qc_ref.dtype)


def _out_cross_q(x, o_att, o_gla, w_out_att, w_out_gla, g_cross, w_q, *, tm=512):
    m, d = x.shape
    tm = min(tm, m)
    const = lambda shape: pl.BlockSpec(shape, lambda i: (0, 0))
    return pl.pallas_call(
        _out_cross_q_body,
        grid=(m // tm,),
        in_specs=[
            pl.BlockSpec((tm, d), lambda i: (i, 0)),
            pl.BlockSpec((tm, ATT_WIDTH), lambda i: (i, 0)),
            pl.BlockSpec((tm, GLA_VALUE_WIDTH), lambda i: (i, 0)),
            const(w_out_att.shape),
            const(w_out_gla.shape),
            const((1, d)),
            const(w_q.shape),
        ],
        out_specs=[pl.BlockSpec((tm, d), lambda i: (i, 0)), pl.BlockSpec((tm, MEM_WIDTH), lambda i: (i, 0))],
        out_shape=[jax.ShapeDtypeStruct((m, d), F32), jax.ShapeDtypeStruct((m, MEM_WIDTH), BF16)],
        compiler_params=_params("parallel"),
        name="out_cross_q",
    )(x, o_att, o_gla, w_out_att, w_out_gla, g_cross.reshape(1, d), w_q)


def _cross_body(q_ref, mk_ref, mv_ref, x_ref, w_ref, o_ref, *, groups):
    rows = q_ref.shape[0]
    per = rows // groups
    group_of_row = lax.broadcasted_iota(jnp.int32, (rows, 1), 0) // per
    hd = MEM_HEAD_DIM
    heads = []
    for h in range(MEM_HEADS):
        sl = slice(h * hd, (h + 1) * hd)
        qh = q_ref[:, sl]
        oh = None
        for gi in range(groups):
            s = _dot_nt(qh, mk_ref[gi, :, h, :].astype(BF16))
            p = jnp.exp(s - jnp.max(s, axis=-1, keepdims=True))
            og = _dot(p.astype(BF16), mv_ref[gi, :, h, :].astype(BF16)) / jnp.sum(p, axis=-1, keepdims=True)
            oh = og if oh is None else jnp.where(group_of_row == gi, og, oh)
        heads.append(oh)
    o = jnp.concatenate(heads, axis=1).astype(BF16)
    o_ref[...] = x_ref[...] + _dot(o, w_ref[...])


def _cross(qc, mem_k, mem_v, x, w_o, *, rows, groups, mem_index):
    m, d = x.shape
    mem = pl.BlockSpec((groups, MEM_LEN, MEM_HEADS, MEM_HEAD_DIM), lambda i: (mem_index(i), 0, 0, 0))
    return pl.pallas_call(
        functools.partial(_cross_body, groups=groups),
        grid=(m // rows,),
        in_specs=[
            pl.BlockSpec((rows, MEM_WIDTH), lambda i: (i, 0)),
            mem,
            mem,
            pl.BlockSpec((rows, d), lambda i: (i, 0)),
            pl.BlockSpec(w_o.shape, lambda i: (0, 0)),
        ],
        out_specs=pl.BlockSpec((rows, d), lambda i: (i, 0)),
        out_shape=jax.ShapeDtypeStruct((m, d), F32),
        compiler_params=_params("parallel"),
        name="cross",
    )(qc, mem_k, mem_v, x, w_o)


def kernel(x_prompt, x_sample, mem_prompt, cache_win_k, cache_win_v, state_gla, cache_mem_k, cache_mem_v, g_ffn1, w1_gate, w1_up, w1_down, g_mix, w_in, w_gla_a2, b_gla_a, g_gla_out, w_out, g_mem, w_mem_k, w_mem_v, g_cross, w_cross_q, w_cross_o, g_ffn2, w2_gate, w2_up, w2_down, g_final):
    b, seq, d = x_prompt.shape
    n, n_new, _ = x_sample.shape
    depth = g_ffn1.shape[0]
    past_len = cache_win_k.shape[2]
    n_keep = min(MAX_WINDOW, seq)
    bf = lambda a: a.astype(BF16)

    xp = x_prompt.reshape(b * seq, d)
    xs = x_sample.reshape(n * n_new, d)
    rope_p = _rope_tables(jnp.arange(seq))
    rope_s = _rope_tables(jnp.tile(past_len + jnp.arange(n_new), n))
    outs = [[] for _ in range(8)]
    cols = np.cumsum([0, ATT_WIDTH, ATT_WIDTH, ATT_WIDTH, GLA_KEY_WIDTH, GLA_KEY_WIDTH, GLA_VALUE_WIDTH, GLA_GATE_RANK])
    for l in range(depth):
        w_main = bf(jnp.concatenate([w_in[l][:, :cols[6]], w_in[l][:, cols[7]:]], axis=1))
        w_lr = bf(jnp.pad(w_in[l][:, cols[6]:cols[7]], ((0, 0), (0, LANES - GLA_GATE_RANK))))
        w_a2 = bf(jnp.pad(w_gla_a2[l], ((0, LANES - GLA_GATE_RANK), (0, 0))))
        w_out_att, w_out_gla = bf(w_out[l][:ATT_WIDTH]), bf(w_out[l][ATT_WIDTH:])
        w_mem = bf(jnp.concatenate([w_mem_k[l], w_mem_v[l]], axis=1))
        w1 = (bf(w1_gate[l]), bf(w1_up[l]), bf(w1_down[l]))
        w2 = (bf(w2_gate[l]), bf(w2_up[l]), bf(w2_down[l]))
        w_cq, w_co = bf(w_cross_q[l]), bf(w_cross_o[l])
        last = l == depth - 1

        mem_kv = _norm_proj(mem_prompt.reshape(b * MEM_LEN, d), g_mem[l], w_mem).reshape(b, MEM_LEN, 2 * MEM_WIDTH)
        mem_k = mem_kv[:, :, :MEM_WIDTH].reshape(b, MEM_LEN, MEM_HEADS, MEM_HEAD_DIM)
        mem_v = mem_kv[:, :, MEM_WIDTH:].reshape(b, MEM_LEN, MEM_HEADS, MEM_HEAD_DIM)

        def layer(x, rope, attend, gla, cross):
            x = _ffn(x, g_ffn1[l], *w1)
            z, la = _mix_in(x, g_mix[l], w_main, w_lr, w_a2, b_gla_a[l], rope)
            o_att = attend(z)
            o_gla, state = gla(z, la)
            x, qc = _out_cross_q(x, o_att, o_gla, w_out_att, w_out_gla, g_cross[l], w_cq)
            x = cross(qc, x)
            x = _ffn(x, g_ffn2[l], *w2, g_final=g_final if last else None)
            return x, z, state

        tq = min(512, seq)
        xp, zp, st_p = layer(
            xp, rope_p,
            lambda z: _attn_prompt(z.reshape(b, seq, Z_COLS)).reshape(b * seq, ATT_WIDTH),
            lambda z, la: (lambda o, s: (o.reshape(b * seq, GLA_VALUE_WIDTH), s))(
                *_gla_prompt(z.reshape(b, seq, Z_COLS), la.reshape(b, seq, GLA_KEY_WIDTH), g_gla_out[l])),
            lambda qc, x: _cross(qc, mem_k, mem_v, x, w_co, rows=tq, groups=1,
                                 mem_index=lambda i: i // (seq // tq)),
        )
        groups = min(SAMPLE_GROUP, n)
        xs, zs, st_s = layer(
            xs, rope_s,
            lambda z: _attn_sample(z.reshape(n, n_new, Z_COLS), cache_win_k[l], cache_win_v[l]).reshape(n * n_new, ATT_WIDTH),
            lambda z, la: _gla_sample(z, la, g_gla_out[l], state_gla[l], n_new),
            lambda qc, x: _cross(qc, cache_mem_k[l], cache_mem_v[l], x, w_co,
                                 rows=groups * n_new, groups=groups, mem_index=lambda i: i),
        )

        heads = lambda a, t: a.reshape(-1, t, ATT_HEADS, ATT_HEAD_DIM)
        zp3 = zp.reshape(b, seq, Z_COLS)
        outs[0].append(heads(zp3[:, seq - n_keep:, ATT_WIDTH:2 * ATT_WIDTH], n_keep))
        outs[1].append(heads(zp3[:, seq - n_keep:, 2 * ATT_WIDTH:3 * ATT_WIDTH], n_keep))
        outs[2].append(st_p)
        outs[3].append(mem_k)
        outs[4].append(mem_v)
        outs[5].append(heads(zs[:, ATT_WIDTH:2 * ATT_WIDTH], n_new))
        outs[6].append(heads(zs[:, 2 * ATT_WIDTH:3 * ATT_WIDTH], n_new))
        outs[7].append(st_s)

    return (xp.reshape(b, seq, d), xs.reshape(n, n_new, d)) + tuple(jnp.stack(o) for o in outs)
```

```python
import functools

import jax
import jax.numpy as jnp
import numpy as np
from jax import lax
from jax.experimental import pallas as pl
from jax.experimental.pallas import tpu as pltpu

F32 = jnp.float32
BF16 = jnp.bfloat16

EPS = 1e-6
ATT_HEADS = 16
ATT_HEAD_DIM = 64
ATT_WIDTH = ATT_HEADS * ATT_HEAD_DIM
DILATIONS = (1, 4, 16)
N_BACK = 128
BAND = 128
ROT_DIM = 16
ROPE_THETA = 500000.0
GLA_HEADS = 4
GLA_HEAD_K = 128
GLA_HEAD_V = 256
GLA_KEY_WIDTH = GLA_HEADS * GLA_HEAD_K
GLA_VALUE_WIDTH = GLA_HEADS * GLA_HEAD_V
GLA_GATE_RANK = 16
GLA_GATE_TAU = 16.0
GLA_CHUNK = 64
MEM_LEN = 256
MEM_HEADS = 4
MEM_HEAD_DIM = 128
MEM_WIDTH = MEM_HEADS * MEM_HEAD_DIM
MAX_WINDOW = 2048
LANES = 128
SAMPLE_GROUP = 8
Z_COLS = 3 * ATT_WIDTH + 2 * GLA_KEY_WIDTH + 2 * GLA_VALUE_WIDTH
Z_TILE = 1024
VMEM_LIMIT_BYTES = 56 * 1024 * 1024


def _params(*semantics):
    return pltpu.CompilerParams(dimension_semantics=semantics, vmem_limit_bytes=VMEM_LIMIT_BYTES)


def _rms(x, g):
    return x * lax.rsqrt(jnp.mean(x * x, axis=-1, keepdims=True) + EPS) * g


def _dot(a, b):
    return jnp.dot(a, b, preferred_element_type=F32)


def _dot_nt(a, b):
    return lax.dot_general(a, b, (((1,), (1,)), ((), ())), preferred_element_type=F32)


def _dot_tn(a, b):
    return lax.dot_general(a, b, (((0,), (0,)), ((), ())), preferred_element_type=F32)


def _split_bf16(x):
    hi = x.astype(BF16)
    lo = (x - hi.astype(F32)).astype(BF16)
    return hi, lo


def _ffn_body(*refs, nf, final, emit):
    x_ref, g_ref, wg_ref, wu_ref, wd_ref = refs[:5]
    refs = refs[5:]
    if final:
        gf_ref, refs = refs[0], refs[1:]
    o_ref, h_ref = refs[0], refs[-1]
    f = pl.program_id(1)

    @pl.when(f == 0)
    def _():
        h_ref[...] = _rms(x_ref[...], g_ref[...]).astype(BF16)
        o_ref[...] = jnp.zeros_like(o_ref)

    wg, wu, wd = wg_ref[...].astype(BF16), wu_ref[...].astype(BF16), wd_ref[...].astype(BF16)
    if emit:
        for w_out_ref, w in zip(refs[1:4], (wg, wu, wd)):
            w_out_ref[...] = w
    h = h_ref[...]
    gate = _dot(h, wg)
    up = _dot(h, wu)
    act = (gate * jax.nn.sigmoid(gate) * up).astype(BF16)
    o_ref[...] += _dot(act, wd)

    @pl.when(f == nf - 1)
    def _():
        y = x_ref[...] + 0.5 * o_ref[...]
        if final:
            y = _rms(y, gf_ref[...])
        o_ref[...] = y


def _ffn(x, g, wg, wu, wd, g_final=None, *, tm=512, tf=None):
    m, d = x.shape
    dff = wg.shape[1]
    tm = min(tm, m)
    emit = wg.dtype != BF16
    if tf is None:
        tf = 256 if emit else 512
    nf = dff // tf
    assert dff % tf == 0 and m % tm == 0
    assert not emit or m == tm
    final = g_final is not None
    row = pl.BlockSpec((1, d), lambda i, f: (0, 0))
    w_specs = [
        pl.BlockSpec((d, tf), lambda i, f: (0, f)),
        pl.BlockSpec((d, tf), lambda i, f: (0, f)),
        pl.BlockSpec((tf, d), lambda i, f: (f, 0)),
    ]
    in_specs = [pl.BlockSpec((tm, d), lambda i, f: (i, 0)), row] + w_specs
    args = [x, g.reshape(1, d), wg, wu, wd]
    if final:
        in_specs.append(row)
        args.append(g_final.reshape(1, d))
    out_specs = [pl.BlockSpec((tm, d), lambda i, f: (i, 0))]
    out_shape = [jax.ShapeDtypeStruct((m, d), F32)]
    if emit:
        out_specs += w_specs
        out_shape += [jax.ShapeDtypeStruct(w.shape, BF16) for w in (wg, wu, wd)]
    out = pl.pallas_call(
        functools.partial(_ffn_body, nf=nf, final=final, emit=emit),
        grid=(m // tm, nf),
        in_specs=in_specs,
        out_specs=out_specs,
        out_shape=out_shape,
        scratch_shapes=[pltpu.VMEM((tm, d), BF16)],
        compiler_params=_params("parallel", "arbitrary"),
        name=("ffn_final" if final else "ffn") + ("_f32w" if emit else ""),
    )(*args)
    return out if emit else out[0]


def _mix_in_body(x_ref, g_ref, w_ref, wlr_ref, wa2_ref, ba_ref, rope_ref, z_ref, la_ref, h_ref):
    n = pl.program_id(1)

    @pl.when(n == 0)
    def _():
        h = _rms(x_ref[...], g_ref[...]).astype(BF16)
        h_ref[...] = h
        g_lr = _dot(h, wlr_ref[...])
        pre = _dot(g_lr.astype(BF16), wa2_ref[...]) + ba_ref[...]
        la_ref[...] = (jnp.minimum(pre, 0.0) - jnp.log1p(jnp.exp(-jnp.abs(pre)))) * (1.0 / GLA_GATE_TAU)

    z = _dot(h_ref[...], w_ref[...])

    @pl.when(n >= 2)
    def _():
        z_ref[...] = z

    @pl.when(n < 2)
    def _():
        cos, sin_up, sin_dn = rope_ref[0], rope_ref[1], rope_ref[2]
        scale = jnp.where(n == 0, ATT_HEAD_DIM ** -0.5, 1.0).astype(F32)
        half = ROT_DIM // 2
        for j in range(Z_TILE // LANES):
            zc = z[:, j * LANES:(j + 1) * LANES]
            rot = zc * cos + pltpu.roll(zc, half, axis=1) * sin_up + pltpu.roll(zc, LANES - half, axis=1) * sin_dn
            z_ref[:, j * LANES:(j + 1) * LANES] = rot * scale


def _mix_in(x, g, w_main, w_lr, w_a2, b_a, rope, *, tm=512):
    m, d = x.shape
    tm = min(tm, m)
    n_rope = rope.shape[1] // tm
    return pl.pallas_call(
        _mix_in_body,
        grid=(m // tm, Z_COLS // Z_TILE),
        in_specs=[
            pl.BlockSpec((tm, d), lambda i, n: (i, 0)),
            pl.BlockSpec((1, d), lambda i, n: (0, 0)),
            pl.BlockSpec((d, Z_TILE), lambda i, n: (0, n)),
            pl.BlockSpec((d, LANES), lambda i, n: (0, 0)),
            pl.BlockSpec((LANES, GLA_KEY_WIDTH), lambda i, n: (0, 0)),
            pl.BlockSpec((1, GLA_KEY_WIDTH), lambda i, n: (0, 0)),
            pl.BlockSpec((3, tm, LANES), lambda i, n: (0, i % n_rope, 0)),
        ],
        out_specs=[
            pl.BlockSpec((tm, Z_TILE), lambda i, n: (i, n)),
            pl.BlockSpec((tm, GLA_KEY_WIDTH), lambda i, n: (i, 0)),
        ],
        out_shape=[
            jax.ShapeDtypeStruct((m, Z_COLS), F32),
            jax.ShapeDtypeStruct((m, GLA_KEY_WIDTH), F32),
        ],
        scratch_shapes=[pltpu.VMEM((tm, d), BF16)],
        compiler_params=_params("parallel", "arbitrary"),
        name="mix_in",
    )(x, g.reshape(1, d), w_main, w_lr, w_a2, b_a.reshape(1, -1), rope)


def _rope_tables(pos):
    half = ROT_DIM // 2
    inv = ROPE_THETA ** (-jnp.arange(half, dtype=F32) * 2.0 / ROT_DIM)
    ang = pos.astype(F32)[:, None] * inv[None, :]
    cos, sin = jnp.cos(ang), jnp.sin(ang)
    t = pos.shape[0]
    zeros = jnp.zeros((t, half), F32)
    rest0 = jnp.zeros((t, ATT_HEAD_DIM - ROT_DIM), F32)
    c = jnp.concatenate([cos, cos, jnp.ones_like(rest0)], axis=1)
    s_up = jnp.concatenate([zeros, sin, rest0], axis=1)
    s_dn = jnp.concatenate([-sin, zeros, rest0], axis=1)
    reps = LANES // ATT_HEAD_DIM
    return jnp.stack([jnp.tile(c, (1, reps)), jnp.tile(s_up, (1, reps)), jnp.tile(s_dn, (1, reps))])


def _norm_proj_body(x_ref, g_ref, w_ref, o_ref):
    h = _rms(x_ref[...], g_ref[...]).astype(BF16)
    o_ref[...] = _dot(h, w_ref[...])


def _norm_proj(x, g, w, *, tm=512):
    m, d = x.shape
    n = w.shape[1]
    tm = min(tm, m)
    return pl.pallas_call(
        _norm_proj_body,
        grid=(m // tm,),
        in_specs=[
            pl.BlockSpec((tm, d), lambda i: (i, 0)),
            pl.BlockSpec((1, d), lambda i: (0, 0)),
            pl.BlockSpec((d, n), lambda i: (0, 0)),
        ],
        out_specs=pl.BlockSpec((tm, n), lambda i: (i, 0)),
        out_shape=jax.ShapeDtypeStruct((m, n), F32),
        compiler_params=_params("parallel"),
        name="memory_kv",
    )(x, g.reshape(1, d), w)


ATTN_UNROLL = 8
MERGE_ROWS = 256


def _band_bias():
    i = np.arange(BAND)[:, None]
    j = np.arange(2 * BAND)[None, :]
    out = []
    for off in (0, BAND):
        dist = off + i - j
        out.append(np.where((dist >= 0) & (dist <= N_BACK), 0.0, -np.inf))
    return jnp.asarray(np.stack(out), F32)


def _attn_prompt_body(q_ref, k_ref, v_ref, bias_ref, o_ref, acc_ref, m_ref, l_ref, *, seq):
    hd = ATT_HEAD_DIM
    assert LANES == 2 * hd
    blocks = seq // BAND
    first_head = lax.broadcasted_iota(jnp.int32, (1, LANES), 1) < hd
    for ci, dil in enumerate(DILATIONS):
        nb = blocks // dil
        shift = nb.bit_length() - 1
        assert nb == 1 << shift

        def step(it, carry, ci=ci, dil=dil, nb=nb, shift=shift):
            loaded = []
            for u in range(ATTN_UNROLL):
                w = it * ATTN_UNROLL + u
                r = lax.shift_right_logical(w, shift)
                n = w & (nb - 1)
                base = jnp.maximum(n - 1, 0) * BAND
                if dil == 1:
                    q_rows = pl.ds(pl.multiple_of(n * BAND, BAND), BAND)
                    k_rows = pl.ds(pl.multiple_of(base, BAND), 2 * BAND)
                else:
                    q_rows = pl.ds(r + dil * BAND * n, BAND, stride=dil)
                    k_rows = pl.ds(r + dil * base, 2 * BAND, stride=dil)
                bias = bias_ref[jnp.minimum(n, 1)]
                loaded.append((q_rows, q_ref[0, q_rows, :], k_ref[0, k_rows, :], v_ref[0, k_rows, :], bias))
            results = []
            for q_rows, q, k, v, bias in loaded:
                kb = k.astype(BF16)
                qs = (jnp.where(first_head, q, 0.0), jnp.where(first_head, 0.0, q))
                vs = (jnp.where(first_head, v, 1.0), jnp.where(first_head, 1.0, v))
                out, mx = [], []
                for hh in range(LANES // hd):
                    s = _dot_nt(qs[hh].astype(BF16), kb) + bias
                    m_c = jnp.max(s, axis=1, keepdims=True)
                    p = jnp.exp(s - m_c)
                    out.append(_dot(p.astype(BF16), vs[hh].astype(BF16)))
                    mx.append(m_c)
                results.append((q_rows, jnp.where(first_head, out[0], out[1]),
                                jnp.where(first_head, mx[0], mx[1]), jnp.where(first_head, out[1], out[0])))
            for q_rows, pv, mx, den_swapped in results:
                acc_ref[ci, q_rows, :] = pv
                m_ref[ci, q_rows, :] = mx
                l_ref[ci, q_rows, :] = den_swapped
            return carry

        lax.fori_loop(0, blocks // ATTN_UNROLL, step, 0)

    def merge(i, carry):
        rows = pl.ds(pl.multiple_of(i * MERGE_ROWS, MERGE_ROWS), MERGE_ROWS)
        ms = [m_ref[ci, rows, :] for ci in range(len(DILATIONS))]
        m_all = functools.reduce(jnp.maximum, ms)
        ws = [jnp.exp(m - m_all) for m in ms]
        num = sum(w * acc_ref[ci, rows, :] for ci, w in enumerate(ws))
        den = sum(w * pltpu.roll(l_ref[ci, rows, :], hd, axis=1) for ci, w in enumerate(ws))
        o_ref[0, rows, :] = (num / den).astype(o_ref.dtype)
        return carry

    lax.fori_loop(0, seq // MERGE_ROWS, merge, 0)


def _attn_prompt(z3):
    b, seq, _ = z3.shape
    pairs = ATT_WIDTH // LANES
    assert seq % (DILATIONS[-1] * BAND) == 0 and seq // DILATIONS[-1] >= 2 * BAND
    assert (seq // BAND) % ATTN_UNROLL == 0 and seq % MERGE_ROWS == 0
    blk = lambda off: pl.BlockSpec((1, seq, LANES), lambda i, p: (i, 0, off + p))
    return pl.pallas_call(
        functools.partial(_attn_prompt_body, seq=seq),
        grid=(b, pairs),
        in_specs=[blk(0), blk(pairs), blk(2 * pairs),
                  pl.BlockSpec((2, BAND, 2 * BAND), lambda i, p: (0, 0, 0))],
        out_specs=pl.BlockSpec((1, seq, LANES), lambda i, p: (i, 0, p)),
        out_shape=jax.ShapeDtypeStruct((b, seq, ATT_WIDTH), BF16),
        scratch_shapes=[pltpu.VMEM((len(DILATIONS), seq, LANES), F32)] * 3,
        compiler_params=_params("parallel", "parallel"),
        name="attn_prompt",
    )(z3, z3, z3, _band_bias())


SAMPLE_HEADS_PER_STEP = 16
SAMPLE_HEAD_UNROLL = 4
SAMPLE_ROWS = 8


def _sample_multiplicity(n_buf, n_new):
    t = (np.arange(SAMPLE_ROWS) % n_new)[:, None]
    r = np.arange(n_buf)[None, :]
    mult = np.zeros((SAMPLE_ROWS, n_buf), np.float32)
    for dil in DILATIONS:
        back = n_buf + t - r
        mult += (back % dil == 0) & (back >= dil) & (back <= dil * N_BACK)
    return jnp.asarray(mult, F32)


def _attn_sample_body(q_ref, kn_ref, vn_ref, kt_ref, vt_ref, mult_ref, o_ref, *, n_new, heads):
    mult = mult_ref[...]
    live = mult > 0.0
    t_row = lax.broadcasted_iota(jnp.int32, (SAMPLE_ROWS, 1), 0) % n_new
    mult_new = [(t_row >= t).astype(F32) + (len(DILATIONS) - 1.0) * (t_row == t).astype(F32) for t in range(n_new)]

    def head(h, carry):
        q = q_ref[0, h]
        kn = kn_ref[0, h]
        vn = vn_ref[0, h]
        s = jnp.where(live, _dot(q.astype(BF16), kt_ref[0, h].astype(BF16)), -jnp.inf)
        s_new = [jnp.where(mult_new[t] > 0.0, jnp.sum(q * kn[t:t + 1], axis=1, keepdims=True), -jnp.inf)
                 for t in range(n_new)]
        m = functools.reduce(jnp.maximum, s_new, jnp.max(s, axis=1, keepdims=True))
        w = mult * jnp.exp(s - m)
        w_new = [mult_new[t] * jnp.exp(s_new[t] - m) for t in range(n_new)]
        den = jnp.sum(w, axis=1, keepdims=True) + sum(w_new)
        o = _dot_nt(w.astype(BF16), vt_ref[0, h].astype(BF16))
        o = o + sum(w_new[t] * vn[t:t + 1] for t in range(n_new))
        o_ref[0, h] = o / den
        return carry

    lax.fori_loop(0, heads, head, 0, unroll=SAMPLE_HEAD_UNROLL)


def _attn_sample(z3, cache_k, cache_v):
    n, n_new, _ = z3.shape
    n_buf = cache_k.shape[1]
    nh, hd = ATT_HEADS, ATT_HEAD_DIM
    assert n_buf >= DILATIONS[-1] * N_BACK and n_new <= SAMPLE_ROWS
    hs = SAMPLE_HEADS_PER_STEP

    def per_head(col):
        a = z3[:, :, col * ATT_WIDTH:(col + 1) * ATT_WIDTH].reshape(n, n_new, nh, hd).transpose(0, 2, 1, 3)
        return jnp.pad(a, ((0, 0), (0, 0), (0, SAMPLE_ROWS - n_new), (0, 0)))

    rows_on_lanes = lambda c: c.transpose(0, 2, 3, 1)
    small = pl.BlockSpec((1, hs, SAMPLE_ROWS, hd), lambda i, j: (i, j, 0, 0))
    cache = pl.BlockSpec((1, hs, hd, n_buf), lambda i, j: (i, j, 0, 0))
    o = pl.pallas_call(
        functools.partial(_attn_sample_body, n_new=n_new, heads=hs),
        grid=(n, nh // hs),
        in_specs=[small, small, small, cache, cache, pl.BlockSpec((SAMPLE_ROWS, n_buf), lambda i, j: (0, 0))],
        out_specs=small,
        out_shape=jax.ShapeDtypeStruct((n, nh, SAMPLE_ROWS, hd), F32),
        compiler_params=_params("parallel", "parallel"),
        name="attn_sample",
    )(per_head(0), per_head(1), per_head(2), rows_on_lanes(cache_k), rows_on_lanes(cache_v),
      _sample_multiplicity(n_buf, n_new))
    return o[:, :, :n_new].transpose(0, 2, 1, 3).reshape(n, n_new, ATT_WIDTH)


def _gla_rows(q, k, v, li, states, causal, same, group_of_row):
    c = q.shape[0]
    hi, lo = _split_bf16(li)
    tri = causal.astype(BF16)
    b = _dot(tri, hi) + _dot(tri, lo)
    if len(states) == 1:
        b_end = b[c - 1:c, :]
    else:
        full = same.astype(BF16)
        b_end = _dot(full, hi) + _dot(full, lo)
    q_t = (q * (GLA_HEAD_K ** -0.5) * jnp.exp(b)).astype(BF16)
    k_n = (k * jnp.exp(-b)).astype(BF16)
    k_w = k * jnp.exp(b_end - b)
    vb = v.astype(BF16)
    a = jnp.where(causal, _dot_nt(q_t, k_n), 0.0)
    o = _dot(a.astype(BF16), vb)
    ones = jnp.ones((c, LANES), BF16)
    new_states = []
    for gi, s in enumerate(states):
        if len(states) == 1:
            o = o + _dot(q_t, s.astype(BF16))
            kw_g, hi_g, lo_g = k_w.astype(BF16), hi, lo
        else:
            mine = group_of_row == gi
            o = o + jnp.where(mine, _dot(q_t, s.astype(BF16)), 0.0)
            kw_g = jnp.where(mine, k_w, 0.0).astype(BF16)
            hi_g = jnp.where(mine, hi, jnp.zeros_like(hi))
            lo_g = jnp.where(mine, lo, jnp.zeros_like(lo))
        decay = jnp.exp(_dot_tn(hi_g, ones) + _dot_tn(lo_g, ones))
        decay = jnp.concatenate([decay] * (s.shape[1] // LANES), axis=1)
        new_states.append(decay * s + _dot_tn(kw_g, vb))
    return o, new_states


def _gla_finish(o, g_out, rg):
    o = o * lax.rsqrt(jnp.mean(o * o, axis=-1, keepdims=True) + EPS)
    return (o * g_out) * (rg * jax.nn.sigmoid(rg))


GLA_HEADS_PER_STEP = 2
GLA_SEQ_TILE = 1024
GLA_CHUNK_UNROLL = 2


def _gla_prompt_body(q_ref, k_ref, v_ref, rg_ref, la_ref, g_ref, o_ref, s_ref, *, rows, chunk, heads):
    hk, hv = GLA_HEAD_K, GLA_HEAD_V
    r = lax.broadcasted_iota(jnp.int32, (chunk, chunk), 0)
    c = lax.broadcasted_iota(jnp.int32, (chunk, chunk), 1)
    causal = c <= r

    @pl.when(pl.program_id(2) == 0)
    def _():
        s_ref[...] = jnp.zeros_like(s_ref)

    def step(i, carry):
        rr = pl.ds(pl.multiple_of(i * chunk, chunk), chunk)
        for hh in range(heads):
            ks = slice(hh * hk, (hh + 1) * hk)
            vs = slice(hh * hv, (hh + 1) * hv)
            o, (s_new,) = _gla_rows(q_ref[0, rr, ks], k_ref[0, rr, ks], v_ref[0, rr, vs], la_ref[0, rr, ks],
                                    [s_ref[0, hh]], causal, None, None)
            s_ref[0, hh] = s_new
            o_ref[0, rr, vs] = _gla_finish(o, g_ref[:, vs], rg_ref[0, rr, vs]).astype(o_ref.dtype)
        return carry

    lax.fori_loop(0, rows // chunk, step, 0, unroll=GLA_CHUNK_UNROLL)


def _gla_prompt(z3, la3, g_out):
    b, seq, _ = z3.shape
    hk, hv, nh = GLA_HEAD_K, GLA_HEAD_V, GLA_HEADS
    chunk = int(np.gcd(seq, GLA_CHUNK))
    hs = GLA_HEADS_PER_STEP
    rows = min(GLA_SEQ_TILE, seq)
    assert seq % rows == 0 and (rows // chunk) % GLA_CHUNK_UNROLL == 0 and nh % hs == 0
    q0 = 3 * ATT_WIDTH // (hs * hk)
    k0 = q0 + nh // hs
    v0 = (3 * ATT_WIDTH + 2 * GLA_KEY_WIDTH) // (hs * hv)
    r0 = v0 + nh // hs
    return pl.pallas_call(
        functools.partial(_gla_prompt_body, rows=rows, chunk=chunk, heads=hs),
        grid=(b, nh // hs, seq // rows),
        in_specs=[
            pl.BlockSpec((1, rows, hs * hk), lambda i, h, t: (i, t, q0 + h)),
            pl.BlockSpec((1, rows, hs * hk), lambda i, h, t: (i, t, k0 + h)),
            pl.BlockSpec((1, rows, hs * hv), lambda i, h, t: (i, t, v0 + h)),
            pl.BlockSpec((1, rows, hs * hv), lambda i, h, t: (i, t, r0 + h)),
            pl.BlockSpec((1, rows, hs * hk), lambda i, h, t: (i, t, h)),
            pl.BlockSpec((1, hs * hv), lambda i, h, t: (0, h)),
        ],
        out_specs=[
            pl.BlockSpec((1, rows, hs * hv), lambda i, h, t: (i, t, h)),
            pl.BlockSpec((1, hs, hk, hv), lambda i, h, t: (i, h, 0, 0)),
        ],
        out_shape=[
            jax.ShapeDtypeStruct((b, seq, GLA_VALUE_WIDTH), BF16),
            jax.ShapeDtypeStruct((b, nh, hk, hv), F32),
        ],
        compiler_params=_params("parallel", "parallel", "arbitrary"),
        name="gla_prompt",
    )(z3, z3, z3, z3, la3, g_out.reshape(1, -1))


def _gla_sample_body(q_ref, k_ref, v_ref, rg_ref, la_ref, g_ref, s0_ref, o_ref, s_ref, *, n_new, groups):
    rows = n_new * groups
    r = lax.broadcasted_iota(jnp.int32, (rows, rows), 0)
    c = lax.broadcasted_iota(jnp.int32, (rows, rows), 1)
    same = (r // n_new) == (c // n_new)
    causal = same & (c <= r)
    group_of_row = lax.broadcasted_iota(jnp.int32, (rows, 1), 0) // n_new
    o, new_states = _gla_rows(q_ref[...], k_ref[...], v_ref[...], la_ref[...],
                              [s0_ref[gi, 0] for gi in range(groups)], causal, same, group_of_row)
    for gi in range(groups):
        s_ref[gi, 0] = new_states[gi]
    o_ref[...] = _gla_finish(o, g_ref[...], rg_ref[...]).astype(o_ref.dtype)


def _gla_sample(z, la, g_out, state, n_new):
    m = z.shape[0]
    n = m // n_new
    hk, hv, nh = GLA_HEAD_K, GLA_HEAD_V, GLA_HEADS
    groups = min(SAMPLE_GROUP, n)
    rows = groups * n_new
    q0 = 3 * ATT_WIDTH // hk
    k0 = q0 + nh
    v0 = (3 * ATT_WIDTH + 2 * GLA_KEY_WIDTH) // hv
    r0 = v0 + nh
    state_spec = pl.BlockSpec((groups, 1, hk, hv), lambda i, h: (i, h, 0, 0))
    return pl.pallas_call(
        functools.partial(_gla_sample_body, n_new=n_new, groups=groups),
        grid=(n // groups, nh),
        in_specs=[
            pl.BlockSpec((rows, hk), lambda i, h: (i, q0 + h)),
            pl.BlockSpec((rows, hk), lambda i, h: (i, k0 + h)),
            pl.BlockSpec((rows, hv), lambda i, h: (i, v0 + h)),
            pl.BlockSpec((rows, hv), lambda i, h: (i, r0 + h)),
            pl.BlockSpec((rows, hk), lambda i, h: (i, h)),
            pl.BlockSpec((1, hv), lambda i, h: (0, h)),
            state_spec,
        ],
        out_specs=[pl.BlockSpec((rows, hv), lambda i, h: (i, h)), state_spec],
        out_shape=[
            jax.ShapeDtypeStruct((m, GLA_VALUE_WIDTH), BF16),
            jax.ShapeDtypeStruct(state.shape, F32),
        ],
        compiler_params=_params("parallel", "parallel"),
        name="gla_sample",
    )(z, z, z, z, la, g_out.reshape(1, -1), state)


def _out_cross_q_body(x_ref, oa_ref, og_ref, wa_ref, wg_ref, g_ref, wq_ref, x2_ref, qc_ref):
    oa = oa_ref[...].astype(BF16)
    x2 = x_ref[...] + _dot(oa, wa_ref[...]) + _dot(og_ref[...], wg_ref[...])
    x2_ref[...] = x2
    h = _rms(x2, g_ref[...]).astype(BF16)
    qc_ref[...] = (_dot(h, wq_ref[...]) * (MEM_HEAD_DIM ** -0.5)).astype(qc_ref.dtype)


def _out_cross_q(x, o_att, o_gla, w_out_att, w_out_gla, g_cross, w_q, *, tm=512):
    m, d = x.shape
    tm = min(tm, m)
    const = lambda shape: pl.BlockSpec(shape, lambda i: (0, 0))
    return pl.pallas_call(
        _out_cross_q_body,
        grid=(m // tm,),
        in_specs=[
            pl.BlockSpec((tm, d), lambda i: (i, 0)),
            pl.BlockSpec((tm, ATT_WIDTH), lambda i: (i, 0)),
            pl.BlockSpec((tm, GLA_VALUE_WIDTH), lambda i: (i, 0)),
            const(w_out_att.shape),
            const(w_out_gla.shape),
            const((1, d)),
            const(w_q.shape),
        ],
        out_specs=[pl.BlockSpec((tm, d), lambda i: (i, 0)), pl.BlockSpec((tm, MEM_WIDTH), lambda i: (i, 0))],
        out_shape=[jax.ShapeDtypeStruct((m, d), F32), jax.ShapeDtypeStruct((m, MEM_WIDTH), BF16)],
        compiler_params=_params("parallel"),
        name="out_cross_q",
    )(x, o_att, o_gla, w_out_att, w_out_gla, g_cross.reshape(1, d), w_q)


def _cross_heads(q, mk_ref, mv_ref, groups, head_major):
    rows = q.shape[0]
    group_of_row = lax.broadcasted_iota(jnp.int32, (rows, 1), 0) // (rows // groups)
    hd = MEM_HEAD_DIM
    mem = (lambda ref, gi, h: ref[gi, h]) if head_major else (lambda ref, gi, h: ref[gi, :, h, :])
    heads = []
    for h in range(MEM_HEADS):
        qh = q[:, h * hd:(h + 1) * hd]
        oh = None
        for gi in range(groups):
            s = _dot_nt(qh, mem(mk_ref, gi, h).astype(BF16))
            p = jnp.exp(s - jnp.max(s, axis=-1, keepdims=True))
            og = _dot(p.astype(BF16), mem(mv_ref, gi, h).astype(BF16)) / jnp.sum(p, axis=-1, keepdims=True)
            oh = og if oh is None else jnp.where(group_of_row == gi, og, oh)
        heads.append(oh)
    return jnp.concatenate(heads, axis=1).astype(BF16)


def _cross_body(q_ref, mk_ref, mv_ref, x_ref, w_ref, o_ref, *, groups):
    o_ref[...] = x_ref[...] + _dot(_cross_heads(q_ref[...], mk_ref, mv_ref, groups, False), w_ref[...])


def _post_mix_body(x_ref, oa_ref, og_ref, wa_ref, wg_ref, g_ref, wq_ref, mk_ref, mv_ref, wo_ref, o_ref):
    x2 = x_ref[...] + _dot(oa_ref[...].astype(BF16), wa_ref[...]) + _dot(og_ref[...], wg_ref[...])
    h = _rms(x2, g_ref[...]).astype(BF16)
    q = (_dot(h, wq_ref[...]) * (MEM_HEAD_DIM ** -0.5)).astype(BF16)
    o_ref[...] = x2 + _dot(_cross_heads(q, mk_ref, mv_ref, 1, True), wo_ref[...])


def _post_mix(x, o_att, o_gla, w_out_att, w_out_gla, g_cross, w_q, mem_k, mem_v, w_o, *, seq, tm=512):
    m, d = x.shape
    tm = min(tm, seq)
    assert seq % tm == 0
    const = lambda a: pl.BlockSpec(a.shape, lambda i: (0,) * a.ndim, pipeline_mode=pl.Buffered(1))
    rows = lambda width: pl.BlockSpec((tm, width), lambda i: (i, 0))
    mem = pl.BlockSpec((1, MEM_HEADS, MEM_LEN, MEM_HEAD_DIM), lambda i: (i // (seq // tm), 0, 0, 0))
    g = g_cross.reshape(1, d)
    return pl.pallas_call(
        _post_mix_body,
        grid=(m // tm,),
        in_specs=[rows(d), rows(ATT_WIDTH), rows(GLA_VALUE_WIDTH), const(w_out_att), const(w_out_gla), const(g),
                  const(w_q), mem, mem, const(w_o)],
        out_specs=rows(d),
        out_shape=jax.ShapeDtypeStruct((m, d), F32),
        compiler_params=_params("parallel"),
        name="post_mix",
    )(x, o_att, o_gla, w_out_att, w_out_gla, g, w_q, mem_k, mem_v, w_o)


def _cross(qc, mem_k, mem_v, x, w_o, *, rows, groups):
    m, d = x.shape
    mem = pl.BlockSpec((groups, MEM_LEN, MEM_HEADS, MEM_HEAD_DIM), lambda i: (i, 0, 0, 0))
    return pl.pallas_call(
        functools.partial(_cross_body, groups=groups),
        grid=(m // rows,),
        in_specs=[
            pl.BlockSpec((rows, MEM_WIDTH), lambda i: (i, 0)),
            mem,
            mem,
            pl.BlockSpec((rows, d), lambda i: (i, 0)),
            pl.BlockSpec(w_o.shape, lambda i: (0, 0)),
        ],
        out_specs=pl.BlockSpec((rows, d), lambda i: (i, 0)),
        out_shape=jax.ShapeDtypeStruct((m, d), F32),
        compiler_params=_params("parallel"),
        name="cross",
    )(qc, mem_k, mem_v, x, w_o)


def kernel(x_prompt, x_sample, mem_prompt, cache_win_k, cache_win_v, state_gla, cache_mem_k, cache_mem_v, g_ffn1, w1_gate, w1_up, w1_down, g_mix, w_in, w_gla_a2, b_gla_a, g_gla_out, w_out, g_mem, w_mem_k, w_mem_v, g_cross, w_cross_q, w_cross_o, g_ffn2, w2_gate, w2_up, w2_down, g_final):
    b, seq, d = x_prompt.shape
    n, n_new, _ = x_sample.shape
    depth = g_ffn1.shape[0]
    past_len = cache_win_k.shape[2]
    n_keep = min(MAX_WINDOW, seq)
    bf = lambda a: a.astype(BF16)

    xp = x_prompt.reshape(b * seq, d)
    xs = x_sample.reshape(n * n_new, d)
    rope_p = _rope_tables(jnp.arange(seq))
    rope_s = _rope_tables(jnp.tile(past_len + jnp.arange(n_new), n))
    outs = [[] for _ in range(8)]
    cols = np.cumsum([0, ATT_WIDTH, ATT_WIDTH, ATT_WIDTH, GLA_KEY_WIDTH, GLA_KEY_WIDTH, GLA_VALUE_WIDTH, GLA_GATE_RANK])
    for l in range(depth):
        w_main = bf(jnp.concatenate([w_in[l][:, :cols[6]], w_in[l][:, cols[7]:]], axis=1))
        w_lr = bf(jnp.pad(w_in[l][:, cols[6]:cols[7]], ((0, 0), (0, LANES - GLA_GATE_RANK))))
        w_a2 = bf(jnp.pad(w_gla_a2[l], ((0, LANES - GLA_GATE_RANK), (0, 0))))
        w_out_att, w_out_gla = bf(w_out[l][:ATT_WIDTH]), bf(w_out[l][ATT_WIDTH:])
        w_mem = bf(jnp.concatenate([w_mem_k[l], w_mem_v[l]], axis=1))
        w_cq, w_co = bf(w_cross_q[l]), bf(w_cross_o[l])
        gf = g_final if l == depth - 1 else None

        def mixers(x, rope, attend, gla, post):
            z, la = _mix_in(x, g_mix[l], w_main, w_lr, w_a2, b_gla_a[l], rope)
            o_att = attend(z)
            o_gla, state = gla(z, la)
            return post(x, o_att, o_gla), z, state

        groups = min(SAMPLE_GROUP, n)

        def post_s(x, oa, og):
            x, qc = _out_cross_q(x, oa, og, w_out_att, w_out_gla, g_cross[l], w_cq)
            return _cross(qc, cache_mem_k[l], cache_mem_v[l], x, w_co, rows=groups * n_new, groups=groups)

        xs, *w1 = _ffn(xs, g_ffn1[l], w1_gate[l], w1_up[l], w1_down[l])
        xs, zs, st_s = mixers(
            xs, rope_s,
            lambda z: _attn_sample(z.reshape(n, n_new, Z_COLS), cache_win_k[l], cache_win_v[l]).reshape(n * n_new, ATT_WIDTH),
            lambda z, la: _gla_sample(z, la, g_gla_out[l], state_gla[l], n_new),
            post_s,
        )
        xs, *w2 = _ffn(xs, g_ffn2[l], w2_gate[l], w2_up[l], w2_down[l], g_final=gf)

        mem_kv = _norm_proj(mem_prompt.reshape(b * MEM_LEN, d), g_mem[l], w_mem).reshape(b, MEM_LEN, 2 * MEM_WIDTH)
        mem_k = mem_kv[:, :, :MEM_WIDTH].reshape(b, MEM_LEN, MEM_HEADS, MEM_HEAD_DIM)
        mem_v = mem_kv[:, :, MEM_WIDTH:].reshape(b, MEM_LEN, MEM_HEADS, MEM_HEAD_DIM)

        def gla_p(z, la):
            o, s = _gla_prompt(z.reshape(b, seq, Z_COLS), la.reshape(b, seq, GLA_KEY_WIDTH), g_gla_out[l])
            return o.reshape(b * seq, GLA_VALUE_WIDTH), s

        xp = _ffn(xp, g_ffn1[l], *w1)
        xp, zp, st_p = mixers(
            xp, rope_p,
            lambda z: _attn_prompt(z.reshape(b, seq, Z_COLS)).reshape(b * seq, ATT_WIDTH),
            gla_p,
            lambda x, oa, og: _post_mix(x, oa, og, w_out_att, w_out_gla, g_cross[l], w_cq,
                                        mem_k.transpose(0, 2, 1, 3), mem_v.transpose(0, 2, 1, 3), w_co, seq=seq),
        )
        xp = _ffn(xp, g_ffn2[l], *w2, g_final=gf)


        heads = lambda a, t: a.reshape(-1, t, ATT_HEADS, ATT_HEAD_DIM)
        zp3 = zp.reshape(b, seq, Z_COLS)
        outs[0].append(heads(zp3[:, seq - n_keep:, ATT_WIDTH:2 * ATT_WIDTH], n_keep))
        outs[1].append(heads(zp3[:, seq - n_keep:, 2 * ATT_WIDTH:3 * ATT_WIDTH], n_keep))
        outs[2].append(st_p)
        outs[3].append(mem_k)
        outs[4].append(mem_v)
        outs[5].append(heads(zs[:, ATT_WIDTH:2 * ATT_WIDTH], n_new))
        outs[6].append(heads(zs[:, 2 * ATT_WIDTH:3 * ATT_WIDTH], n_new))
        outs[7].append(st_s)

    return (xp.reshape(b, seq, d), xs.reshape(n, n_new, d)) + tuple(jnp.stack(o) for o in outs)
```

```python
import functools

import jax
import jax.numpy as jnp
import numpy as np
from jax import lax
from jax.experimental import pallas as pl
from jax.experimental.pallas import tpu as pltpu

F32 = jnp.float32
BF16 = jnp.bfloat16

EPS = 1e-6
ATT_HEADS = 16
ATT_HEAD_DIM = 64
ATT_WIDTH = ATT_HEADS * ATT_HEAD_DIM
DILATIONS = (1, 4, 16)
N_BACK = 128
BAND = 128
ROT_DIM = 16
ROPE_THETA = 500000.0
GLA_HEADS = 4
GLA_HEAD_K = 128
GLA_HEAD_V = 256
GLA_KEY_WIDTH = GLA_HEADS * GLA_HEAD_K
GLA_VALUE_WIDTH = GLA_HEADS * GLA_HEAD_V
GLA_GATE_RANK = 16
GLA_GATE_TAU = 16.0
GLA_CHUNK = 64
MEM_LEN = 256
MEM_HEADS = 4
MEM_HEAD_DIM = 128
MEM_WIDTH = MEM_HEADS * MEM_HEAD_DIM
MAX_WINDOW = 2048
LANES = 128
SAMPLE_GROUP = 8
Z_COLS = 3 * ATT_WIDTH + 2 * GLA_KEY_WIDTH + 2 * GLA_VALUE_WIDTH
Z_TILE = 1024
VMEM_LIMIT_BYTES = 56 * 1024 * 1024


def _params(*semantics):
    return pltpu.CompilerParams(dimension_semantics=semantics, vmem_limit_bytes=VMEM_LIMIT_BYTES)


def _rms(x, g):
    return x * lax.rsqrt(jnp.mean(x * x, axis=-1, keepdims=True) + EPS) * g


def _dot(a, b):
    return jnp.dot(a, b, preferred_element_type=F32)


def _dot_nt(a, b):
    return lax.dot_general(a, b, (((1,), (1,)), ((), ())), preferred_element_type=F32)


def _dot_tn(a, b):
    return lax.dot_general(a, b, (((0,), (0,)), ((), ())), preferred_element_type=F32)


def _split_bf16(x):
    hi = x.astype(BF16)
    lo = (x - hi.astype(F32)).astype(BF16)
    return hi, lo


def _ffn_body(*refs, nf, final, emit):
    x_ref, g_ref, wg_ref, wu_ref, wd_ref = refs[:5]
    refs = refs[5:]
    if final:
        gf_ref, refs = refs[0], refs[1:]
    o_ref, h_ref = refs[0], refs[-1]
    f = pl.program_id(1)

    @pl.when(f == 0)
    def _():
        h_ref[...] = _rms(x_ref[...], g_ref[...]).astype(BF16)
        o_ref[...] = jnp.zeros_like(o_ref)

    wg, wu, wd = wg_ref[...].astype(BF16), wu_ref[...].astype(BF16), wd_ref[...].astype(BF16)
    if emit:
        for w_out_ref, w in zip(refs[1:4], (wg, wu, wd)):
            w_out_ref[...] = w
    h = h_ref[...]
    gate = _dot(h, wg)
    up = _dot(h, wu)
    act = (gate * jax.nn.sigmoid(gate) * up).astype(BF16)
    o_ref[...] += _dot(act, wd)

    @pl.when(f == nf - 1)
    def _():
        y = x_ref[...] + 0.5 * o_ref[...]
        if final:
            y = _rms(y, gf_ref[...])
        o_ref[...] = y


def _ffn(x, g, wg, wu, wd, g_final=None, *, tm=1024, tf=None):
    m, d = x.shape
    dff = wg.shape[1]
    tm = min(tm, m)
    emit = wg.dtype != BF16
    if tf is None:
        tf = 256 if emit or tm > 512 else 512
    nf = dff // tf
    assert dff % tf == 0 and m % tm == 0
    assert not emit or m == tm
    final = g_final is not None
    row = pl.BlockSpec((1, d), lambda i, f: (0, 0))
    w_specs = [
        pl.BlockSpec((d, tf), lambda i, f: (0, f)),
        pl.BlockSpec((d, tf), lambda i, f: (0, f)),
        pl.BlockSpec((tf, d), lambda i, f: (f, 0)),
    ]
    in_specs = [pl.BlockSpec((tm, d), lambda i, f: (i, 0)), row] + w_specs
    args = [x, g.reshape(1, d), wg, wu, wd]
    if final:
        in_specs.append(row)
        args.append(g_final.reshape(1, d))
    out_specs = [pl.BlockSpec((tm, d), lambda i, f: (i, 0))]
    out_shape = [jax.ShapeDtypeStruct((m, d), F32)]
    if emit:
        out_specs += w_specs
        out_shape += [jax.ShapeDtypeStruct(w.shape, BF16) for w in (wg, wu, wd)]
    out = pl.pallas_call(
        functools.partial(_ffn_body, nf=nf, final=final, emit=emit),
        grid=(m // tm, nf),
        in_specs=in_specs,
        out_specs=out_specs,
        out_shape=out_shape,
        scratch_shapes=[pltpu.VMEM((tm, d), BF16)],
        compiler_params=_params("parallel", "arbitrary"),
        name=("ffn_final" if final else "ffn") + ("_f32w" if emit else ""),
    )(*args)
    return out if emit else out[0]


def _mix_in_body(x_ref, g_ref, w_ref, wlr_ref, wa2_ref, ba_ref, rope_ref, z_ref, la_ref, h_ref):
    n = pl.program_id(1)

    @pl.when(n == 0)
    def _():
        h = _rms(x_ref[...], g_ref[...]).astype(BF16)
        h_ref[...] = h
        g_lr = _dot(h, wlr_ref[...])
        pre = _dot(g_lr.astype(BF16), wa2_ref[...]) + ba_ref[...]
        la_ref[...] = (jnp.minimum(pre, 0.0) - jnp.log1p(jnp.exp(-jnp.abs(pre)))) * (1.0 / GLA_GATE_TAU)

    z = _dot(h_ref[...], w_ref[...])

    @pl.when(n >= 2)
    def _():
        z_ref[...] = z

    @pl.when(n < 2)
    def _():
        cos, sin_up, sin_dn = rope_ref[0], rope_ref[1], rope_ref[2]
        scale = jnp.where(n == 0, ATT_HEAD_DIM ** -0.5, 1.0).astype(F32)
        half = ROT_DIM // 2
        for j in range(Z_TILE // LANES):
            zc = z[:, j * LANES:(j + 1) * LANES]
            rot = zc * cos + pltpu.roll(zc, half, axis=1) * sin_up + pltpu.roll(zc, LANES - half, axis=1) * sin_dn
            z_ref[:, j * LANES:(j + 1) * LANES] = rot * scale


def _mix_in(x, g, w_main, w_lr, w_a2, b_a, rope, *, tm=512):
    m, d = x.shape
    tm = min(tm, m)
    n_rope = rope.shape[1] // tm
    return pl.pallas_call(
        _mix_in_body,
        grid=(m // tm, Z_COLS // Z_TILE),
        in_specs=[
            pl.BlockSpec((tm, d), lambda i, n: (i, 0)),
            pl.BlockSpec((1, d), lambda i, n: (0, 0)),
            pl.BlockSpec((d, Z_TILE), lambda i, n: (0, n)),
            pl.BlockSpec((d, LANES), lambda i, n: (0, 0)),
            pl.BlockSpec((LANES, GLA_KEY_WIDTH), lambda i, n: (0, 0)),
            pl.BlockSpec((1, GLA_KEY_WIDTH), lambda i, n: (0, 0)),
            pl.BlockSpec((3, tm, LANES), lambda i, n: (0, i % n_rope, 0)),
        ],
        out_specs=[
            pl.BlockSpec((tm, Z_TILE), lambda i, n: (i, n)),
            pl.BlockSpec((tm, GLA_KEY_WIDTH), lambda i, n: (i, 0)),
        ],
        out_shape=[
            jax.ShapeDtypeStruct((m, Z_COLS), F32),
            jax.ShapeDtypeStruct((m, GLA_KEY_WIDTH), F32),
        ],
        scratch_shapes=[pltpu.VMEM((tm, d), BF16)],
        compiler_params=_params("parallel", "arbitrary"),
        name="mix_in",
    )(x, g.reshape(1, d), w_main, w_lr, w_a2, b_a.reshape(1, -1), rope)


def _rope_tables(pos):
    half = ROT_DIM // 2
    inv = ROPE_THETA ** (-jnp.arange(half, dtype=F32) * 2.0 / ROT_DIM)
    ang = pos.astype(F32)[:, None] * inv[None, :]
    cos, sin = jnp.cos(ang), jnp.sin(ang)
    t = pos.shape[0]
    zeros = jnp.zeros((t, half), F32)
    rest0 = jnp.zeros((t, ATT_HEAD_DIM - ROT_DIM), F32)
    c = jnp.concatenate([cos, cos, jnp.ones_like(rest0)], axis=1)
    s_up = jnp.concatenate([zeros, sin, rest0], axis=1)
    s_dn = jnp.concatenate([-sin, zeros, rest0], axis=1)
    reps = LANES // ATT_HEAD_DIM
    return jnp.stack([jnp.tile(c, (1, reps)), jnp.tile(s_up, (1, reps)), jnp.tile(s_dn, (1, reps))])


def _norm_proj_body(x_ref, g_ref, w_ref, o_ref):
    h = _rms(x_ref[...], g_ref[...]).astype(BF16)
    o_ref[...] = _dot(h, w_ref[...])


def _norm_proj(x, g, w, *, tm=512):
    m, d = x.shape
    n = w.shape[1]
    tm = min(tm, m)
    return pl.pallas_call(
        _norm_proj_body,
        grid=(m // tm,),
        in_specs=[
            pl.BlockSpec((tm, d), lambda i: (i, 0)),
            pl.BlockSpec((1, d), lambda i: (0, 0)),
            pl.BlockSpec((d, n), lambda i: (0, 0)),
        ],
        out_specs=pl.BlockSpec((tm, n), lambda i: (i, 0)),
        out_shape=jax.ShapeDtypeStruct((m, n), F32),
        compiler_params=_params("parallel"),
        name="memory_kv",
    )(x, g.reshape(1, d), w)


ATTN_UNROLL = 8
MERGE_ROWS = 256


def _band_bias():
    i = np.arange(BAND)[:, None]
    j = np.arange(2 * BAND)[None, :]
    out = []
    for off in (0, BAND):
        dist = off + i - j
        out.append(np.where((dist >= 0) & (dist <= N_BACK), 0.0, -np.inf))
    return jnp.asarray(np.stack(out), F32)


def _attn_prompt_body(q_ref, k_ref, v_ref, bias_ref, o_ref, acc_ref, m_ref, l_ref, *, seq):
    hd = ATT_HEAD_DIM
    assert LANES == 2 * hd
    blocks = seq // BAND
    first_head = lax.broadcasted_iota(jnp.int32, (1, LANES), 1) < hd
    for ci, dil in enumerate(DILATIONS):
        nb = blocks // dil
        shift = nb.bit_length() - 1
        assert nb == 1 << shift

        def step(it, carry, ci=ci, dil=dil, nb=nb, shift=shift):
            loaded = []
            for u in range(ATTN_UNROLL):
                w = it * ATTN_UNROLL + u
                r = lax.shift_right_logical(w, shift)
                n = w & (nb - 1)
                base = jnp.maximum(n - 1, 0) * BAND
                if dil == 1:
                    q_rows = pl.ds(pl.multiple_of(n * BAND, BAND), BAND)
                    k_rows = pl.ds(pl.multiple_of(base, BAND), 2 * BAND)
                else:
                    q_rows = pl.ds(r + dil * BAND * n, BAND, stride=dil)
                    k_rows = pl.ds(r + dil * base, 2 * BAND, stride=dil)
                bias = bias_ref[jnp.minimum(n, 1)]
                loaded.append((q_rows, q_ref[0, q_rows, :], k_ref[0, k_rows, :], v_ref[0, k_rows, :], bias))
            results = []
            for q_rows, q, k, v, bias in loaded:
                kb = k.astype(BF16)
                qs = (jnp.where(first_head, q, 0.0), jnp.where(first_head, 0.0, q))
                vs = (jnp.where(first_head, v, 1.0), jnp.where(first_head, 1.0, v))
                out, mx = [], []
                for hh in range(LANES // hd):
                    s = _dot_nt(qs[hh].astype(BF16), kb) + bias
                    m_c = jnp.max(s, axis=1, keepdims=True)
                    p = jnp.exp(s - m_c)
                    out.append(_dot(p.astype(BF16), vs[hh].astype(BF16)))
                    mx.append(m_c)
                results.append((q_rows, jnp.where(first_head, out[0], out[1]),
                                jnp.where(first_head, mx[0], mx[1]), jnp.where(first_head, out[1], out[0])))
            for q_rows, pv, mx, den_swapped in results:
                acc_ref[ci, q_rows, :] = pv
                m_ref[ci, q_rows, :] = mx
                l_ref[ci, q_rows, :] = den_swapped
            return carry

        lax.fori_loop(0, blocks // ATTN_UNROLL, step, 0)

    def merge(i, carry):
        rows = pl.ds(pl.multiple_of(i * MERGE_ROWS, MERGE_ROWS), MERGE_ROWS)
        ms = [m_ref[ci, rows, :] for ci in range(len(DILATIONS))]
        m_all = functools.reduce(jnp.maximum, ms)
        ws = [jnp.exp(m - m_all) for m in ms]
        num = sum(w * acc_ref[ci, rows, :] for ci, w in enumerate(ws))
        den = sum(w * pltpu.roll(l_ref[ci, rows, :], hd, axis=1) for ci, w in enumerate(ws))
        o_ref[0, rows, :] = (num / den).astype(o_ref.dtype)
        return carry

    lax.fori_loop(0, seq // MERGE_ROWS, merge, 0)


def _attn_prompt(z3):
    b, seq, _ = z3.shape
    pairs = ATT_WIDTH // LANES
    assert seq % (DILATIONS[-1] * BAND) == 0 and seq // DILATIONS[-1] >= 2 * BAND
    assert (seq // BAND) % ATTN_UNROLL == 0 and seq % MERGE_ROWS == 0
    blk = lambda off: pl.BlockSpec((1, seq, LANES), lambda i, p: (i, 0, off + p))
    return pl.pallas_call(
        functools.partial(_attn_prompt_body, seq=seq),
        grid=(b, pairs),
        in_specs=[blk(0), blk(pairs), blk(2 * pairs),
                  pl.BlockSpec((2, BAND, 2 * BAND), lambda i, p: (0, 0, 0))],
        out_specs=pl.BlockSpec((1, seq, LANES), lambda i, p: (i, 0, p)),
        out_shape=jax.ShapeDtypeStruct((b, seq, ATT_WIDTH), BF16),
        scratch_shapes=[pltpu.VMEM((len(DILATIONS), seq, LANES), F32)] * 3,
        compiler_params=_params("parallel", "parallel"),
        name="attn_prompt",
    )(z3, z3, z3, _band_bias())


SAMPLE_HEADS_PER_STEP = 16
SAMPLE_HEAD_UNROLL = 4
SAMPLE_ROWS = 8


def _sample_multiplicity(n_buf, n_new):
    t = (np.arange(SAMPLE_ROWS) % n_new)[:, None]
    r = np.arange(n_buf)[None, :]
    mult = np.zeros((SAMPLE_ROWS, n_buf), np.float32)
    for dil in DILATIONS:
        back = n_buf + t - r
        mult += (back % dil == 0) & (back >= dil) & (back <= dil * N_BACK)
    return jnp.asarray(mult, F32)


def _attn_sample_body(q_ref, kn_ref, vn_ref, kt_ref, vt_ref, mult_ref, o_ref, *, n_new, heads):
    mult = mult_ref[...]
    live = mult > 0.0
    t_row = lax.broadcasted_iota(jnp.int32, (SAMPLE_ROWS, 1), 0) % n_new
    mult_new = [(t_row >= t).astype(F32) + (len(DILATIONS) - 1.0) * (t_row == t).astype(F32) for t in range(n_new)]

    def head(h, carry):
        q = q_ref[0, h]
        kn = kn_ref[0, h]
        vn = vn_ref[0, h]
        s = jnp.where(live, _dot(q.astype(BF16), kt_ref[0, h].astype(BF16)), -jnp.inf)
        s_new = [jnp.where(mult_new[t] > 0.0, jnp.sum(q * kn[t:t + 1], axis=1, keepdims=True), -jnp.inf)
                 for t in range(n_new)]
        m = functools.reduce(jnp.maximum, s_new, jnp.max(s, axis=1, keepdims=True))
        w = mult * jnp.exp(s - m)
        w_new = [mult_new[t] * jnp.exp(s_new[t] - m) for t in range(n_new)]
        den = jnp.sum(w, axis=1, keepdims=True) + sum(w_new)
        o = _dot_nt(w.astype(BF16), vt_ref[0, h].astype(BF16))
        o = o + sum(w_new[t] * vn[t:t + 1] for t in range(n_new))
        o_ref[0, h] = o / den
        return carry

    lax.fori_loop(0, heads, head, 0, unroll=SAMPLE_HEAD_UNROLL)


def _attn_sample(z3, cache_k, cache_v):
    n, n_new, _ = z3.shape
    n_buf = cache_k.shape[1]
    nh, hd = ATT_HEADS, ATT_HEAD_DIM
    assert n_buf >= DILATIONS[-1] * N_BACK and n_new <= SAMPLE_ROWS
    hs = SAMPLE_HEADS_PER_STEP

    def per_head(col):
        a = z3[:, :, col * ATT_WIDTH:(col + 1) * ATT_WIDTH].reshape(n, n_new, nh, hd).transpose(0, 2, 1, 3)
        return jnp.pad(a, ((0, 0), (0, 0), (0, SAMPLE_ROWS - n_new), (0, 0)))

    rows_on_lanes = lambda c: c.transpose(0, 2, 3, 1)
    small = pl.BlockSpec((1, hs, SAMPLE_ROWS, hd), lambda i, j: (i, j, 0, 0))
    cache = pl.BlockSpec((1, hs, hd, n_buf), lambda i, j: (i, j, 0, 0))
    o = pl.pallas_call(
        functools.partial(_attn_sample_body, n_new=n_new, heads=hs),
        grid=(n, nh // hs),
        in_specs=[small, small, small, cache, cache, pl.BlockSpec((SAMPLE_ROWS, n_buf), lambda i, j: (0, 0))],
        out_specs=small,
        out_shape=jax.ShapeDtypeStruct((n, nh, SAMPLE_ROWS, hd), F32),
        compiler_params=_params("parallel", "parallel"),
        name="attn_sample",
    )(per_head(0), per_head(1), per_head(2), rows_on_lanes(cache_k), rows_on_lanes(cache_v),
      _sample_multiplicity(n_buf, n_new))
    return o[:, :, :n_new].transpose(0, 2, 1, 3).reshape(n, n_new, ATT_WIDTH)


def _gla_rows(q, k, v, li, states, causal, same, group_of_row):
    c = q.shape[0]
    hi, lo = _split_bf16(li)
    tri = causal.astype(BF16)
    b = _dot(tri, hi) + _dot(tri, lo)
    if len(states) == 1:
        b_end = b[c - 1:c, :]
    else:
        full = same.astype(BF16)
        b_end = _dot(full, hi) + _dot(full, lo)
    q_t = (q * (GLA_HEAD_K ** -0.5) * jnp.exp(b)).astype(BF16)
    k_n = (k * jnp.exp(-b)).astype(BF16)
    k_w = k * jnp.exp(b_end - b)
    vb = v.astype(BF16)
    a = jnp.where(causal, _dot_nt(q_t, k_n), 0.0)
    o = _dot(a.astype(BF16), vb)
    ones = jnp.ones((c, LANES), BF16)
    new_states = []
    for gi, s in enumerate(states):
        if len(states) == 1:
            o = o + _dot(q_t, s.astype(BF16))
            kw_g, hi_g, lo_g = k_w.astype(BF16), hi, lo
        else:
            mine = group_of_row == gi
            o = o + jnp.where(mine, _dot(q_t, s.astype(BF16)), 0.0)
            kw_g = jnp.where(mine, k_w, 0.0).astype(BF16)
            hi_g = jnp.where(mine, hi, jnp.zeros_like(hi))
            lo_g = jnp.where(mine, lo, jnp.zeros_like(lo))
        decay = jnp.exp(_dot_tn(hi_g, ones) + _dot_tn(lo_g, ones))
        decay = jnp.concatenate([decay] * (s.shape[1] // LANES), axis=1)
        new_states.append(decay * s + _dot_tn(kw_g, vb))
    return o, new_states


def _gla_finish(o, g_out, rg):
    o = o * lax.rsqrt(jnp.mean(o * o, axis=-1, keepdims=True) + EPS)
    return (o * g_out) * (rg * jax.nn.sigmoid(rg))


GLA_HEADS_PER_STEP = 4
GLA_SEQ_TILE = 1024
GLA_CHUNK_UNROLL = 2


def _gla_prompt_body(q_ref, k_ref, v_ref, rg_ref, la_ref, g_ref, o_ref, s_ref, *, rows, chunk, heads):
    hk, hv = GLA_HEAD_K, GLA_HEAD_V
    r = lax.broadcasted_iota(jnp.int32, (chunk, chunk), 0)
    c = lax.broadcasted_iota(jnp.int32, (chunk, chunk), 1)
    causal = c <= r

    @pl.when(pl.program_id(2) == 0)
    def _():
        s_ref[...] = jnp.zeros_like(s_ref)

    def step(i, carry):
        rr = pl.ds(pl.multiple_of(i * chunk, chunk), chunk)
        for hh in range(heads):
            ks = slice(hh * hk, (hh + 1) * hk)
            vs = slice(hh * hv, (hh + 1) * hv)
            o, (s_new,) = _gla_rows(q_ref[0, rr, ks], k_ref[0, rr, ks], v_ref[0, rr, vs], la_ref[0, rr, ks],
                                    [s_ref[0, hh]], causal, None, None)
            s_ref[0, hh] = s_new
            o_ref[0, rr, vs] = _gla_finish(o, g_ref[:, vs], rg_ref[0, rr, vs]).astype(o_ref.dtype)
        return carry

    lax.fori_loop(0, rows // chunk, step, 0, unroll=GLA_CHUNK_UNROLL)


def _gla_prompt(z3, la3, g_out):
    b, seq, _ = z3.shape
    hk, hv, nh = GLA_HEAD_K, GLA_HEAD_V, GLA_HEADS
    chunk = int(np.gcd(seq, GLA_CHUNK))
    hs = GLA_HEADS_PER_STEP
    rows = min(GLA_SEQ_TILE, seq)
    assert seq % rows == 0 and (rows // chunk) % GLA_CHUNK_UNROLL == 0 and nh % hs == 0
    q0 = 3 * ATT_WIDTH // (hs * hk)
    k0 = q0 + nh // hs
    v0 = (3 * ATT_WIDTH + 2 * GLA_KEY_WIDTH) // (hs * hv)
    r0 = v0 + nh // hs
    return pl.pallas_call(
        functools.partial(_gla_prompt_body, rows=rows, chunk=chunk, heads=hs),
        grid=(b, nh // hs, seq // rows),
        in_specs=[
            pl.BlockSpec((1, rows, hs * hk), lambda i, h, t: (i, t, q0 + h)),
            pl.BlockSpec((1, rows, hs * hk), lambda i, h, t: (i, t, k0 + h)),
            pl.BlockSpec((1, rows, hs * hv), lambda i, h, t: (i, t, v0 + h)),
            pl.BlockSpec((1, rows, hs * hv), lambda i, h, t: (i, t, r0 + h)),
            pl.BlockSpec((1, rows, hs * hk), lambda i, h, t: (i, t, h)),
            pl.BlockSpec((1, hs * hv), lambda i, h, t: (0, h)),
        ],
        out_specs=[
            pl.BlockSpec((1, rows, hs * hv), lambda i, h, t: (i, t, h)),
            pl.BlockSpec((1, hs, hk, hv), lambda i, h, t: (i, h, 0, 0)),
        ],
        out_shape=[
            jax.ShapeDtypeStruct((b, seq, GLA_VALUE_WIDTH), BF16),
            jax.ShapeDtypeStruct((b, nh, hk, hv), F32),
        ],
        compiler_params=_params("parallel", "parallel", "arbitrary"),
        name="gla_prompt",
    )(z3, z3, z3, z3, la3, g_out.reshape(1, -1))


def _gla_sample_body(q_ref, k_ref, v_ref, rg_ref, la_ref, g_ref, s0_ref, o_ref, s_ref, *, n_new, groups):
    rows = n_new * groups
    r = lax.broadcasted_iota(jnp.int32, (rows, rows), 0)
    c = lax.broadcasted_iota(jnp.int32, (rows, rows), 1)
    same = (r // n_new) == (c // n_new)
    causal = same & (c <= r)
    group_of_row = lax.broadcasted_iota(jnp.int32, (rows, 1), 0) // n_new
    o, new_states = _gla_rows(q_ref[...], k_ref[...], v_ref[...], la_ref[...],
                              [s0_ref[gi, 0] for gi in range(groups)], causal, same, group_of_row)
    for gi in range(groups):
        s_ref[gi, 0] = new_states[gi]
    o_ref[...] = _gla_finish(o, g_ref[...], rg_ref[...]).astype(o_ref.dtype)


def _gla_sample(z, la, g_out, state, n_new):
    m = z.shape[0]
    n = m // n_new
    hk, hv, nh = GLA_HEAD_K, GLA_HEAD_V, GLA_HEADS
    groups = min(SAMPLE_GROUP, n)
    rows = groups * n_new
    q0 = 3 * ATT_WIDTH // hk
    k0 = q0 + nh
    v0 = (3 * ATT_WIDTH + 2 * GLA_KEY_WIDTH) // hv
    r0 = v0 + nh
    state_spec = pl.BlockSpec((groups, 1, hk, hv), lambda i, h: (i, h, 0, 0))
    return pl.pallas_call(
        functools.partial(_gla_sample_body, n_new=n_new, groups=groups),
        grid=(n // groups, nh),
        in_specs=[
            pl.BlockSpec((rows, hk), lambda i, h: (i, q0 + h)),
            pl.BlockSpec((rows, hk), lambda i, h: (i, k0 + h)),
            pl.BlockSpec((rows, hv), lambda i, h: (i, v0 + h)),
            pl.BlockSpec((rows, hv), lambda i, h: (i, r0 + h)),
            pl.BlockSpec((rows, hk), lambda i, h: (i, h)),
            pl.BlockSpec((1, hv), lambda i, h: (0, h)),
            state_spec,
        ],
        out_specs=[pl.BlockSpec((rows, hv), lambda i, h: (i, h)), state_spec],
        out_shape=[
            jax.ShapeDtypeStruct((m, GLA_VALUE_WIDTH), BF16),
            jax.ShapeDtypeStruct(state.shape, F32),
        ],
        compiler_params=_params("parallel", "parallel"),
        name="gla_sample",
    )(z, z, z, z, la, g_out.reshape(1, -1), state)


def _out_cross_q_body(x_ref, oa_ref, og_ref, wa_ref, wg_ref, g_ref, wq_ref, x2_ref, qc_ref):
    oa = oa_ref[...].astype(BF16)
    x2 = x_ref[...] + _dot(oa, wa_ref[...]) + _dot(og_ref[...], wg_ref[...])
    x2_ref[...] = x2
    h = _rms(x2, g_ref[...]).astype(BF16)
    qc_ref[...] = (_dot(h, wq_ref[...]) * (MEM_HEAD_DIM ** -0.5)).astype(qc_ref.dtype)


def _out_cross_q(x, o_att, o_gla, w_out_att, w_out_gla, g_cross, w_q, *, tm=512):
    m, d = x.shape
    tm = min(tm, m)
    const = lambda shape: pl.BlockSpec(shape, lambda i: (0, 0))
    return pl.pallas_call(
        _out_cross_q_body,
        grid=(m // tm,),
        in_specs=[
            pl.BlockSpec((tm, d), lambda i: (i, 0)),
            pl.BlockSpec((tm, ATT_WIDTH), lambda i: (i, 0)),
            pl.BlockSpec((tm, GLA_VALUE_WIDTH), lambda i: (i, 0)),
            const(w_out_att.shape),
            const(w_out_gla.shape),
            const((1, d)),
            const(w_q.shape),
        ],
        out_specs=[pl.BlockSpec((tm, d), lambda i: (i, 0)), pl.BlockSpec((tm, MEM_WIDTH), lambda i: (i, 0))],
        out_shape=[jax.ShapeDtypeStruct((m, d), F32), jax.ShapeDtypeStruct((m, MEM_WIDTH), BF16)],
        compiler_params=_params("parallel"),
        name="out_cross_q",
    )(x, o_att, o_gla, w_out_att, w_out_gla, g_cross.reshape(1, d), w_q)


def _cross_heads(q, mk_ref, mv_ref):
    hd = MEM_HEAD_DIM
    heads = []
    for h in range(MEM_HEADS):
        s = _dot_nt(q[:, h * hd:(h + 1) * hd], mk_ref[0, h].astype(BF16))
        p = jnp.exp(s - jnp.max(s, axis=-1, keepdims=True))
        heads.append(_dot(p.astype(BF16), mv_ref[0, h].astype(BF16)) / jnp.sum(p, axis=-1, keepdims=True))
    return jnp.concatenate(heads, axis=1).astype(BF16)


def _cross_body(q_ref, mk_ref, mv_ref, x_ref, w_ref, o_ref, *, groups):
    rows = q_ref.shape[0]
    per_mem = MEM_LEN * MEM_HEADS
    keys = groups * per_mem
    hd = MEM_HEAD_DIM
    k_all = mk_ref[...].reshape(keys, hd).astype(BF16)
    v_all = mv_ref[...].reshape(keys, hd).astype(BF16)
    col = lax.broadcasted_iota(jnp.int32, (1, keys), 1)
    group_of_row = lax.broadcasted_iota(jnp.int32, (rows, 1), 0) // (rows // groups)
    own = (col // per_mem) == group_of_row
    head_of_col = col % MEM_HEADS
    heads = []
    for h in range(MEM_HEADS):
        s = jnp.where(own & (head_of_col == h), _dot_nt(q_ref[:, h * hd:(h + 1) * hd], k_all), -jnp.inf)
        p = jnp.exp(s - jnp.max(s, axis=-1, keepdims=True))
        heads.append(_dot(p.astype(BF16), v_all) / jnp.sum(p, axis=-1, keepdims=True))
    o = jnp.concatenate(heads, axis=1).astype(BF16)
    o_ref[...] = x_ref[...] + _dot(o, w_ref[...])


def _post_mix_body(x_ref, oa_ref, og_ref, wa_ref, wg_ref, g_ref, wq_ref, mk_ref, mv_ref, wo_ref, o_ref):
    x2 = x_ref[...] + _dot(oa_ref[...].astype(BF16), wa_ref[...]) + _dot(og_ref[...], wg_ref[...])
    h = _rms(x2, g_ref[...]).astype(BF16)
    q = (_dot(h, wq_ref[...]) * (MEM_HEAD_DIM ** -0.5)).astype(BF16)
    o_ref[...] = x2 + _dot(_cross_heads(q, mk_ref, mv_ref), wo_ref[...])


def _post_mix(x, o_att, o_gla, w_out_att, w_out_gla, g_cross, w_q, mem_k, mem_v, w_o, *, seq, tm=512):
    m, d = x.shape
    tm = min(tm, seq)
    assert seq % tm == 0
    const = lambda a: pl.BlockSpec(a.shape, lambda i: (0,) * a.ndim, pipeline_mode=pl.Buffered(1))
    rows = lambda width: pl.BlockSpec((tm, width), lambda i: (i, 0))
    mem = pl.BlockSpec((1, MEM_HEADS, MEM_LEN, MEM_HEAD_DIM), lambda i: (i // (seq // tm), 0, 0, 0))
    g = g_cross.reshape(1, d)
    return pl.pallas_call(
        _post_mix_body,
        grid=(m // tm,),
        in_specs=[rows(d), rows(ATT_WIDTH), rows(GLA_VALUE_WIDTH), const(w_out_att), const(w_out_gla), const(g),
                  const(w_q), mem, mem, const(w_o)],
        out_specs=rows(d),
        out_shape=jax.ShapeDtypeStruct((m, d), F32),
        compiler_params=_params("parallel"),
        name="post_mix",
    )(x, o_att, o_gla, w_out_att, w_out_gla, g, w_q, mem_k, mem_v, w_o)


def _cross(qc, mem_k, mem_v, x, w_o, *, rows, groups):
    m, d = x.shape
    mem_k, mem_v = (a.reshape(a.shape[0], MEM_LEN * MEM_HEADS, MEM_HEAD_DIM) for a in (mem_k, mem_v))
    mem = pl.BlockSpec((groups, MEM_LEN * MEM_HEADS, MEM_HEAD_DIM), lambda i: (i, 0, 0))
    return pl.pallas_call(
        functools.partial(_cross_body, groups=groups),
        grid=(m // rows,),
        in_specs=[
            pl.BlockSpec((rows, MEM_WIDTH), lambda i: (i, 0)),
            mem,
            mem,
            pl.BlockSpec((rows, d), lambda i: (i, 0)),
            pl.BlockSpec(w_o.shape, lambda i: (0, 0)),
        ],
        out_specs=pl.BlockSpec((rows, d), lambda i: (i, 0)),
        out_shape=jax.ShapeDtypeStruct((m, d), F32),
        compiler_params=_params("parallel"),
        name="cross",
    )(qc, mem_k, mem_v, x, w_o)


def kernel(x_prompt, x_sample, mem_prompt, cache_win_k, cache_win_v, state_gla, cache_mem_k, cache_mem_v, g_ffn1, w1_gate, w1_up, w1_down, g_mix, w_in, w_gla_a2, b_gla_a, g_gla_out, w_out, g_mem, w_mem_k, w_mem_v, g_cross, w_cross_q, w_cross_o, g_ffn2, w2_gate, w2_up, w2_down, g_final):
    b, seq, d = x_prompt.shape
    n, n_new, _ = x_sample.shape
    depth = g_ffn1.shape[0]
    past_len = cache_win_k.shape[2]
    n_keep = min(MAX_WINDOW, seq)
    bf = lambda a: a.astype(BF16)

    xp = x_prompt.reshape(b * seq, d)
    xs = x_sample.reshape(n * n_new, d)
    rope_p = _rope_tables(jnp.arange(seq))
    rope_s = _rope_tables(jnp.tile(past_len + jnp.arange(n_new), n))
    outs = [[] for _ in range(8)]
    cols = np.cumsum([0, ATT_WIDTH, ATT_WIDTH, ATT_WIDTH, GLA_KEY_WIDTH, GLA_KEY_WIDTH, GLA_VALUE_WIDTH, GLA_GATE_RANK])
    for l in range(depth):
        w_main = bf(jnp.concatenate([w_in[l][:, :cols[6]], w_in[l][:, cols[7]:]], axis=1))
        w_lr = bf(jnp.pad(w_in[l][:, cols[6]:cols[7]], ((0, 0), (0, LANES - GLA_GATE_RANK))))
        w_a2 = bf(jnp.pad(w_gla_a2[l], ((0, LANES - GLA_GATE_RANK), (0, 0))))
        w_out_att, w_out_gla = bf(w_out[l][:ATT_WIDTH]), bf(w_out[l][ATT_WIDTH:])
        w_mem = bf(jnp.concatenate([w_mem_k[l], w_mem_v[l]], axis=1))
        w_cq, w_co = bf(w_cross_q[l]), bf(w_cross_o[l])
        gf = g_final if l == depth - 1 else None

        def mixers(x, rope, attend, gla, post):
            z, la = _mix_in(x, g_mix[l], w_main, w_lr, w_a2, b_gla_a[l], rope)
            o_att = attend(z)
            o_gla, state = gla(z, la)
            return post(x, o_att, o_gla), z, state

        groups = min(SAMPLE_GROUP, n)

        def post_s(x, oa, og):
            x, qc = _out_cross_q(x, oa, og, w_out_att, w_out_gla, g_cross[l], w_cq)
            return _cross(qc, cache_mem_k[l], cache_mem_v[l], x, w_co, rows=groups * n_new, groups=groups)

        xs, *w1 = _ffn(xs, g_ffn1[l], w1_gate[l], w1_up[l], w1_down[l])
        xs, zs, st_s = mixers(
            xs, rope_s,
            lambda z: _attn_sample(z.reshape(n, n_new, Z_COLS), cache_win_k[l], cache_win_v[l]).reshape(n * n_new, ATT_WIDTH),
            lambda z, la: _gla_sample(z, la, g_gla_out[l], state_gla[l], n_new),
            post_s,
        )
        xs, *w2 = _ffn(xs, g_ffn2[l], w2_gate[l], w2_up[l], w2_down[l], g_final=gf)

        mem_kv = _norm_proj(mem_prompt.reshape(b * MEM_LEN, d), g_mem[l], w_mem).reshape(b, MEM_LEN, 2 * MEM_WIDTH)
        mem_k = mem_kv[:, :, :MEM_WIDTH].reshape(b, MEM_LEN, MEM_HEADS, MEM_HEAD_DIM)
        mem_v = mem_kv[:, :, MEM_WIDTH:].reshape(b, MEM_LEN, MEM_HEADS, MEM_HEAD_DIM)

        def gla_p(z, la):
            o, s = _gla_prompt(z.reshape(b, seq, Z_COLS), la.reshape(b, seq, GLA_KEY_WIDTH), g_gla_out[l])
            return o.reshape(b * seq, GLA_VALUE_WIDTH), s

        xp = _ffn(xp, g_ffn1[l], *w1)
        xp, zp, st_p = mixers(
            xp, rope_p,
            lambda z: _attn_prompt(z.reshape(b, seq, Z_COLS)).reshape(b * seq, ATT_WIDTH),
            gla_p,
            lambda x, oa, og: _post_mix(x, oa, og, w_out_att, w_out_gla, g_cross[l], w_cq,
                                        mem_k.transpose(0, 2, 1, 3), mem_v.transpose(0, 2, 1, 3), w_co, seq=seq),
        )
        xp = _ffn(xp, g_ffn2[l], *w2, g_final=gf)


        heads = lambda a, t: a.reshape(-1, t, ATT_HEADS, ATT_HEAD_DIM)
        zp3 = zp.reshape(b, seq, Z_COLS)
        outs[0].append(heads(zp3[:, seq - n_keep:, ATT_WIDTH:2 * ATT_WIDTH], n_keep))
        outs[1].append(heads(zp3[:, seq - n_keep:, 2 * ATT_WIDTH:3 * ATT_WIDTH], n_keep))
        outs[2].append(st_p)
        outs[3].append(mem_k)
        outs[4].append(mem_v)
        outs[5].append(heads(zs[:, ATT_WIDTH:2 * ATT_WIDTH], n_new))
        outs[6].append(heads(zs[:, 2 * ATT_WIDTH:3 * ATT_WIDTH], n_new))
        outs[7].append(st_s)

    return (xp.reshape(b, seq, d), xs.reshape(n, n_new, d)) + tuple(jnp.stack(o) for o in outs)
```

```python
import functools

import jax
import jax.numpy as jnp
import numpy as np
from jax import lax
from jax.experimental import pallas as pl
from jax.experimental.pallas import tpu as pltpu

F32 = jnp.float32
BF16 = jnp.bfloat16

EPS = 1e-6
ATT_HEADS = 16
ATT_HEAD_DIM = 64
ATT_WIDTH = ATT_HEADS * ATT_HEAD_DIM
DILATIONS = (1, 4, 16)
N_BACK = 128
BAND = 128
ROT_DIM = 16
ROPE_THETA = 500000.0
GLA_HEADS = 4
GLA_HEAD_K = 128
GLA_HEAD_V = 256
GLA_KEY_WIDTH = GLA_HEADS * GLA_HEAD_K
GLA_VALUE_WIDTH = GLA_HEADS * GLA_HEAD_V
GLA_GATE_RANK = 16
GLA_GATE_TAU = 16.0
GLA_CHUNK = 64
MEM_LEN = 256
MEM_HEADS = 4
MEM_HEAD_DIM = 128
MEM_WIDTH = MEM_HEADS * MEM_HEAD_DIM
MAX_WINDOW = 2048
LANES = 128
SAMPLE_GROUP = 8
Z_COLS = 3 * ATT_WIDTH + 2 * GLA_KEY_WIDTH + 2 * GLA_VALUE_WIDTH
Z_TILE = 1024
VMEM_LIMIT_BYTES = 56 * 1024 * 1024


def _params(*semantics):
    return pltpu.CompilerParams(dimension_semantics=semantics, vmem_limit_bytes=VMEM_LIMIT_BYTES)


def _rms(x, g):
    return x * lax.rsqrt(jnp.mean(x * x, axis=-1, keepdims=True) + EPS) * g


def _dot(a, b):
    return jnp.dot(a, b, preferred_element_type=F32)


def _dot_nt(a, b):
    return lax.dot_general(a, b, (((1,), (1,)), ((), ())), preferred_element_type=F32)


def _dot_tn(a, b):
    return lax.dot_general(a, b, (((0,), (0,)), ((), ())), preferred_element_type=F32)


def _split_bf16(x):
    hi = x.astype(BF16)
    lo = (x - hi.astype(F32)).astype(BF16)
    return hi, lo


def _ffn_body(*refs, nf, final, emit, rider):
    x_ref, g_ref, wg_ref, wu_ref, wd_ref = refs[:5]
    refs = refs[5:]
    if final:
        gf_ref, refs = refs[0], refs[1:]
    if rider is not None:
        rider_in, refs = refs[:rider.n_in], refs[rider.n_in:]
    o_ref, h_ref = refs[0], refs[-1]
    f = pl.program_id(1)

    @pl.when(f == 0)
    def _():
        h_ref[...] = _rms(x_ref[...], g_ref[...]).astype(BF16)
        o_ref[...] = jnp.zeros_like(o_ref)

    wg, wu, wd = wg_ref[...].astype(BF16), wu_ref[...].astype(BF16), wd_ref[...].astype(BF16)
    if emit:
        for w_out_ref, w in zip(refs[1:4], (wg, wu, wd)):
            w_out_ref[...] = w
    h = h_ref[...]
    gate = _dot(h, wg)
    if rider is not None:
        carried = rider.first(*rider_in)
    up = _dot(h, wu)
    if rider is not None:
        rider.second(*rider_in, refs[-2], carried)
    act = (gate * jax.nn.sigmoid(gate) * up).astype(BF16)
    o_ref[...] += _dot(act, wd)

    @pl.when(f == nf - 1)
    def _():
        y = x_ref[...] + 0.5 * o_ref[...]
        if final:
            y = _rms(y, gf_ref[...])
        o_ref[...] = y


def _ffn(x, g, wg, wu, wd, g_final=None, *, tm=512, tf=None, rider=None):
    m, d = x.shape
    dff = wg.shape[1]
    tm = min(tm, m)
    emit = wg.dtype != BF16
    if tf is None:
        tf = 256 if emit or rider is not None else 512
    nf = dff // tf
    assert dff % tf == 0 and m % tm == 0
    assert not emit or m == tm
    final = g_final is not None
    row = pl.BlockSpec((1, d), lambda i, f: (0, 0))
    w_specs = [
        pl.BlockSpec((d, tf), lambda i, f: (0, f)),
        pl.BlockSpec((d, tf), lambda i, f: (0, f)),
        pl.BlockSpec((tf, d), lambda i, f: (f, 0)),
    ]
    in_specs = [pl.BlockSpec((tm, d), lambda i, f: (i, 0)), row] + w_specs
    args = [x, g.reshape(1, d), wg, wu, wd]
    if final:
        in_specs.append(row)
        args.append(g_final.reshape(1, d))
    out_specs = [pl.BlockSpec((tm, d), lambda i, f: (i, 0))]
    out_shape = [jax.ShapeDtypeStruct((m, d), F32)]
    if emit:
        out_specs += w_specs
        out_shape += [jax.ShapeDtypeStruct(w.shape, BF16) for w in (wg, wu, wd)]
    if rider is not None:
        rider = rider(m // tm, nf)
        in_specs += rider.in_specs
        args += rider.args
        out_specs.append(rider.out_spec)
        out_shape.append(rider.out_shape)
    out = pl.pallas_call(
        functools.partial(_ffn_body, nf=nf, final=final, emit=emit, rider=rider),
        grid=(m // tm, nf),
        in_specs=in_specs,
        out_specs=out_specs,
        out_shape=out_shape,
        scratch_shapes=[pltpu.VMEM((tm, d), BF16)],
        compiler_params=_params("parallel" if rider is None else "arbitrary", "arbitrary"),
        name=("ffn_final" if final else "ffn") + ("_f32w" if emit else "") + ("" if rider is None else "_" + rider.name),
    )(*args)
    return out if len(out) > 1 else out[0]


def _mix_in_body(x_ref, g_ref, w_ref, wlr_ref, wa2_ref, ba_ref, rope_ref, z_ref, la_ref, h_ref):
    n = pl.program_id(1)

    @pl.when(n == 0)
    def _():
        h = _rms(x_ref[...], g_ref[...]).astype(BF16)
        h_ref[...] = h
        g_lr = _dot(h, wlr_ref[...])
        pre = _dot(g_lr.astype(BF16), wa2_ref[...]) + ba_ref[...]
        la_ref[...] = (jnp.minimum(pre, 0.0) - jnp.log1p(jnp.exp(-jnp.abs(pre)))) * (1.0 / GLA_GATE_TAU)

    z = _dot(h_ref[...], w_ref[...])

    @pl.when(n >= 2)
    def _():
        z_ref[...] = z

    @pl.when(n < 2)
    def _():
        cos, sin_up, sin_dn = rope_ref[0], rope_ref[1], rope_ref[2]
        scale = jnp.where(n == 0, ATT_HEAD_DIM ** -0.5, 1.0).astype(F32)
        half = ROT_DIM // 2
        for j in range(Z_TILE // LANES):
            zc = z[:, j * LANES:(j + 1) * LANES]
            rot = zc * cos + pltpu.roll(zc, half, axis=1) * sin_up + pltpu.roll(zc, LANES - half, axis=1) * sin_dn
            z_ref[:, j * LANES:(j + 1) * LANES] = rot * scale


def _mix_in(x, g, w_main, w_lr, w_a2, b_a, rope, *, tm=512):
    m, d = x.shape
    tm = min(tm, m)
    n_rope = rope.shape[1] // tm
    return pl.pallas_call(
        _mix_in_body,
        grid=(m // tm, Z_COLS // Z_TILE),
        in_specs=[
            pl.BlockSpec((tm, d), lambda i, n: (i, 0)),
            pl.BlockSpec((1, d), lambda i, n: (0, 0)),
            pl.BlockSpec((d, Z_TILE), lambda i, n: (0, n)),
            pl.BlockSpec((d, LANES), lambda i, n: (0, 0)),
            pl.BlockSpec((LANES, GLA_KEY_WIDTH), lambda i, n: (0, 0)),
            pl.BlockSpec((1, GLA_KEY_WIDTH), lambda i, n: (0, 0)),
            pl.BlockSpec((3, tm, LANES), lambda i, n: (0, i % n_rope, 0)),
        ],
        out_specs=[
            pl.BlockSpec((tm, Z_TILE), lambda i, n: (i, n)),
            pl.BlockSpec((tm, GLA_KEY_WIDTH), lambda i, n: (i, 0)),
        ],
        out_shape=[
            jax.ShapeDtypeStruct((m, Z_COLS), F32),
            jax.ShapeDtypeStruct((m, GLA_KEY_WIDTH), F32),
        ],
        scratch_shapes=[pltpu.VMEM((tm, d), BF16)],
        compiler_params=_params("parallel", "arbitrary"),
        name="mix_in",
    )(x, g.reshape(1, d), w_main, w_lr, w_a2, b_a.reshape(1, -1), rope)


def _rope_tables(pos):
    half = ROT_DIM // 2
    inv = ROPE_THETA ** (-jnp.arange(half, dtype=F32) * 2.0 / ROT_DIM)
    ang = pos.astype(F32)[:, None] * inv[None, :]
    cos, sin = jnp.cos(ang), jnp.sin(ang)
    t = pos.shape[0]
    zeros = jnp.zeros((t, half), F32)
    rest0 = jnp.zeros((t, ATT_HEAD_DIM - ROT_DIM), F32)
    c = jnp.concatenate([cos, cos, jnp.ones_like(rest0)], axis=1)
    s_up = jnp.concatenate([zeros, sin, rest0], axis=1)
    s_dn = jnp.concatenate([-sin, zeros, rest0], axis=1)
    reps = LANES // ATT_HEAD_DIM
    return jnp.stack([jnp.tile(c, (1, reps)), jnp.tile(s_up, (1, reps)), jnp.tile(s_dn, (1, reps))])


def _norm_proj_body(x_ref, g_ref, w_ref, o_ref):
    h = _rms(x_ref[...], g_ref[...]).astype(BF16)
    o_ref[...] = _dot(h, w_ref[...])


def _norm_proj(x, g, w, *, tm=512):
    m, d = x.shape
    n = w.shape[1]
    tm = min(tm, m)
    return pl.pallas_call(
        _norm_proj_body,
        grid=(m // tm,),
        in_specs=[
            pl.BlockSpec((tm, d), lambda i: (i, 0)),
            pl.BlockSpec((1, d), lambda i: (0, 0)),
            pl.BlockSpec((d, n), lambda i: (0, 0)),
        ],
        out_specs=pl.BlockSpec((tm, n), lambda i: (i, 0)),
        out_shape=jax.ShapeDtypeStruct((m, n), F32),
        compiler_params=_params("parallel"),
        name="memory_kv",
    )(x, g.reshape(1, d), w)


ATTN_UNROLL = 8
MERGE_ROWS = 256


def _band_bias():
    i = np.arange(BAND)[:, None]
    j = np.arange(2 * BAND)[None, :]
    out = []
    for off in (0, BAND):
        dist = off + i - j
        out.append(np.where((dist >= 0) & (dist <= N_BACK), 0.0, -np.inf))
    return jnp.asarray(np.stack(out), F32)


def _attn_prompt_body(q_ref, k_ref, v_ref, bias_ref, o_ref, acc_ref, m_ref, l_ref, *, seq):
    hd = ATT_HEAD_DIM
    assert LANES == 2 * hd
    blocks = seq // BAND
    first_head = lax.broadcasted_iota(jnp.int32, (1, LANES), 1) < hd
    for ci, dil in enumerate(DILATIONS):
        nb = blocks // dil
        shift = nb.bit_length() - 1
        assert nb == 1 << shift

        def step(it, carry, ci=ci, dil=dil, nb=nb, shift=shift):
            loaded = []
            for u in range(ATTN_UNROLL):
                w = it * ATTN_UNROLL + u
                r = lax.shift_right_logical(w, shift)
                n = w & (nb - 1)
                base = jnp.maximum(n - 1, 0) * BAND
                if dil == 1:
                    q_rows = pl.ds(pl.multiple_of(n * BAND, BAND), BAND)
                    k_rows = pl.ds(pl.multiple_of(base, BAND), 2 * BAND)
                else:
                    q_rows = pl.ds(r + dil * BAND * n, BAND, stride=dil)
                    k_rows = pl.ds(r + dil * base, 2 * BAND, stride=dil)
                bias = bias_ref[jnp.minimum(n, 1)]
                loaded.append((q_rows, q_ref[0, q_rows, :], k_ref[0, k_rows, :], v_ref[0, k_rows, :], bias))
            results = []
            for q_rows, q, k, v, bias in loaded:
                kb = k.astype(BF16)
                qs = (jnp.where(first_head, q, 0.0), jnp.where(first_head, 0.0, q))
                vs = (jnp.where(first_head, v, 1.0), jnp.where(first_head, 1.0, v))
                out, mx = [], []
                for hh in range(LANES // hd):
                    s = _dot_nt(qs[hh].astype(BF16), kb) + bias
                    m_c = jnp.max(s, axis=1, keepdims=True)
                    p = jnp.exp(s - m_c)
                    out.append(_dot(p.astype(BF16), vs[hh].astype(BF16)))
                    mx.append(m_c)
                results.append((q_rows, jnp.where(first_head, out[0], out[1]),
                                jnp.where(first_head, mx[0], mx[1]), jnp.where(first_head, out[1], out[0])))
            for q_rows, pv, mx, den_swapped in results:
                acc_ref[ci, q_rows, :] = pv
                m_ref[ci, q_rows, :] = mx
                l_ref[ci, q_rows, :] = den_swapped
            return carry

        lax.fori_loop(0, blocks // ATTN_UNROLL, step, 0)

    def merge(i, carry):
        rows = pl.ds(pl.multiple_of(i * MERGE_ROWS, MERGE_ROWS), MERGE_ROWS)
        ms = [m_ref[ci, rows, :] for ci in range(len(DILATIONS))]
        m_all = functools.reduce(jnp.maximum, ms)
        ws = [jnp.exp(m - m_all) for m in ms]
        num = sum(w * acc_ref[ci, rows, :] for ci, w in enumerate(ws))
        den = sum(w * pltpu.roll(l_ref[ci, rows, :], hd, axis=1) for ci, w in enumerate(ws))
        o_ref[0, rows, :] = (num / den).astype(o_ref.dtype)
        return carry

    lax.fori_loop(0, seq // MERGE_ROWS, merge, 0)


def _attn_prompt(z3):
    b, seq, _ = z3.shape
    pairs = ATT_WIDTH // LANES
    assert seq % (DILATIONS[-1] * BAND) == 0 and seq // DILATIONS[-1] >= 2 * BAND
    assert (seq // BAND) % ATTN_UNROLL == 0 and seq % MERGE_ROWS == 0
    blk = lambda off: pl.BlockSpec((1, seq, LANES), lambda i, p: (i, 0, off + p))
    return pl.pallas_call(
        functools.partial(_attn_prompt_body, seq=seq),
        grid=(b, pairs),
        in_specs=[blk(0), blk(pairs), blk(2 * pairs),
                  pl.BlockSpec((2, BAND, 2 * BAND), lambda i, p: (0, 0, 0))],
        out_specs=pl.BlockSpec((1, seq, LANES), lambda i, p: (i, 0, p)),
        out_shape=jax.ShapeDtypeStruct((b, seq, ATT_WIDTH), BF16),
        scratch_shapes=[pltpu.VMEM((len(DILATIONS), seq, LANES), F32)] * 3,
        compiler_params=_params("parallel", "parallel"),
        name="attn_prompt",
    )(z3, z3, z3, _band_bias())


SAMPLE_HEADS_PER_STEP = 8
SAMPLE_ROWS = 8


def _sample_multiplicity(n_buf, n_new):
    t = (np.arange(SAMPLE_ROWS) % n_new)[:, None]
    r = np.arange(n_buf)[None, :]
    mult = np.zeros((SAMPLE_ROWS, n_buf), np.float32)
    for dil in DILATIONS:
        back = n_buf + t - r
        mult += (back % dil == 0) & (back >= dil) & (back <= dil * N_BACK)
    return jnp.asarray(mult, F32)


def _attn_sample_scores(q_ref, kn_ref, vn_ref, kt_ref, vt_ref, mult_ref, *, n_new, heads):
    live = mult_ref[...] > 0.0
    mult_new = _new_row_multiplicity(n_new)
    scores = []
    for h in range(heads):
        q = q_ref[0, h]
        kn = kn_ref[0, h]
        s = jnp.where(live, _dot(q.astype(BF16), kt_ref[0, h].astype(BF16)), -jnp.inf)
        s_new = [jnp.where(mult_new[t] > 0.0, jnp.sum(q * kn[t:t + 1], axis=1, keepdims=True), -jnp.inf)
                 for t in range(n_new)]
        scores.append((s, s_new))
    return scores


def _attn_sample_outputs(q_ref, kn_ref, vn_ref, kt_ref, vt_ref, mult_ref, o_ref, scores, *, n_new, heads):
    mult = mult_ref[...]
    mult_new = _new_row_multiplicity(n_new)
    for h, (s, s_new) in enumerate(scores):
        vn = vn_ref[0, h]
        m = functools.reduce(jnp.maximum, s_new, jnp.max(s, axis=1, keepdims=True))
        w = mult * jnp.exp(s - m)
        w_new = [mult_new[t] * jnp.exp(s_new[t] - m) for t in range(n_new)]
        den = jnp.sum(w, axis=1, keepdims=True) + sum(w_new)
        o = _dot_nt(w.astype(BF16), vt_ref[0, h].astype(BF16))
        o = o + sum(w_new[t] * vn[t:t + 1] for t in range(n_new))
        o_ref[0, h] = o / den


def _new_row_multiplicity(n_new):
    t_row = lax.broadcasted_iota(jnp.int32, (SAMPLE_ROWS, 1), 0) % n_new
    return [(t_row >= t).astype(F32) + (len(DILATIONS) - 1.0) * (t_row == t).astype(F32) for t in range(n_new)]


class _Rider:
    def __init__(self, name, in_specs, args, out_spec, out_shape, first, second):
        self.name, self.in_specs, self.args, self.out_spec, self.out_shape = name, in_specs, args, out_spec, out_shape
        self.first, self.second, self.n_in = first, second, len(args)


def _attn_sample_rider(z3, cache_k, cache_v):
    n, n_new, _ = z3.shape
    n_buf = cache_k.shape[1]
    nh, hd = ATT_HEADS, ATT_HEAD_DIM
    assert n_buf >= DILATIONS[-1] * N_BACK and n_new <= SAMPLE_ROWS
    hs = SAMPLE_HEADS_PER_STEP
    groups = nh // hs
    blocks = n * groups

    def per_head(col):
        a = z3[:, :, col * ATT_WIDTH:(col + 1) * ATT_WIDTH].reshape(n, n_new, nh, hd).transpose(0, 2, 1, 3)
        return jnp.pad(a, ((0, 0), (0, 0), (0, SAMPLE_ROWS - n_new), (0, 0)))

    rows_on_lanes = lambda c: c.transpose(0, 2, 3, 1)
    args = [per_head(0), per_head(1), per_head(2), rows_on_lanes(cache_k), rows_on_lanes(cache_v),
            _sample_multiplicity(n_buf, n_new)]

    def make_rider(tiles, steps):
        per_tile = -(-blocks // tiles)
        assert per_tile <= steps and per_tile * tiles == blocks
        block_of = lambda i, f: jnp.minimum(i * per_tile + jnp.minimum(f, per_tile - 1), blocks - 1)
        where = lambda i, f: (block_of(i, f) // groups, block_of(i, f) % groups, 0, 0)
        small = pl.BlockSpec((1, hs, SAMPLE_ROWS, hd), where)
        cache = pl.BlockSpec((1, hs, hd, n_buf), where)
        return _Rider(
            "attn_sample",
            [small, small, small, cache, cache, pl.BlockSpec((SAMPLE_ROWS, n_buf), lambda i, f: (0, 0))],
            args, small, jax.ShapeDtypeStruct((n, nh, SAMPLE_ROWS, hd), F32),
            functools.partial(_attn_sample_scores, n_new=n_new, heads=hs),
            functools.partial(_attn_sample_outputs, n_new=n_new, heads=hs),
        )

    finish = lambda o: o[:, :, :n_new].transpose(0, 2, 1, 3).reshape(n, n_new, ATT_WIDTH)
    return make_rider, finish


def _gla_rows(q, k, v, li, states, causal, same, group_of_row):
    c = q.shape[0]
    hi, lo = _split_bf16(li)
    tri = causal.astype(BF16)
    b = _dot(tri, hi) + _dot(tri, lo)
    if len(states) == 1:
        b_end = b[c - 1:c, :]
    else:
        full = same.astype(BF16)
        b_end = _dot(full, hi) + _dot(full, lo)
    q_t = (q * (GLA_HEAD_K ** -0.5) * jnp.exp(b)).astype(BF16)
    k_n = (k * jnp.exp(-b)).astype(BF16)
    k_w = k * jnp.exp(b_end - b)
    vb = v.astype(BF16)
    a = jnp.where(causal, _dot_nt(q_t, k_n), 0.0)
    o = _dot(a.astype(BF16), vb)
    ones = jnp.ones((c, LANES), BF16)
    new_states = []
    for gi, s in enumerate(states):
        if len(states) == 1:
            o = o + _dot(q_t, s.astype(BF16))
            kw_g, hi_g, lo_g = k_w.astype(BF16), hi, lo
        else:
            mine = group_of_row == gi
            o = o + jnp.where(mine, _dot(q_t, s.astype(BF16)), 0.0)
            kw_g = jnp.where(mine, k_w, 0.0).astype(BF16)
            hi_g = jnp.where(mine, hi, jnp.zeros_like(hi))
            lo_g = jnp.where(mine, lo, jnp.zeros_like(lo))
        decay = jnp.exp(_dot_tn(hi_g, ones) + _dot_tn(lo_g, ones))
        decay = jnp.concatenate([decay] * (s.shape[1] // LANES), axis=1)
        new_states.append(decay * s + _dot_tn(kw_g, vb))
    return o, new_states


def _gla_finish(o, g_out, rg):
    o = o * lax.rsqrt(jnp.mean(o * o, axis=-1, keepdims=True) + EPS)
    return (o * g_out) * (rg * jax.nn.sigmoid(rg))


GLA_HEADS_PER_STEP = 4
GLA_SEQ_TILE = 1024
GLA_CHUNK_UNROLL = 2


def _gla_prompt_body(q_ref, k_ref, v_ref, rg_ref, la_ref, g_ref, o_ref, s_ref, *, rows, chunk, heads):
    hk, hv = GLA_HEAD_K, GLA_HEAD_V
    r = lax.broadcasted_iota(jnp.int32, (chunk, chunk), 0)
    c = lax.broadcasted_iota(jnp.int32, (chunk, chunk), 1)
    causal = c <= r

    @pl.when(pl.program_id(2) == 0)
    def _():
        s_ref[...] = jnp.zeros_like(s_ref)

    def step(i, carry):
        rr = pl.ds(pl.multiple_of(i * chunk, chunk), chunk)
        for hh in range(heads):
            ks = slice(hh * hk, (hh + 1) * hk)
            vs = slice(hh * hv, (hh + 1) * hv)
            o, (s_new,) = _gla_rows(q_ref[0, rr, ks], k_ref[0, rr, ks], v_ref[0, rr, vs], la_ref[0, rr, ks],
                                    [s_ref[0, hh]], causal, None, None)
            s_ref[0, hh] = s_new
            o_ref[0, rr, vs] = _gla_finish(o, g_ref[:, vs], rg_ref[0, rr, vs]).astype(o_ref.dtype)
        return carry

    lax.fori_loop(0, rows // chunk, step, 0, unroll=GLA_CHUNK_UNROLL)


def _gla_prompt(z3, la3, g_out):
    b, seq, _ = z3.shape
    hk, hv, nh = GLA_HEAD_K, GLA_HEAD_V, GLA_HEADS
    chunk = int(np.gcd(seq, GLA_CHUNK))
    hs = GLA_HEADS_PER_STEP
    rows = min(GLA_SEQ_TILE, seq)
    assert seq % rows == 0 and (rows // chunk) % GLA_CHUNK_UNROLL == 0 and nh % hs == 0
    q0 = 3 * ATT_WIDTH // (hs * hk)
    k0 = q0 + nh // hs
    v0 = (3 * ATT_WIDTH + 2 * GLA_KEY_WIDTH) // (hs * hv)
    r0 = v0 + nh // hs
    return pl.pallas_call(
        functools.partial(_gla_prompt_body, rows=rows, chunk=chunk, heads=hs),
        grid=(b, nh // hs, seq // rows),
        in_specs=[
            pl.BlockSpec((1, rows, hs * hk), lambda i, h, t: (i, t, q0 + h)),
            pl.BlockSpec((1, rows, hs * hk), lambda i, h, t: (i, t, k0 + h)),
            pl.BlockSpec((1, rows, hs * hv), lambda i, h, t: (i, t, v0 + h)),
            pl.BlockSpec((1, rows, hs * hv), lambda i, h, t: (i, t, r0 + h)),
            pl.BlockSpec((1, rows, hs * hk), lambda i, h, t: (i, t, h)),
            pl.BlockSpec((1, hs * hv), lambda i, h, t: (0, h)),
        ],
        out_specs=[
            pl.BlockSpec((1, rows, hs * hv), lambda i, h, t: (i, t, h)),
            pl.BlockSpec((1, hs, hk, hv), lambda i, h, t: (i, h, 0, 0)),
        ],
        out_shape=[
            jax.ShapeDtypeStruct((b, seq, GLA_VALUE_WIDTH), BF16),
            jax.ShapeDtypeStruct((b, nh, hk, hv), F32),
        ],
        compiler_params=_params("parallel", "parallel", "arbitrary"),
        name="gla_prompt",
    )(z3, z3, z3, z3, la3, g_out.reshape(1, -1))


def _gla_sample_body(q_ref, k_ref, v_ref, rg_ref, la_ref, g_ref, s0_ref, o_ref, s_ref, *, n_new, groups):
    rows = n_new * groups
    r = lax.broadcasted_iota(jnp.int32, (rows, rows), 0)
    c = lax.broadcasted_iota(jnp.int32, (rows, rows), 1)
    same = (r // n_new) == (c // n_new)
    causal = same & (c <= r)
    group_of_row = lax.broadcasted_iota(jnp.int32, (rows, 1), 0) // n_new
    o, new_states = _gla_rows(q_ref[...], k_ref[...], v_ref[...], la_ref[...],
                              [s0_ref[gi, 0] for gi in range(groups)], causal, same, group_of_row)
    for gi in range(groups):
        s_ref[gi, 0] = new_states[gi]
    o_ref[...] = _gla_finish(o, g_ref[...], rg_ref[...]).astype(o_ref.dtype)


def _gla_sample(z, la, g_out, state, n_new):
    m = z.shape[0]
    n = m // n_new
    hk, hv, nh = GLA_HEAD_K, GLA_HEAD_V, GLA_HEADS
    groups = min(SAMPLE_GROUP, n)
    rows = groups * n_new
    q0 = 3 * ATT_WIDTH // hk
    k0 = q0 + nh
    v0 = (3 * ATT_WIDTH + 2 * GLA_KEY_WIDTH) // hv
    r0 = v0 + nh
    state_spec = pl.BlockSpec((groups, 1, hk, hv), lambda i, h: (i, h, 0, 0))
    return pl.pallas_call(
        functools.partial(_gla_sample_body, n_new=n_new, groups=groups),
        grid=(n // groups, nh),
        in_specs=[
            pl.BlockSpec((rows, hk), lambda i, h: (i, q0 + h)),
            pl.BlockSpec((rows, hk), lambda i, h: (i, k0 + h)),
            pl.BlockSpec((rows, hv), lambda i, h: (i, v0 + h)),
            pl.BlockSpec((rows, hv), lambda i, h: (i, r0 + h)),
            pl.BlockSpec((rows, hk), lambda i, h: (i, h)),
            pl.BlockSpec((1, hv), lambda i, h: (0, h)),
            state_spec,
        ],
        out_specs=[pl.BlockSpec((rows, hv), lambda i, h: (i, h)), state_spec],
        out_shape=[
            jax.ShapeDtypeStruct((m, GLA_VALUE_WIDTH), BF16),
            jax.ShapeDtypeStruct(state.shape, F32),
        ],
        compiler_params=_params("parallel", "parallel"),
        name="gla_sample",
    )(z, z, z, z, la, g_out.reshape(1, -1), state)


def _out_cross_q_body(x_ref, oa_ref, og_ref, wa_ref, wg_ref, g_ref, wq_ref, x2_ref, qc_ref):
    oa = oa_ref[...].astype(BF16)
    x2 = x_ref[...] + _dot(oa, wa_ref[...]) + _dot(og_ref[...], wg_ref[...])
    x2_ref[...] = x2
    h = _rms(x2, g_ref[...]).astype(BF16)
    qc_ref[...] = (_dot(h, wq_ref[...]) * (MEM_HEAD_DIM ** -0.5)).astype(qc_ref.dtype)


def _out_cross_q(x, o_att, o_gla, w_out_att, w_out_gla, g_cross, w_q, *, tm=512):
    m, d = x.shape
    tm = min(tm, m)
    const = lambda shape: pl.BlockSpec(shape, lambda i: (0, 0))
    return pl.pallas_call(
        _out_cross_q_body,
        grid=(m // tm,),
        in_specs=[
            pl.BlockSpec((tm, d), lambda i: (i, 0)),
            pl.BlockSpec((tm, ATT_WIDTH), lambda i: (i, 0)),
            pl.BlockSpec((tm, GLA_VALUE_WIDTH), lambda i: (i, 0)),
            const(w_out_att.shape),
            const(w_out_gla.shape),
            const((1, d)),
            const(w_q.shape),
        ],
        out_specs=[pl.BlockSpec((tm, d), lambda i: (i, 0)), pl.BlockSpec((tm, MEM_WIDTH), lambda i: (i, 0))],
        out_shape=[jax.ShapeDtypeStruct((m, d), F32), jax.ShapeDtypeStruct((m, MEM_WIDTH), BF16)],
        compiler_params=_params("parallel"),
        name="out_cross_q",
    )(x, o_att, o_gla, w_out_att, w_out_gla, g_cross.reshape(1, d), w_q)


def _cross_heads(q, mk_ref, mv_ref):
    hd = MEM_HEAD_DIM
    heads = []
    for h in range(MEM_HEADS):
        s = _dot_nt(q[:, h * hd:(h + 1) * hd], mk_ref[0, h].astype(BF16))
        p = jnp.exp(s - jnp.max(s, axis=-1, keepdims=True))
        heads.append(_dot(p.astype(BF16), mv_ref[0, h].astype(BF16)) / jnp.sum(p, axis=-1, keepdims=True))
    return jnp.concatenate(heads, axis=1).astype(BF16)


def _cross_body(q_ref, mk_ref, mv_ref, x_ref, w_ref, o_ref, *, groups):
    rows = q_ref.shape[0]
    per_mem = MEM_LEN * MEM_HEADS
    keys = groups * per_mem
    hd = MEM_HEAD_DIM
    k_all = mk_ref[...].reshape(keys, hd).astype(BF16)
    v_all = mv_ref[...].reshape(keys, hd).astype(BF16)
    col = lax.broadcasted_iota(jnp.int32, (1, keys), 1)
    group_of_row = lax.broadcasted_iota(jnp.int32, (rows, 1), 0) // (rows // groups)
    own = (col // per_mem) == group_of_row
    head_of_col = col % MEM_HEADS
    heads = []
    for h in range(MEM_HEADS):
        s = jnp.where(own & (head_of_col == h), _dot_nt(q_ref[:, h * hd:(h + 1) * hd], k_all), -jnp.inf)
        p = jnp.exp(s - jnp.max(s, axis=-1, keepdims=True))
        heads.append(_dot(p.astype(BF16), v_all) / jnp.sum(p, axis=-1, keepdims=True))
    o = jnp.concatenate(heads, axis=1).astype(BF16)
    o_ref[...] = x_ref[...] + _dot(o, w_ref[...])


def _post_mix_body(x_ref, oa_ref, og_ref, wa_ref, wg_ref, g_ref, wq_ref, mk_ref, mv_ref, wo_ref, o_ref):
    x2 = x_ref[...] + _dot(oa_ref[...].astype(BF16), wa_ref[...]) + _dot(og_ref[...], wg_ref[...])
    h = _rms(x2, g_ref[...]).astype(BF16)
    q = (_dot(h, wq_ref[...]) * (MEM_HEAD_DIM ** -0.5)).astype(BF16)
    o_ref[...] = x2 + _dot(_cross_heads(q, mk_ref, mv_ref), wo_ref[...])


def _post_mix(x, o_att, o_gla, w_out_att, w_out_gla, g_cross, w_q, mem_k, mem_v, w_o, *, seq, tm=512):
    m, d = x.shape
    tm = min(tm, seq)
    assert seq % tm == 0
    const = lambda a: pl.BlockSpec(a.shape, lambda i: (0,) * a.ndim, pipeline_mode=pl.Buffered(1))
    rows = lambda width: pl.BlockSpec((tm, width), lambda i: (i, 0))
    mem = pl.BlockSpec((1, MEM_HEADS, MEM_LEN, MEM_HEAD_DIM), lambda i: (i // (seq // tm), 0, 0, 0))
    g = g_cross.reshape(1, d)
    return pl.pallas_call(
        _post_mix_body,
        grid=(m // tm,),
        in_specs=[rows(d), rows(ATT_WIDTH), rows(GLA_VALUE_WIDTH), const(w_out_att), const(w_out_gla), const(g),
                  const(w_q), mem, mem, const(w_o)],
        out_specs=rows(d),
        out_shape=jax.ShapeDtypeStruct((m, d), F32),
        compiler_params=_params("parallel"),
        name="post_mix",
    )(x, o_att, o_gla, w_out_att, w_out_gla, g, w_q, mem_k, mem_v, w_o)


def _cross(qc, mem_k, mem_v, x, w_o, *, rows, groups):
    m, d = x.shape
    mem_k, mem_v = (a.reshape(a.shape[0], MEM_LEN * MEM_HEADS, MEM_HEAD_DIM) for a in (mem_k, mem_v))
    mem = pl.BlockSpec((groups, MEM_LEN * MEM_HEADS, MEM_HEAD_DIM), lambda i: (i, 0, 0))
    return pl.pallas_call(
        functools.partial(_cross_body, groups=groups),
        grid=(m // rows,),
        in_specs=[
            pl.BlockSpec((rows, MEM_WIDTH), lambda i: (i, 0)),
            mem,
            mem,
            pl.BlockSpec((rows, d), lambda i: (i, 0)),
            pl.BlockSpec(w_o.shape, lambda i: (0, 0)),
        ],
        out_specs=pl.BlockSpec((rows, d), lambda i: (i, 0)),
        out_shape=jax.ShapeDtypeStruct((m, d), F32),
        compiler_params=_params("parallel"),
        name="cross",
    )(qc, mem_k, mem_v, x, w_o)


def kernel(x_prompt, x_sample, mem_prompt, cache_win_k, cache_win_v, state_gla, cache_mem_k, cache_mem_v, g_ffn1, w1_gate, w1_up, w1_down, g_mix, w_in, w_gla_a2, b_gla_a, g_gla_out, w_out, g_mem, w_mem_k, w_mem_v, g_cross, w_cross_q, w_cross_o, g_ffn2, w2_gate, w2_up, w2_down, g_final):
    b, seq, d = x_prompt.shape
    n, n_new, _ = x_sample.shape
    depth = g_ffn1.shape[0]
    past_len = cache_win_k.shape[2]
    n_keep = min(MAX_WINDOW, seq)
    bf = lambda a: a.astype(BF16)

    xp = x_prompt.reshape(b * seq, d)
    xs = x_sample.reshape(n * n_new, d)
    rope_p = _rope_tables(jnp.arange(seq))
    rope_s = _rope_tables(jnp.tile(past_len + jnp.arange(n_new), n))
    outs = [[] for _ in range(8)]
    cols = np.cumsum([0, ATT_WIDTH, ATT_WIDTH, ATT_WIDTH, GLA_KEY_WIDTH, GLA_KEY_WIDTH, GLA_VALUE_WIDTH, GLA_GATE_RANK])
    for l in range(depth):
        w_main = bf(jnp.concatenate([w_in[l][:, :cols[6]], w_in[l][:, cols[7]:]], axis=1))
        w_lr = bf(jnp.pad(w_in[l][:, cols[6]:cols[7]], ((0, 0), (0, LANES - GLA_GATE_RANK))))
        w_a2 = bf(jnp.pad(w_gla_a2[l], ((0, LANES - GLA_GATE_RANK), (0, 0))))
        w_out_att, w_out_gla = bf(w_out[l][:ATT_WIDTH]), bf(w_out[l][ATT_WIDTH:])
        w_mem = bf(jnp.concatenate([w_mem_k[l], w_mem_v[l]], axis=1))
        w_cq, w_co = bf(w_cross_q[l]), bf(w_cross_o[l])
        gf = g_final if l == depth - 1 else None
        mix_in = lambda x, rope: _mix_in(x, g_mix[l], w_main, w_lr, w_a2, b_gla_a[l], rope)

        xs, *w1 = _ffn(xs, g_ffn1[l], w1_gate[l], w1_up[l], w1_down[l])
        zs, la_s = mix_in(xs, rope_s)
        attn_rider, attn_finish = _attn_sample_rider(zs.reshape(n, n_new, Z_COLS), cache_win_k[l], cache_win_v[l])
        xp, o_att_s = _ffn(xp, g_ffn1[l], *w1, rider=attn_rider)
        o_att_s = attn_finish(o_att_s).reshape(n * n_new, ATT_WIDTH)
        o_gla_s, st_s = _gla_sample(zs, la_s, g_gla_out[l], state_gla[l], n_new)
        xs, qc = _out_cross_q(xs, o_att_s, o_gla_s, w_out_att, w_out_gla, g_cross[l], w_cq)
        groups = min(SAMPLE_GROUP, n)
        xs = _cross(qc, cache_mem_k[l], cache_mem_v[l], xs, w_co, rows=groups * n_new, groups=groups)
        xs, *w2 = _ffn(xs, g_ffn2[l], w2_gate[l], w2_up[l], w2_down[l], g_final=gf)

        mem_kv = _norm_proj(mem_prompt.reshape(b * MEM_LEN, d), g_mem[l], w_mem).reshape(b, MEM_LEN, 2 * MEM_WIDTH)
        mem_k = mem_kv[:, :, :MEM_WIDTH].reshape(b, MEM_LEN, MEM_HEADS, MEM_HEAD_DIM)
        mem_v = mem_kv[:, :, MEM_WIDTH:].reshape(b, MEM_LEN, MEM_HEADS, MEM_HEAD_DIM)
        zp, la_p = mix_in(xp, rope_p)
        zp3 = zp.reshape(b, seq, Z_COLS)
        o_att_p = _attn_prompt(zp3).reshape(b * seq, ATT_WIDTH)
        o_gla_p, st_p = _gla_prompt(zp3, la_p.reshape(b, seq, GLA_KEY_WIDTH), g_gla_out[l])
        xp = _post_mix(xp, o_att_p, o_gla_p.reshape(b * seq, GLA_VALUE_WIDTH), w_out_att, w_out_gla, g_cross[l], w_cq,
                       mem_k.transpose(0, 2, 1, 3), mem_v.transpose(0, 2, 1, 3), w_co, seq=seq)
        xp = _ffn(xp, g_ffn2[l], *w2, g_final=gf)

        heads = lambda a, t: a.reshape(-1, t, ATT_HEADS, ATT_HEAD_DIM)
        outs[0].append(heads(zp3[:, seq - n_keep:, ATT_WIDTH:2 * ATT_WIDTH], n_keep))
        outs[1].append(heads(zp3[:, seq - n_keep:, 2 * ATT_WIDTH:3 * ATT_WIDTH], n_keep))
        outs[2].append(st_p)
        outs[3].append(mem_k)
        outs[4].append(mem_v)
        outs[5].append(heads(zs[:, ATT_WIDTH:2 * ATT_WIDTH], n_new))
        outs[6].append(heads(zs[:, 2 * ATT_WIDTH:3 * ATT_WIDTH], n_new))
        outs[7].append(st_s)

    return (xp.reshape(b, seq, d), xs.reshape(n, n_new, d)) + tuple(jnp.stack(o) for o in outs)
```

```python
import functools

import jax
import jax.numpy as jnp
import numpy as np
from jax import lax
from jax.experimental import pallas as pl
from jax.experimental.pallas import tpu as pltpu

F32 = jnp.float32
BF16 = jnp.bfloat16

EPS = 1e-6
ATT_HEADS = 16
ATT_HEAD_DIM = 64
ATT_WIDTH = ATT_HEADS * ATT_HEAD_DIM
DILATIONS = (1, 4, 16)
N_BACK = 128
BAND = 128
ROT_DIM = 16
ROPE_THETA = 500000.0
GLA_HEADS = 4
GLA_HEAD_K = 128
GLA_HEAD_V = 256
GLA_KEY_WIDTH = GLA_HEADS * GLA_HEAD_K
GLA_VALUE_WIDTH = GLA_HEADS * GLA_HEAD_V
GLA_GATE_RANK = 16
GLA_GATE_TAU = 16.0
GLA_CHUNK = 64
MEM_LEN = 256
MEM_HEADS = 4
MEM_HEAD_DIM = 128
MEM_WIDTH = MEM_HEADS * MEM_HEAD_DIM
MAX_WINDOW = 2048
LANES = 128
SAMPLE_GROUP = 8
Z_COLS = 3 * ATT_WIDTH + 2 * GLA_KEY_WIDTH + 2 * GLA_VALUE_WIDTH
Z_TILE = 1024
VMEM_LIMIT_BYTES = 56 * 1024 * 1024


def _params(*semantics, vmem_limit_bytes=VMEM_LIMIT_BYTES):
    return pltpu.CompilerParams(dimension_semantics=semantics, vmem_limit_bytes=vmem_limit_bytes)


def _rms(x, g):
    return x * lax.rsqrt(jnp.mean(x * x, axis=-1, keepdims=True) + EPS) * g


def _dot(a, b):
    return jnp.dot(a, b, preferred_element_type=F32)


def _dot_nt(a, b):
    return lax.dot_general(a, b, (((1,), (1,)), ((), ())), preferred_element_type=F32)


def _dot_tn(a, b):
    return lax.dot_general(a, b, (((0,), (0,)), ((), ())), preferred_element_type=F32)


def _split_bf16(x):
    hi = x.astype(BF16)
    lo = (x - hi.astype(F32)).astype(BF16)
    return hi, lo


RIDER_HOST_ROWS = 1024
RIDER_VMEM_LIMIT_BYTES = 62 * 1024 * 1024


def _ffn_body(*refs, nf, final, emit, rider):
    x_ref, g_ref, wg_ref, wu_ref, wd_ref = refs[:5]
    refs = refs[5:]
    if final:
        gf_ref, refs = refs[0], refs[1:]
    if rider is not None:
        rider_in, refs = refs[:len(rider.args)], refs[len(rider.args):]
        rider_out = refs[-1 - len(rider.out_specs):-1]
    o_ref, h_ref = refs[0], refs[-1]
    f = pl.program_id(1)

    @pl.when(f == 0)
    def _():
        h_ref[...] = _rms(x_ref[...], g_ref[...]).astype(BF16)
        o_ref[...] = jnp.zeros_like(o_ref)

    def step(with_rider):
        wg, wu, wd = wg_ref[...].astype(BF16), wu_ref[...].astype(BF16), wd_ref[...].astype(BF16)
        if emit:
            for w_out_ref, w in zip(refs[1:4], (wg, wu, wd)):
                w_out_ref[...] = w
        h = h_ref[...]
        gate = _dot(h, wg)
        if with_rider:
            carried = rider.first(*rider_in)
        up = _dot(h, wu)
        if with_rider:
            rider.second(*rider_in, *rider_out, carried)
        act = (gate * jax.nn.sigmoid(gate) * up).astype(BF16)
        o_ref[...] += _dot(act, wd)

    if rider is None:
        step(False)
    else:
        active = rider.active(f)
        pl.when(active)(functools.partial(step, True))
        pl.when(jnp.logical_not(active))(functools.partial(step, False))

    @pl.when(f == nf - 1)
    def _():
        y = x_ref[...] + 0.5 * o_ref[...]
        if final:
            y = _rms(y, gf_ref[...])
        o_ref[...] = y


def _ffn(x, g, wg, wu, wd, g_final=None, *, tm=512, tf=None, rider=None):
    m, d = x.shape
    dff = wg.shape[1]
    if rider is not None:
        tm = RIDER_HOST_ROWS
    tm = min(tm, m)
    emit = wg.dtype != BF16
    if tf is None:
        tf = 256 if emit or rider is not None else 512
    nf = dff // tf
    assert dff % tf == 0 and m % tm == 0
    assert not emit or m == tm
    final = g_final is not None
    row = pl.BlockSpec((1, d), lambda i, f: (0, 0))
    w_specs = [
        pl.BlockSpec((d, tf), lambda i, f: (0, f)),
        pl.BlockSpec((d, tf), lambda i, f: (0, f)),
        pl.BlockSpec((tf, d), lambda i, f: (f, 0)),
    ]
    x_mode = {} if rider is None else dict(pipeline_mode=pl.Buffered(1))
    in_specs = [pl.BlockSpec((tm, d), lambda i, f: (i, 0), **x_mode), row] + w_specs
    args = [x, g.reshape(1, d), wg, wu, wd]
    if final:
        in_specs.append(row)
        args.append(g_final.reshape(1, d))
    out_specs = [pl.BlockSpec((tm, d), lambda i, f: (i, 0))]
    out_shape = [jax.ShapeDtypeStruct((m, d), F32)]
    if emit:
        out_specs += w_specs
        out_shape += [jax.ShapeDtypeStruct(w.shape, BF16) for w in (wg, wu, wd)]
    if rider is not None:
        rider = rider(m // tm, nf)
        in_specs += rider.in_specs
        args += rider.args
        out_specs += rider.out_specs
        out_shape += rider.out_shape
    out = pl.pallas_call(
        functools.partial(_ffn_body, nf=nf, final=final, emit=emit, rider=rider),
        grid=(m // tm, nf),
        in_specs=in_specs,
        out_specs=out_specs,
        out_shape=out_shape,
        scratch_shapes=[pltpu.VMEM((tm, d), BF16)],
        compiler_params=(_params("parallel", "arbitrary") if rider is None else
                         _params("arbitrary", "arbitrary", vmem_limit_bytes=RIDER_VMEM_LIMIT_BYTES)),
        name=("ffn_final" if final else "ffn") + ("_f32w" if emit else "") + ("" if rider is None else "_" + rider.name),
    )(*args)
    return out if len(out) > 1 else out[0]


def _mix_in_body(x_ref, g_ref, w_ref, wlr_ref, wa2_ref, ba_ref, rope_ref, z_ref, la_ref, h_ref):
    n = pl.program_id(1)

    @pl.when(n == 0)
    def _():
        h = _rms(x_ref[...], g_ref[...]).astype(BF16)
        h_ref[...] = h
        g_lr = _dot(h, wlr_ref[...])
        pre = _dot(g_lr.astype(BF16), wa2_ref[...]) + ba_ref[...]
        la_ref[...] = (jnp.minimum(pre, 0.0) - jnp.log1p(jnp.exp(-jnp.abs(pre)))) * (1.0 / GLA_GATE_TAU)

    z = _dot(h_ref[...], w_ref[...])

    @pl.when(n >= 2)
    def _():
        z_ref[...] = z

    @pl.when(n < 2)
    def _():
        cos, sin_up, sin_dn = rope_ref[0], rope_ref[1], rope_ref[2]
        scale = jnp.where(n == 0, ATT_HEAD_DIM ** -0.5, 1.0).astype(F32)
        half = ROT_DIM // 2
        for j in range(Z_TILE // LANES):
            zc = z[:, j * LANES:(j + 1) * LANES]
            rot = zc * cos + pltpu.roll(zc, half, axis=1) * sin_up + pltpu.roll(zc, LANES - half, axis=1) * sin_dn
            z_ref[:, j * LANES:(j + 1) * LANES] = rot * scale


def _mix_in(x, g, w_main, w_lr, w_a2, b_a, rope, *, tm=512):
    m, d = x.shape
    tm = min(tm, m)
    n_rope = rope.shape[1] // tm
    return pl.pallas_call(
        _mix_in_body,
        grid=(m // tm, Z_COLS // Z_TILE),
        in_specs=[
            pl.BlockSpec((tm, d), lambda i, n: (i, 0)),
            pl.BlockSpec((1, d), lambda i, n: (0, 0)),
            pl.BlockSpec((d, Z_TILE), lambda i, n: (0, n)),
            pl.BlockSpec((d, LANES), lambda i, n: (0, 0)),
            pl.BlockSpec((LANES, GLA_KEY_WIDTH), lambda i, n: (0, 0)),
            pl.BlockSpec((1, GLA_KEY_WIDTH), lambda i, n: (0, 0)),
            pl.BlockSpec((3, tm, LANES), lambda i, n: (0, i % n_rope, 0)),
        ],
        out_specs=[
            pl.BlockSpec((tm, Z_TILE), lambda i, n: (i, n)),
            pl.BlockSpec((tm, GLA_KEY_WIDTH), lambda i, n: (i, 0)),
        ],
        out_shape=[
            jax.ShapeDtypeStruct((m, Z_COLS), F32),
            jax.ShapeDtypeStruct((m, GLA_KEY_WIDTH), F32),
        ],
        scratch_shapes=[pltpu.VMEM((tm, d), BF16)],
        compiler_params=_params("parallel", "arbitrary"),
        name="mix_in",
    )(x, g.reshape(1, d), w_main, w_lr, w_a2, b_a.reshape(1, -1), rope)


def _rope_tables(pos):
    half = ROT_DIM // 2
    inv = ROPE_THETA ** (-jnp.arange(half, dtype=F32) * 2.0 / ROT_DIM)
    ang = pos.astype(F32)[:, None] * inv[None, :]
    cos, sin = jnp.cos(ang), jnp.sin(ang)
    t = pos.shape[0]
    zeros = jnp.zeros((t, half), F32)
    rest0 = jnp.zeros((t, ATT_HEAD_DIM - ROT_DIM), F32)
    c = jnp.concatenate([cos, cos, jnp.ones_like(rest0)], axis=1)
    s_up = jnp.concatenate([zeros, sin, rest0], axis=1)
    s_dn = jnp.concatenate([-sin, zeros, rest0], axis=1)
    reps = LANES // ATT_HEAD_DIM
    return jnp.stack([jnp.tile(c, (1, reps)), jnp.tile(s_up, (1, reps)), jnp.tile(s_dn, (1, reps))])


def _norm_proj_body(x_ref, g_ref, w_ref, o_ref):
    h = _rms(x_ref[...], g_ref[...]).astype(BF16)
    o_ref[...] = _dot(h, w_ref[...])


def _norm_proj(x, g, w, *, tm=512):
    m, d = x.shape
    n = w.shape[1]
    tm = min(tm, m)
    return pl.pallas_call(
        _norm_proj_body,
        grid=(m // tm,),
        in_specs=[
            pl.BlockSpec((tm, d), lambda i: (i, 0)),
            pl.BlockSpec((1, d), lambda i: (0, 0)),
            pl.BlockSpec((d, n), lambda i: (0, 0)),
        ],
        out_specs=pl.BlockSpec((tm, n), lambda i: (i, 0)),
        out_shape=jax.ShapeDtypeStruct((m, n), F32),
        compiler_params=_params("parallel"),
        name="memory_kv",
    )(x, g.reshape(1, d), w)


ATTN_UNROLL = 8
MERGE_ROWS = 256


def _band_bias():
    i = np.arange(BAND)[:, None]
    j = np.arange(2 * BAND)[None, :]
    out = []
    for off in (0, BAND):
        dist = off + i - j
        out.append(np.where((dist >= 0) & (dist <= N_BACK), 0.0, -np.inf))
    return jnp.asarray(np.stack(out), F32)


def _attn_prompt_body(q_ref, k_ref, v_ref, bias_ref, o_ref, acc_ref, m_ref, l_ref, *, seq):
    hd = ATT_HEAD_DIM
    assert LANES == 2 * hd
    blocks = seq // BAND
    first_head = lax.broadcasted_iota(jnp.int32, (1, LANES), 1) < hd
    for ci, dil in enumerate(DILATIONS):
        nb = blocks // dil
        shift = nb.bit_length() - 1
        assert nb == 1 << shift

        def step(it, carry, ci=ci, dil=dil, nb=nb, shift=shift):
            loaded = []
            for u in range(ATTN_UNROLL):
                w = it * ATTN_UNROLL + u
                r = lax.shift_right_logical(w, shift)
                n = w & (nb - 1)
                base = jnp.maximum(n - 1, 0) * BAND
                if dil == 1:
                    q_rows = pl.ds(pl.multiple_of(n * BAND, BAND), BAND)
                    k_rows = pl.ds(pl.multiple_of(base, BAND), 2 * BAND)
                else:
                    q_rows = pl.ds(r + dil * BAND * n, BAND, stride=dil)
                    k_rows = pl.ds(r + dil * base, 2 * BAND, stride=dil)
                bias = bias_ref[jnp.minimum(n, 1)]
                loaded.append((q_rows, q_ref[0, q_rows, :], k_ref[0, k_rows, :], v_ref[0, k_rows, :], bias))
            results = []
            for q_rows, q, k, v, bias in loaded:
                kb = k.astype(BF16)
                qs = (jnp.where(first_head, q, 0.0), jnp.where(first_head, 0.0, q))
                vs = (jnp.where(first_head, v, 1.0), jnp.where(first_head, 1.0, v))
                out, mx = [], []
                for hh in range(LANES // hd):
                    s = _dot_nt(qs[hh].astype(BF16), kb) + bias
                    m_c = jnp.max(s, axis=1, keepdims=True)
                    p = jnp.exp(s - m_c)
                    out.append(_dot(p.astype(BF16), vs[hh].astype(BF16)))
                    mx.append(m_c)
                results.append((q_rows, jnp.where(first_head, out[0], out[1]),
                                jnp.where(first_head, mx[0], mx[1]), jnp.where(first_head, out[1], out[0])))
            for q_rows, pv, mx, den_swapped in results:
                acc_ref[ci, q_rows, :] = pv
                m_ref[ci, q_rows, :] = mx
                l_ref[ci, q_rows, :] = den_swapped
            return carry

        lax.fori_loop(0, blocks // ATTN_UNROLL, step, 0)

    def merge(i, carry):
        rows = pl.ds(pl.multiple_of(i * MERGE_ROWS, MERGE_ROWS), MERGE_ROWS)
        ms = [m_ref[ci, rows, :] for ci in range(len(DILATIONS))]
        m_all = functools.reduce(jnp.maximum, ms)
        ws = [jnp.exp(m - m_all) for m in ms]
        num = sum(w * acc_ref[ci, rows, :] for ci, w in enumerate(ws))
        den = sum(w * pltpu.roll(l_ref[ci, rows, :], hd, axis=1) for ci, w in enumerate(ws))
        o_ref[0, rows, :] = (num / den).astype(o_ref.dtype)
        return carry

    lax.fori_loop(0, seq // MERGE_ROWS, merge, 0)


def _attn_prompt(z3):
    b, seq, _ = z3.shape
    pairs = ATT_WIDTH // LANES
    assert seq % (DILATIONS[-1] * BAND) == 0 and seq // DILATIONS[-1] >= 2 * BAND
    assert (seq // BAND) % ATTN_UNROLL == 0 and seq % MERGE_ROWS == 0
    blk = lambda off: pl.BlockSpec((1, seq, LANES), lambda i, p: (i, 0, off + p))
    return pl.pallas_call(
        functools.partial(_attn_prompt_body, seq=seq),
        grid=(b, pairs),
        in_specs=[blk(0), blk(pairs), blk(2 * pairs),
                  pl.BlockSpec((2, BAND, 2 * BAND), lambda i, p: (0, 0, 0))],
        out_specs=pl.BlockSpec((1, seq, LANES), lambda i, p: (i, 0, p)),
        out_shape=jax.ShapeDtypeStruct((b, seq, ATT_WIDTH), BF16),
        scratch_shapes=[pltpu.VMEM((len(DILATIONS), seq, LANES), F32)] * 3,
        compiler_params=_params("parallel", "parallel"),
        name="attn_prompt",
    )(z3, z3, z3, _band_bias())


SAMPLE_ROWS = 8


def _sample_multiplicity(n_buf, n_new):
    t = (np.arange(SAMPLE_ROWS) % n_new)[:, None]
    r = np.arange(n_buf)[None, :]
    mult = np.zeros((SAMPLE_ROWS, n_buf), np.float32)
    for dil in DILATIONS:
        back = n_buf + t - r
        mult += (back % dil == 0) & (back >= dil) & (back <= dil * N_BACK)
    return jnp.asarray(mult, F32)


def _new_row_multiplicity(n_new):
    t_row = lax.broadcasted_iota(jnp.int32, (SAMPLE_ROWS, 1), 0) % n_new
    return [(t_row >= t).astype(F32) + (len(DILATIONS) - 1.0) * (t_row == t).astype(F32) for t in range(n_new)]


def _attn_scores(q_ref, kn_ref, kt_ref, mult_ref, *, n_new):
    live = mult_ref[...] > 0.0
    mult_new = _new_row_multiplicity(n_new)
    scores = []
    for h in range(ATT_HEADS):
        q = q_ref[0, h]
        kn = kn_ref[0, h]
        s = jnp.where(live, _dot(q.astype(BF16), kt_ref[0, h].astype(BF16)), -jnp.inf)
        s_new = [jnp.where(mult_new[t] > 0.0, jnp.sum(q * kn[t:t + 1], axis=1, keepdims=True), -jnp.inf)
                 for t in range(n_new)]
        scores.append((s, s_new))
    return scores


def _attn_weights(q_ref, kn_ref, kt_ref, mult_ref, w_ref, aux_ref, scores, *, n_new):
    mult = mult_ref[...]
    mult_new = _new_row_multiplicity(n_new)
    lane = lax.broadcasted_iota(jnp.int32, (1, LANES), 1)
    for h, (s, s_new) in enumerate(scores):
        m = functools.reduce(jnp.maximum, s_new, jnp.max(s, axis=1, keepdims=True))
        w = mult * jnp.exp(s - m)
        w_new = [mult_new[t] * jnp.exp(s_new[t] - m) for t in range(n_new)]
        den = jnp.sum(w, axis=1, keepdims=True) + sum(w_new)
        w_ref[0, h] = w
        aux_ref[0, h] = sum(jnp.where(lane == t, w_new[t], 0.0) for t in range(n_new)) + jnp.where(lane == n_new, den, 0.0)


def _attn_value_products(w_ref, aux_ref, vn_ref, vt_ref):
    return [_dot_nt(w_ref[0, h].astype(BF16), vt_ref[0, h].astype(BF16)) for h in range(ATT_HEADS)]


def _attn_outputs(w_ref, aux_ref, vn_ref, vt_ref, o_ref, products, *, n_new):
    for h, o in enumerate(products):
        aux = aux_ref[0, h]
        vn = vn_ref[0, h]
        o = o + sum(aux[:, t:t + 1] * vn[t:t + 1] for t in range(n_new))
        o_ref[0, h] = o / aux[:, n_new:n_new + 1]


class _Rider:
    def __init__(self, name, in_specs, args, out_specs, out_shape, first, second, active):
        self.name, self.in_specs, self.args, self.out_specs, self.out_shape = name, in_specs, args, out_specs, out_shape
        self.first, self.second = first, second
        self.active = active


def _attn_sample_riders(z3, cache_k, cache_v):
    n, n_new, _ = z3.shape
    n_buf = cache_k.shape[1]
    nh, hd = ATT_HEADS, ATT_HEAD_DIM
    assert n_buf >= DILATIONS[-1] * N_BACK and n_new < SAMPLE_ROWS and n_new < LANES

    def per_head(col):
        a = z3[:, :, col * ATT_WIDTH:(col + 1) * ATT_WIDTH].reshape(n, n_new, nh, hd).transpose(0, 2, 1, 3)
        return jnp.pad(a, ((0, 0), (0, 0), (0, SAMPLE_ROWS - n_new), (0, 0)))

    rows_on_lanes = lambda c: c.transpose(0, 2, 3, 1)

    def specs(tiles, steps):
        per_tile = n // tiles
        assert per_tile <= steps and per_tile * tiles == n
        where = lambda i, f: (i * per_tile + jnp.minimum(f, per_tile - 1), 0, 0, 0)
        block = lambda minor: pl.BlockSpec((1, nh) + minor, where)
        active = lambda f: f < per_tile
        return (block((SAMPLE_ROWS, hd)), block((hd, n_buf)), block((SAMPLE_ROWS, n_buf)), block((SAMPLE_ROWS, LANES)),
                active)

    def weights_rider(tiles, steps):
        small, cache, weights, aux, active = specs(tiles, steps)
        return _Rider(
            "attn_weights",
            [small, small, cache, pl.BlockSpec((SAMPLE_ROWS, n_buf), lambda i, f: (0, 0))],
            [per_head(0), per_head(1), rows_on_lanes(cache_k), _sample_multiplicity(n_buf, n_new)],
            [weights, aux],
            [jax.ShapeDtypeStruct((n, nh, SAMPLE_ROWS, n_buf), F32), jax.ShapeDtypeStruct((n, nh, SAMPLE_ROWS, LANES), F32)],
            functools.partial(_attn_scores, n_new=n_new),
            functools.partial(_attn_weights, n_new=n_new),
            active,
        )

    def values_rider(w, aux_w):
        def make(tiles, steps):
            small, cache, weights, aux, active = specs(tiles, steps)
            return _Rider(
                "attn_values",
                [weights, aux, small, cache],
                [w, aux_w, per_head(2), rows_on_lanes(cache_v)],
                [small],
                [jax.ShapeDtypeStruct((n, nh, SAMPLE_ROWS, hd), F32)],
                _attn_value_products,
                functools.partial(_attn_outputs, n_new=n_new),
                active,
            )
        return make

    finish = lambda o: o[:, :, :n_new].transpose(0, 2, 1, 3).reshape(n, n_new, ATT_WIDTH)
    return weights_rider, values_rider, finish


def _gla_rows(q, k, v, li, states, causal, same, group_of_row):
    c = q.shape[0]
    hi, lo = _split_bf16(li)
    tri = causal.astype(BF16)
    b = _dot(tri, hi) + _dot(tri, lo)
    if len(states) == 1:
        b_end = b[c - 1:c, :]
    else:
        full = same.astype(BF16)
        b_end = _dot(full, hi) + _dot(full, lo)
    q_t = (q * (GLA_HEAD_K ** -0.5) * jnp.exp(b)).astype(BF16)
    k_n = (k * jnp.exp(-b)).astype(BF16)
    k_w = k * jnp.exp(b_end - b)
    vb = v.astype(BF16)
    a = jnp.where(causal, _dot_nt(q_t, k_n), 0.0)
    o = _dot(a.astype(BF16), vb)
    ones = jnp.ones((c, LANES), BF16)
    new_states = []
    for gi, s in enumerate(states):
        if len(states) == 1:
            o = o + _dot(q_t, s.astype(BF16))
            kw_g, hi_g, lo_g = k_w.astype(BF16), hi, lo
        else:
            mine = group_of_row == gi
            o = o + jnp.where(mine, _dot(q_t, s.astype(BF16)), 0.0)
            kw_g = jnp.where(mine, k_w, 0.0).astype(BF16)
            hi_g = jnp.where(mine, hi, jnp.zeros_like(hi))
            lo_g = jnp.where(mine, lo, jnp.zeros_like(lo))
        decay = jnp.exp(_dot_tn(hi_g, ones) + _dot_tn(lo_g, ones))
        decay = jnp.concatenate([decay] * (s.shape[1] // LANES), axis=1)
        new_states.append(decay * s + _dot_tn(kw_g, vb))
    return o, new_states


def _gla_finish(o, g_out, rg):
    o = o * lax.rsqrt(jnp.mean(o * o, axis=-1, keepdims=True) + EPS)
    return (o * g_out) * (rg * jax.nn.sigmoid(rg))


GLA_HEADS_PER_STEP = 4
GLA_SEQ_TILE = 1024
GLA_CHUNK_UNROLL = 2


def _gla_prompt_body(q_ref, k_ref, v_ref, rg_ref, la_ref, g_ref, o_ref, s_ref, *, rows, chunk, heads):
    hk, hv = GLA_HEAD_K, GLA_HEAD_V
    r = lax.broadcasted_iota(jnp.int32, (chunk, chunk), 0)
    c = lax.broadcasted_iota(jnp.int32, (chunk, chunk), 1)
    causal = c <= r

    @pl.when(pl.program_id(2) == 0)
    def _():
        s_ref[...] = jnp.zeros_like(s_ref)

    def step(i, carry):
        rr = pl.ds(pl.multiple_of(i * chunk, chunk), chunk)
        for hh in range(heads):
            ks = slice(hh * hk, (hh + 1) * hk)
            vs = slice(hh * hv, (hh + 1) * hv)
            o, (s_new,) = _gla_rows(q_ref[0, rr, ks], k_ref[0, rr, ks], v_ref[0, rr, vs], la_ref[0, rr, ks],
                                    [s_ref[0, hh]], causal, None, None)
            s_ref[0, hh] = s_new
            o_ref[0, rr, vs] = _gla_finish(o, g_ref[:, vs], rg_ref[0, rr, vs]).astype(o_ref.dtype)
        return carry

    lax.fori_loop(0, rows // chunk, step, 0, unroll=GLA_CHUNK_UNROLL)


def _gla_prompt(z3, la3, g_out):
    b, seq, _ = z3.shape
    hk, hv, nh = GLA_HEAD_K, GLA_HEAD_V, GLA_HEADS
    chunk = int(np.gcd(seq, GLA_CHUNK))
    hs = GLA_HEADS_PER_STEP
    rows = min(GLA_SEQ_TILE, seq)
    assert seq % rows == 0 and (rows // chunk) % GLA_CHUNK_UNROLL == 0 and nh % hs == 0
    q0 = 3 * ATT_WIDTH // (hs * hk)
    k0 = q0 + nh // hs
    v0 = (3 * ATT_WIDTH + 2 * GLA_KEY_WIDTH) // (hs * hv)
    r0 = v0 + nh // hs
    return pl.pallas_call(
        functools.partial(_gla_prompt_body, rows=rows, chunk=chunk, heads=hs),
        grid=(b, nh // hs, seq // rows),
        in_specs=[
            pl.BlockSpec((1, rows, hs * hk), lambda i, h, t: (i, t, q0 + h)),
            pl.BlockSpec((1, rows, hs * hk), lambda i, h, t: (i, t, k0 + h)),
            pl.BlockSpec((1, rows, hs * hv), lambda i, h, t: (i, t, v0 + h)),
            pl.BlockSpec((1, rows, hs * hv), lambda i, h, t: (i, t, r0 + h)),
            pl.BlockSpec((1, rows, hs * hk), lambda i, h, t: (i, t, h)),
            pl.BlockSpec((1, hs * hv), lambda i, h, t: (0, h)),
        ],
        out_specs=[
            pl.BlockSpec((1, rows, hs * hv), lambda i, h, t: (i, t, h)),
            pl.BlockSpec((1, hs, hk, hv), lambda i, h, t: (i, h, 0, 0)),
        ],
        out_shape=[
            jax.ShapeDtypeStruct((b, seq, GLA_VALUE_WIDTH), BF16),
            jax.ShapeDtypeStruct((b, nh, hk, hv), F32),
        ],
        compiler_params=_params("parallel", "parallel", "arbitrary"),
        name="gla_prompt",
    )(z3, z3, z3, z3, la3, g_out.reshape(1, -1))


def _gla_sample_body(q_ref, k_ref, v_ref, rg_ref, la_ref, g_ref, s0_ref, o_ref, s_ref, *, n_new, groups):
    rows = n_new * groups
    r = lax.broadcasted_iota(jnp.int32, (rows, rows), 0)
    c = lax.broadcasted_iota(jnp.int32, (rows, rows), 1)
    same = (r // n_new) == (c // n_new)
    causal = same & (c <= r)
    group_of_row = lax.broadcasted_iota(jnp.int32, (rows, 1), 0) // n_new
    o, new_states = _gla_rows(q_ref[...], k_ref[...], v_ref[...], la_ref[...],
                              [s0_ref[gi, 0] for gi in range(groups)], causal, same, group_of_row)
    for gi in range(groups):
        s_ref[gi, 0] = new_states[gi]
    o_ref[...] = _gla_finish(o, g_ref[...], rg_ref[...]).astype(o_ref.dtype)


def _gla_sample(z, la, g_out, state, n_new):
    m = z.shape[0]
    n = m // n_new
    hk, hv, nh = GLA_HEAD_K, GLA_HEAD_V, GLA_HEADS
    groups = min(SAMPLE_GROUP, n)
    rows = groups * n_new
    q0 = 3 * ATT_WIDTH // hk
    k0 = q0 + nh
    v0 = (3 * ATT_WIDTH + 2 * GLA_KEY_WIDTH) // hv
    r0 = v0 + nh
    state_spec = pl.BlockSpec((groups, 1, hk, hv), lambda i, h: (i, h, 0, 0))
    return pl.pallas_call(
        functools.partial(_gla_sample_body, n_new=n_new, groups=groups),
        grid=(n // groups, nh),
        in_specs=[
            pl.BlockSpec((rows, hk), lambda i, h: (i, q0 + h)),
            pl.BlockSpec((rows, hk), lambda i, h: (i, k0 + h)),
            pl.BlockSpec((rows, hv), lambda i, h: (i, v0 + h)),
            pl.BlockSpec((rows, hv), lambda i, h: (i, r0 + h)),
            pl.BlockSpec((rows, hk), lambda i, h: (i, h)),
            pl.BlockSpec((1, hv), lambda i, h: (0, h)),
            state_spec,
        ],
        out_specs=[pl.BlockSpec((rows, hv), lambda i, h: (i, h)), state_spec],
        out_shape=[
            jax.ShapeDtypeStruct((m, GLA_VALUE_WIDTH), BF16),
            jax.ShapeDtypeStruct(state.shape, F32),
        ],
        compiler_params=_params("parallel", "parallel"),
        name="gla_sample",
    )(z, z, z, z, la, g_out.reshape(1, -1), state)


def _out_cross_q_body(x_ref, oa_ref, og_ref, wa_ref, wg_ref, g_ref, wq_ref, x2_ref, qc_ref):
    oa = oa_ref[...].astype(BF16)
    x2 = x_ref[...] + _dot(oa, wa_ref[...]) + _dot(og_ref[...], wg_ref[...])
    x2_ref[...] = x2
    h = _rms(x2, g_ref[...]).astype(BF16)
    qc_ref[...] = (_dot(h, wq_ref[...]) * (MEM_HEAD_DIM ** -0.5)).astype(qc_ref.dtype)


def _out_cross_q(x, o_att, o_gla, w_out_att, w_out_gla, g_cross, w_q, *, tm=512):
    m, d = x.shape
    tm = min(tm, m)
    const = lambda shape: pl.BlockSpec(shape, lambda i: (0, 0))
    return pl.pallas_call(
        _out_cross_q_body,
        grid=(m // tm,),
        in_specs=[
            pl.BlockSpec((tm, d), lambda i: (i, 0)),
            pl.BlockSpec((tm, ATT_WIDTH), lambda i: (i, 0)),
            pl.BlockSpec((tm, GLA_VALUE_WIDTH), lambda i: (i, 0)),
            const(w_out_att.shape),
            const(w_out_gla.shape),
            const((1, d)),
            const(w_q.shape),
        ],
        out_specs=[pl.BlockSpec((tm, d), lambda i: (i, 0)), pl.BlockSpec((tm, MEM_WIDTH), lambda i: (i, 0))],
        out_shape=[jax.ShapeDtypeStruct((m, d), F32), jax.ShapeDtypeStruct((m, MEM_WIDTH), BF16)],
        compiler_params=_params("parallel"),
        name="out_cross_q",
    )(x, o_att, o_gla, w_out_att, w_out_gla, g_cross.reshape(1, d), w_q)


def _cross_heads(q, mk_ref, mv_ref):
    hd = MEM_HEAD_DIM
    heads = []
    for h in range(MEM_HEADS):
        s = _dot_nt(q[:, h * hd:(h + 1) * hd], mk_ref[0, h].astype(BF16))
        p = jnp.exp(s - jnp.max(s, axis=-1, keepdims=True))
        heads.append(_dot(p.astype(BF16), mv_ref[0, h].astype(BF16)) / jnp.sum(p, axis=-1, keepdims=True))
    return jnp.concatenate(heads, axis=1).astype(BF16)


def _cross_body(q_ref, mk_ref, mv_ref, x_ref, w_ref, o_ref, *, groups):
    rows = q_ref.shape[0]
    per_mem = MEM_LEN * MEM_HEADS
    keys = groups * per_mem
    hd = MEM_HEAD_DIM
    k_all = mk_ref[...].reshape(keys, hd).astype(BF16)
    v_all = mv_ref[...].reshape(keys, hd).astype(BF16)
    col = lax.broadcasted_iota(jnp.int32, (1, keys), 1)
    group_of_row = lax.broadcasted_iota(jnp.int32, (rows, 1), 0) // (rows // groups)
    own = (col // per_mem) == group_of_row
    head_of_col = col % MEM_HEADS
    heads = []
    for h in range(MEM_HEADS):
        s = jnp.where(own & (head_of_col == h), _dot_nt(q_ref[:, h * hd:(h + 1) * hd], k_all), -jnp.inf)
        p = jnp.exp(s - jnp.max(s, axis=-1, keepdims=True))
        heads.append(_dot(p.astype(BF16), v_all) / jnp.sum(p, axis=-1, keepdims=True))
    o = jnp.concatenate(heads, axis=1).astype(BF16)
    o_ref[...] = x_ref[...] + _dot(o, w_ref[...])


def _post_mix_body(x_ref, oa_ref, og_ref, wa_ref, wg_ref, g_ref, wq_ref, mk_ref, mv_ref, wo_ref, o_ref):
    x2 = x_ref[...] + _dot(oa_ref[...].astype(BF16), wa_ref[...]) + _dot(og_ref[...], wg_ref[...])
    h = _rms(x2, g_ref[...]).astype(BF16)
    q = (_dot(h, wq_ref[...]) * (MEM_HEAD_DIM ** -0.5)).astype(BF16)
    o_ref[...] = x2 + _dot(_cross_heads(q, mk_ref, mv_ref), wo_ref[...])


def _post_mix(x, o_att, o_gla, w_out_att, w_out_gla, g_cross, w_q, mem_k, mem_v, w_o, *, seq, tm=512):
    m, d = x.shape
    tm = min(tm, seq)
    assert seq % tm == 0
    const = lambda a: pl.BlockSpec(a.shape, lambda i: (0,) * a.ndim, pipeline_mode=pl.Buffered(1))
    rows = lambda width: pl.BlockSpec((tm, width), lambda i: (i, 0))
    mem = pl.BlockSpec((1, MEM_HEADS, MEM_LEN, MEM_HEAD_DIM), lambda i: (i // (seq // tm), 0, 0, 0))
    g = g_cross.reshape(1, d)
    return pl.pallas_call(
        _post_mix_body,
        grid=(m // tm,),
        in_specs=[rows(d), rows(ATT_WIDTH), rows(GLA_VALUE_WIDTH), const(w_out_att), const(w_out_gla), const(g),
                  const(w_q), mem, mem, const(w_o)],
        out_specs=rows(d),
        out_shape=jax.ShapeDtypeStruct((m, d), F32),
        compiler_params=_params("parallel"),
        name="post_mix",
    )(x, o_att, o_gla, w_out_att, w_out_gla, g, w_q, mem_k, mem_v, w_o)


def _cross(qc, mem_k, mem_v, x, w_o, *, rows, groups):
    m, d = x.shape
    mem_k, mem_v = (a.reshape(a.shape[0], MEM_LEN * MEM_HEADS, MEM_HEAD_DIM) for a in (mem_k, mem_v))
    mem = pl.BlockSpec((groups, MEM_LEN * MEM_HEADS, MEM_HEAD_DIM), lambda i: (i, 0, 0))
    return pl.pallas_call(
        functools.partial(_cross_body, groups=groups),
        grid=(m // rows,),
        in_specs=[
            pl.BlockSpec((rows, MEM_WIDTH), lambda i: (i, 0)),
            mem,
            mem,
            pl.BlockSpec((rows, d), lambda i: (i, 0)),
            pl.BlockSpec(w_o.shape, lambda i: (0, 0)),
        ],
        out_specs=pl.BlockSpec((rows, d), lambda i: (i, 0)),
        out_shape=jax.ShapeDtypeStruct((m, d), F32),
        compiler_params=_params("parallel"),
        name="cross",
    )(qc, mem_k, mem_v, x, w_o)


def kernel(x_prompt, x_sample, mem_prompt, cache_win_k, cache_win_v, state_gla, cache_mem_k, cache_mem_v, g_ffn1, w1_gate, w1_up, w1_down, g_mix, w_in, w_gla_a2, b_gla_a, g_gla_out, w_out, g_mem, w_mem_k, w_mem_v, g_cross, w_cross_q, w_cross_o, g_ffn2, w2_gate, w2_up, w2_down, g_final):
    b, seq, d = x_prompt.shape
    n, n_new, _ = x_sample.shape
    depth = g_ffn1.shape[0]
    past_len = cache_win_k.shape[2]
    n_keep = min(MAX_WINDOW, seq)
    bf = lambda a: a.astype(BF16)

    xp = x_prompt.reshape(b * seq, d)
    xs = x_sample.reshape(n * n_new, d)
    rope_p = _rope_tables(jnp.arange(seq))
    rope_s = _rope_tables(jnp.tile(past_len + jnp.arange(n_new), n))
    outs = [[] for _ in range(8)]
    cols = np.cumsum([0, ATT_WIDTH, ATT_WIDTH, ATT_WIDTH, GLA_KEY_WIDTH, GLA_KEY_WIDTH, GLA_VALUE_WIDTH, GLA_GATE_RANK])
    for l in range(depth):
        w_main = bf(jnp.concatenate([w_in[l][:, :cols[6]], w_in[l][:, cols[7]:]], axis=1))
        w_lr = bf(jnp.pad(w_in[l][:, cols[6]:cols[7]], ((0, 0), (0, LANES - GLA_GATE_RANK))))
        w_a2 = bf(jnp.pad(w_gla_a2[l], ((0, LANES - GLA_GATE_RANK), (0, 0))))
        w_out_att, w_out_gla = bf(w_out[l][:ATT_WIDTH]), bf(w_out[l][ATT_WIDTH:])
        w_mem = bf(jnp.concatenate([w_mem_k[l], w_mem_v[l]], axis=1))
        w_cq, w_co = bf(w_cross_q[l]), bf(w_cross_o[l])
        gf = g_final if l == depth - 1 else None
        mix_in = lambda x, rope: _mix_in(x, g_mix[l], w_main, w_lr, w_a2, b_gla_a[l], rope)

        w2 = (bf(w2_gate[l]), bf(w2_up[l]), bf(w2_down[l]))
        xs, *w1 = _ffn(xs, g_ffn1[l], w1_gate[l], w1_up[l], w1_down[l])
        zs, la_s = mix_in(xs, rope_s)
        weights_rider, values_rider, attn_finish = _attn_sample_riders(
            zs.reshape(n, n_new, Z_COLS), cache_win_k[l], cache_win_v[l])

        xp, att_w, att_aux = _ffn(xp, g_ffn1[l], *w1, rider=weights_rider)
        mem_kv = _norm_proj(mem_prompt.reshape(b * MEM_LEN, d), g_mem[l], w_mem).reshape(b, MEM_LEN, 2 * MEM_WIDTH)
        mem_k = mem_kv[:, :, :MEM_WIDTH].reshape(b, MEM_LEN, MEM_HEADS, MEM_HEAD_DIM)
        mem_v = mem_kv[:, :, MEM_WIDTH:].reshape(b, MEM_LEN, MEM_HEADS, MEM_HEAD_DIM)
        zp, la_p = mix_in(xp, rope_p)
        zp3 = zp.reshape(b, seq, Z_COLS)
        o_att_p = _attn_prompt(zp3).reshape(b * seq, ATT_WIDTH)
        o_gla_p, st_p = _gla_prompt(zp3, la_p.reshape(b, seq, GLA_KEY_WIDTH), g_gla_out[l])
        xp = _post_mix(xp, o_att_p, o_gla_p.reshape(b * seq, GLA_VALUE_WIDTH), w_out_att, w_out_gla, g_cross[l], w_cq,
                       mem_k.transpose(0, 2, 1, 3), mem_v.transpose(0, 2, 1, 3), w_co, seq=seq)
        xp, o_att_s = _ffn(xp, g_ffn2[l], *w2, g_final=gf, rider=values_rider(att_w, att_aux))

        o_att_s = attn_finish(o_att_s).reshape(n * n_new, ATT_WIDTH)
        o_gla_s, st_s = _gla_sample(zs, la_s, g_gla_out[l], state_gla[l], n_new)
        xs, qc = _out_cross_q(xs, o_att_s, o_gla_s, w_out_att, w_out_gla, g_cross[l], w_cq)
        groups = min(SAMPLE_GROUP, n)
        xs = _cross(qc, cache_mem_k[l], cache_mem_v[l], xs, w_co, rows=groups * n_new, groups=groups)
        xs = _ffn(xs, g_ffn2[l], *w2, g_final=gf)

        heads = lambda a, t: a.reshape(-1, t, ATT_HEADS, ATT_HEAD_DIM)
        outs[0].append(heads(zp3[:, seq - n_keep:, ATT_WIDTH:2 * ATT_WIDTH], n_keep))
        outs[1].append(heads(zp3[:, seq - n_keep:, 2 * ATT_WIDTH:3 * ATT_WIDTH], n_keep))
        outs[2].append(st_p)
        outs[3].append(mem_k)
        outs[4].append(mem_v)
        outs[5].append(heads(zs[:, ATT_WIDTH:2 * ATT_WIDTH], n_new))
        outs[6].append(heads(zs[:, 2 * ATT_WIDTH:3 * ATT_WIDTH], n_new))
        outs[7].append(st_s)

    return (xp.reshape(b, seq, d), xs.reshape(n, n_new, d)) + tuple(jnp.stack(o) for o in outs)
```

```python
import functools

import jax
import jax.numpy as jnp
import numpy as np
from jax import lax
from jax.experimental import pallas as pl
from jax.experimental.pallas import tpu as pltpu

F32 = jnp.float32
BF16 = jnp.bfloat16

EPS = 1e-6
ATT_HEADS = 16
ATT_HEAD_DIM = 64
ATT_WIDTH = ATT_HEADS * ATT_HEAD_DIM
DILATIONS = (1, 4, 16)
N_BACK = 128
BAND = 128
ROT_DIM = 16
ROPE_THETA = 500000.0
GLA_HEADS = 4
GLA_HEAD_K = 128
GLA_HEAD_V = 256
GLA_KEY_WIDTH = GLA_HEADS * GLA_HEAD_K
GLA_VALUE_WIDTH = GLA_HEADS * GLA_HEAD_V
GLA_GATE_RANK = 16
GLA_GATE_TAU = 16.0
GLA_CHUNK = 64
MEM_LEN = 256
MEM_HEADS = 4
MEM_HEAD_DIM = 128
MEM_WIDTH = MEM_HEADS * MEM_HEAD_DIM
MAX_WINDOW = 2048
LANES = 128
SAMPLE_GROUP = 8
Z_COLS = 3 * ATT_WIDTH + 2 * GLA_KEY_WIDTH + 2 * GLA_VALUE_WIDTH
Z_TILE = 1024
VMEM_LIMIT_BYTES = 56 * 1024 * 1024


def _params(*semantics, vmem_limit_bytes=VMEM_LIMIT_BYTES):
    return pltpu.CompilerParams(dimension_semantics=semantics, vmem_limit_bytes=vmem_limit_bytes)


def _rms(x, g):
    return x * lax.rsqrt(jnp.mean(x * x, axis=-1, keepdims=True) + EPS) * g


def _dot(a, b):
    return jnp.dot(a, b, preferred_element_type=F32)


def _dot_nt(a, b):
    return lax.dot_general(a, b, (((1,), (1,)), ((), ())), preferred_element_type=F32)


def _dot_tn(a, b):
    return lax.dot_general(a, b, (((0,), (0,)), ((), ())), preferred_element_type=F32)


def _split_bf16(x):
    hi = x.astype(BF16)
    lo = (x - hi.astype(F32)).astype(BF16)
    return hi, lo


RIDER_HOST_ROWS = 1024
RIDER_VMEM_LIMIT_BYTES = 62 * 1024 * 1024


def _ffn_body(*refs, nf, final, emit, rider):
    x_ref, g_ref, wg_ref, wu_ref, wd_ref = refs[:5]
    refs = refs[5:]
    if final:
        gf_ref, refs = refs[0], refs[1:]
    if rider is not None:
        rider_in, refs = refs[:len(rider.args)], refs[len(rider.args):]
        rider_out = refs[-1 - len(rider.out_specs):-1]
    o_ref, h_ref = refs[0], refs[-1]
    f = pl.program_id(1)

    @pl.when(f == 0)
    def _():
        h_ref[...] = _rms(x_ref[...], g_ref[...]).astype(BF16)
        o_ref[...] = jnp.zeros_like(o_ref)

    def step(with_rider):
        wg, wu, wd = wg_ref[...].astype(BF16), wu_ref[...].astype(BF16), wd_ref[...].astype(BF16)
        if emit:
            for w_out_ref, w in zip(refs[1:4], (wg, wu, wd)):
                w_out_ref[...] = w
        h = h_ref[...]
        gate = _dot(h, wg)
        if with_rider:
            carried = rider.first(*rider_in)
        up = _dot(h, wu)
        if with_rider:
            rider.second(*rider_in, *rider_out, carried)
        act = (gate * jax.nn.sigmoid(gate) * up).astype(BF16)
        o_ref[...] += _dot(act, wd)

    if rider is None:
        step(False)
    else:
        active = rider.active(f)
        pl.when(active)(functools.partial(step, True))
        pl.when(jnp.logical_not(active))(functools.partial(step, False))

    @pl.when(f == nf - 1)
    def _():
        y = x_ref[...] + 0.5 * o_ref[...]
        if final:
            y = _rms(y, gf_ref[...])
        o_ref[...] = y


def _ffn(x, g, wg, wu, wd, g_final=None, *, tm=512, tf=None, rider=None):
    m, d = x.shape
    dff = wg.shape[1]
    if rider is not None:
        tm = RIDER_HOST_ROWS
    tm = min(tm, m)
    emit = wg.dtype != BF16
    if tf is None:
        tf = 256 if emit or rider is not None else 512
    nf = dff // tf
    assert dff % tf == 0 and m % tm == 0
    assert not emit or m == tm
    final = g_final is not None
    row = pl.BlockSpec((1, d), lambda i, f: (0, 0))
    w_specs = [
        pl.BlockSpec((d, tf), lambda i, f: (0, f)),
        pl.BlockSpec((d, tf), lambda i, f: (0, f)),
        pl.BlockSpec((tf, d), lambda i, f: (f, 0)),
    ]
    x_mode = {} if rider is None else dict(pipeline_mode=pl.Buffered(1))
    in_specs = [pl.BlockSpec((tm, d), lambda i, f: (i, 0), **x_mode), row] + w_specs
    args = [x, g.reshape(1, d), wg, wu, wd]
    if final:
        in_specs.append(row)
        args.append(g_final.reshape(1, d))
    out_specs = [pl.BlockSpec((tm, d), lambda i, f: (i, 0))]
    out_shape = [jax.ShapeDtypeStruct((m, d), F32)]
    if emit:
        out_specs += w_specs
        out_shape += [jax.ShapeDtypeStruct(w.shape, BF16) for w in (wg, wu, wd)]
    if rider is not None:
        rider = rider(m // tm, nf)
        in_specs += rider.in_specs
        args += rider.args
        out_specs += rider.out_specs
        out_shape += rider.out_shape
    out = pl.pallas_call(
        functools.partial(_ffn_body, nf=nf, final=final, emit=emit, rider=rider),
        grid=(m // tm, nf),
        in_specs=in_specs,
        out_specs=out_specs,
        out_shape=out_shape,
        scratch_shapes=[pltpu.VMEM((tm, d), BF16)],
        compiler_params=(_params("parallel", "arbitrary") if rider is None else
                         _params("arbitrary", "arbitrary", vmem_limit_bytes=RIDER_VMEM_LIMIT_BYTES)),
        name=("ffn_final" if final else "ffn") + ("_f32w" if emit else "") + ("" if rider is None else "_" + rider.name),
    )(*args)
    return out if len(out) > 1 else out[0]


def _mix_in_body(x_ref, g_ref, w_ref, wlr_ref, wa2_ref, ba_ref, rope_ref, z_ref, la_ref, h_ref):
    n = pl.program_id(1)

    @pl.when(n == 0)
    def _():
        h = _rms(x_ref[...], g_ref[...]).astype(BF16)
        h_ref[...] = h
        g_lr = _dot(h, wlr_ref[...])
        pre = _dot(g_lr.astype(BF16), wa2_ref[...]) + ba_ref[...]
        la_ref[...] = (jnp.minimum(pre, 0.0) - jnp.log1p(jnp.exp(-jnp.abs(pre)))) * (1.0 / GLA_GATE_TAU)

    z = _dot(h_ref[...], w_ref[...])

    @pl.when(n >= 2)
    def _():
        z_ref[...] = z

    @pl.when(n < 2)
    def _():
        cos, sin_up, sin_dn = rope_ref[0], rope_ref[1], rope_ref[2]
        scale = jnp.where(n == 0, ATT_HEAD_DIM ** -0.5, 1.0).astype(F32)
        half = ROT_DIM // 2
        for j in range(Z_TILE // LANES):
            zc = z[:, j * LANES:(j + 1) * LANES]
            rot = zc * cos + pltpu.roll(zc, half, axis=1) * sin_up + pltpu.roll(zc, LANES - half, axis=1) * sin_dn
            z_ref[:, j * LANES:(j + 1) * LANES] = rot * scale


def _mix_in(x, g, w_main, w_lr, w_a2, b_a, rope, *, tm=512):
    m, d = x.shape
    tm = min(tm, m)
    n_rope = rope.shape[1] // tm
    return pl.pallas_call(
        _mix_in_body,
        grid=(m // tm, Z_COLS // Z_TILE),
        in_specs=[
            pl.BlockSpec((tm, d), lambda i, n: (i, 0)),
            pl.BlockSpec((1, d), lambda i, n: (0, 0)),
            pl.BlockSpec((d, Z_TILE), lambda i, n: (0, n)),
            pl.BlockSpec((d, LANES), lambda i, n: (0, 0)),
            pl.BlockSpec((LANES, GLA_KEY_WIDTH), lambda i, n: (0, 0)),
            pl.BlockSpec((1, GLA_KEY_WIDTH), lambda i, n: (0, 0)),
            pl.BlockSpec((3, tm, LANES), lambda i, n: (0, i % n_rope, 0)),
        ],
        out_specs=[
            pl.BlockSpec((tm, Z_TILE), lambda i, n: (i, n)),
            pl.BlockSpec((tm, GLA_KEY_WIDTH), lambda i, n: (i, 0)),
        ],
        out_shape=[
            jax.ShapeDtypeStruct((m, Z_COLS), F32),
            jax.ShapeDtypeStruct((m, GLA_KEY_WIDTH), F32),
        ],
        scratch_shapes=[pltpu.VMEM((tm, d), BF16)],
        compiler_params=_params("parallel", "arbitrary"),
        name="mix_in",
    )(x, g.reshape(1, d), w_main, w_lr, w_a2, b_a.reshape(1, -1), rope)


def _rope_tables(pos):
    half = ROT_DIM // 2
    inv = ROPE_THETA ** (-jnp.arange(half, dtype=F32) * 2.0 / ROT_DIM)
    ang = pos.astype(F32)[:, None] * inv[None, :]
    cos, sin = jnp.cos(ang), jnp.sin(ang)
    t = pos.shape[0]
    zeros = jnp.zeros((t, half), F32)
    rest0 = jnp.zeros((t, ATT_HEAD_DIM - ROT_DIM), F32)
    c = jnp.concatenate([cos, cos, jnp.ones_like(rest0)], axis=1)
    s_up = jnp.concatenate([zeros, sin, rest0], axis=1)
    s_dn = jnp.concatenate([-sin, zeros, rest0], axis=1)
    reps = LANES // ATT_HEAD_DIM
    return jnp.stack([jnp.tile(c, (1, reps)), jnp.tile(s_up, (1, reps)), jnp.tile(s_dn, (1, reps))])


def _norm_proj_body(x_ref, g_ref, w_ref, o_ref):
    h = _rms(x_ref[...], g_ref[...]).astype(BF16)
    o_ref[...] = _dot(h, w_ref[...])


def _norm_proj(x, g, w, *, tm=512):
    m, d = x.shape
    n = w.shape[1]
    tm = min(tm, m)
    return pl.pallas_call(
        _norm_proj_body,
        grid=(m // tm,),
        in_specs=[
            pl.BlockSpec((tm, d), lambda i: (i, 0)),
            pl.BlockSpec((1, d), lambda i: (0, 0)),
            pl.BlockSpec((d, n), lambda i: (0, 0)),
        ],
        out_specs=pl.BlockSpec((tm, n), lambda i: (i, 0)),
        out_shape=jax.ShapeDtypeStruct((m, n), F32),
        compiler_params=_params("parallel"),
        name="memory_kv",
    )(x, g.reshape(1, d), w)


ATTN_UNROLL = 16
MERGE_ROWS = 256


def _band_bias():
    i = np.arange(BAND)[:, None]
    j = np.arange(2 * BAND)[None, :]
    out = []
    for off in (0, BAND):
        dist = off + i - j
        out.append(np.where((dist >= 0) & (dist <= N_BACK), 0.0, -np.inf))
    return jnp.asarray(np.stack(out), F32)


def _attn_prompt_body(q_ref, k_ref, v_ref, bias_ref, o_ref, acc_ref, m_ref, l_ref, *, seq):
    hd = ATT_HEAD_DIM
    assert LANES == 2 * hd
    blocks = seq // BAND
    first_head = lax.broadcasted_iota(jnp.int32, (1, LANES), 1) < hd
    for ci, dil in enumerate(DILATIONS):
        nb = blocks // dil
        shift = nb.bit_length() - 1
        assert nb == 1 << shift

        def step(it, carry, ci=ci, dil=dil, nb=nb, shift=shift):
            loaded = []
            for u in range(ATTN_UNROLL):
                w = it * ATTN_UNROLL + u
                r = lax.shift_right_logical(w, shift)
                n = w & (nb - 1)
                base = jnp.maximum(n - 1, 0) * BAND
                if dil == 1:
                    q_rows = pl.ds(pl.multiple_of(n * BAND, BAND), BAND)
                    k_rows = pl.ds(pl.multiple_of(base, BAND), 2 * BAND)
                else:
                    q_rows = pl.ds(r + dil * BAND * n, BAND, stride=dil)
                    k_rows = pl.ds(r + dil * base, 2 * BAND, stride=dil)
                bias = bias_ref[jnp.minimum(n, 1)]
                loaded.append((q_rows, q_ref[0, q_rows, :], k_ref[0, k_rows, :], v_ref[0, k_rows, :], bias))
            results = []
            for q_rows, q, k, v, bias in loaded:
                kb = k.astype(BF16)
                qs = (jnp.where(first_head, q, 0.0), jnp.where(first_head, 0.0, q))
                vs = (jnp.where(first_head, v, 1.0), jnp.where(first_head, 1.0, v))
                out, mx = [], []
                for hh in range(LANES // hd):
                    s = _dot_nt(qs[hh].astype(BF16), kb) + bias
                    m_c = jnp.max(s, axis=1, keepdims=True)
                    p = jnp.exp(s - m_c)
                    out.append(_dot(p.astype(BF16), vs[hh].astype(BF16)))
                    mx.append(m_c)
                results.append((q_rows, jnp.where(first_head, out[0], out[1]),
                                jnp.where(first_head, mx[0], mx[1]), jnp.where(first_head, out[1], out[0])))
            for q_rows, pv, mx, den_swapped in results:
                acc_ref[ci, q_rows, :] = pv
                m_ref[ci, q_rows, :] = mx
                l_ref[ci, q_rows, :] = den_swapped
            return carry

        lax.fori_loop(0, blocks // ATTN_UNROLL, step, 0)

    def merge(i, carry):
        rows = pl.ds(pl.multiple_of(i * MERGE_ROWS, MERGE_ROWS), MERGE_ROWS)
        ms = [m_ref[ci, rows, :] for ci in range(len(DILATIONS))]
        m_all = functools.reduce(jnp.maximum, ms)
        ws = [jnp.exp(m - m_all) for m in ms]
        num = sum(w * acc_ref[ci, rows, :] for ci, w in enumerate(ws))
        den = sum(w * pltpu.roll(l_ref[ci, rows, :], hd, axis=1) for ci, w in enumerate(ws))
        o_ref[0, rows, :] = (num / den).astype(o_ref.dtype)
        return carry

    lax.fori_loop(0, seq // MERGE_ROWS, merge, 0)


def _attn_prompt(z3):
    b, seq, _ = z3.shape
    pairs = ATT_WIDTH // LANES
    assert seq % (DILATIONS[-1] * BAND) == 0 and seq // DILATIONS[-1] >= 2 * BAND
    assert (seq // BAND) % ATTN_UNROLL == 0 and seq % MERGE_ROWS == 0
    blk = lambda off: pl.BlockSpec((1, seq, LANES), lambda i, p: (i, 0, off + p))
    return pl.pallas_call(
        functools.partial(_attn_prompt_body, seq=seq),
        grid=(b, pairs),
        in_specs=[blk(0), blk(pairs), blk(2 * pairs),
                  pl.BlockSpec((2, BAND, 2 * BAND), lambda i, p: (0, 0, 0))],
        out_specs=pl.BlockSpec((1, seq, LANES), lambda i, p: (i, 0, p)),
        out_shape=jax.ShapeDtypeStruct((b, seq, ATT_WIDTH), BF16),
        scratch_shapes=[pltpu.VMEM((len(DILATIONS), seq, LANES), F32)] * 3,
        compiler_params=_params("parallel", "parallel"),
        name="attn_prompt",
    )(z3, z3, z3, _band_bias())


SAMPLE_ROWS = 8


def _sample_multiplicity(n_buf, n_new):
    t = (np.arange(SAMPLE_ROWS) % n_new)[:, None]
    r = np.arange(n_buf)[None, :]
    mult = np.zeros((SAMPLE_ROWS, n_buf), np.float32)
    for dil in DILATIONS:
        back = n_buf + t - r
        mult += (back % dil == 0) & (back >= dil) & (back <= dil * N_BACK)
    return jnp.asarray(mult, F32)


def _new_row_multiplicity(n_new):
    t_row = lax.broadcasted_iota(jnp.int32, (SAMPLE_ROWS, 1), 0) % n_new
    return [(t_row >= t).astype(F32) + (len(DILATIONS) - 1.0) * (t_row == t).astype(F32) for t in range(n_new)]


def _attn_scores(q_ref, kn_ref, kt_ref, mult_ref, *, n_new):
    live = mult_ref[...] > 0.0
    mult_new = _new_row_multiplicity(n_new)
    scores = []
    for h in range(ATT_HEADS):
        q = q_ref[0, h]
        kn = kn_ref[0, h]
        s = jnp.where(live, _dot(q.astype(BF16), kt_ref[0, h].astype(BF16)), -jnp.inf)
        s_new = [jnp.where(mult_new[t] > 0.0, jnp.sum(q * kn[t:t + 1], axis=1, keepdims=True), -jnp.inf)
                 for t in range(n_new)]
        scores.append((s, s_new))
    return scores


def _attn_weights(q_ref, kn_ref, kt_ref, mult_ref, w_ref, aux_ref, scores, *, n_new):
    mult = mult_ref[...]
    mult_new = _new_row_multiplicity(n_new)
    lane = lax.broadcasted_iota(jnp.int32, (1, LANES), 1)
    for h, (s, s_new) in enumerate(scores):
        m = functools.reduce(jnp.maximum, s_new, jnp.max(s, axis=1, keepdims=True))
        w = mult * jnp.exp(s - m)
        w_new = [mult_new[t] * jnp.exp(s_new[t] - m) for t in range(n_new)]
        den = jnp.sum(w, axis=1, keepdims=True) + sum(w_new)
        w_ref[0, h] = w
        aux_ref[0, h] = sum(jnp.where(lane == t, w_new[t], 0.0) for t in range(n_new)) + jnp.where(lane == n_new, den, 0.0)


def _attn_value_products(w_ref, aux_ref, vn_ref, vt_ref):
    return [_dot_nt(w_ref[0, h].astype(BF16), vt_ref[0, h].astype(BF16)) for h in range(ATT_HEADS)]


def _attn_outputs(w_ref, aux_ref, vn_ref, vt_ref, o_ref, products, *, n_new):
    for h, o in enumerate(products):
        aux = aux_ref[0, h]
        vn = vn_ref[0, h]
        o = o + sum(aux[:, t:t + 1] * vn[t:t + 1] for t in range(n_new))
        o_ref[0, h] = o / aux[:, n_new:n_new + 1]


class _Rider:
    def __init__(self, name, in_specs, args, out_specs, out_shape, first, second, active):
        self.name, self.in_specs, self.args, self.out_specs, self.out_shape = name, in_specs, args, out_specs, out_shape
        self.first, self.second = first, second
        self.active = active


def _attn_sample_riders(z3, cache_k, cache_v):
    n, n_new, _ = z3.shape
    n_buf = cache_k.shape[1]
    nh, hd = ATT_HEADS, ATT_HEAD_DIM
    assert n_buf >= DILATIONS[-1] * N_BACK and n_new < SAMPLE_ROWS and n_new < LANES

    def per_head(col):
        a = z3[:, :, col * ATT_WIDTH:(col + 1) * ATT_WIDTH].reshape(n, n_new, nh, hd).transpose(0, 2, 1, 3)
        return jnp.pad(a, ((0, 0), (0, 0), (0, SAMPLE_ROWS - n_new), (0, 0)))

    rows_on_lanes = lambda c: c.transpose(0, 2, 3, 1)

    def specs(tiles, steps):
        per_tile = n // tiles
        assert per_tile <= steps and per_tile * tiles == n
        where = lambda i, f: (i * per_tile + jnp.minimum(f, per_tile - 1), 0, 0, 0)
        block = lambda minor: pl.BlockSpec((1, nh) + minor, where)
        active = lambda f: f < per_tile
        return (block((SAMPLE_ROWS, hd)), block((hd, n_buf)), block((SAMPLE_ROWS, n_buf)), block((SAMPLE_ROWS, LANES)),
                active)

    def weights_rider(tiles, steps):
        small, cache, weights, aux, active = specs(tiles, steps)
        return _Rider(
            "attn_weights",
            [small, small, cache, pl.BlockSpec((SAMPLE_ROWS, n_buf), lambda i, f: (0, 0))],
            [per_head(0), per_head(1), rows_on_lanes(cache_k), _sample_multiplicity(n_buf, n_new)],
            [weights, aux],
            [jax.ShapeDtypeStruct((n, nh, SAMPLE_ROWS, n_buf), F32), jax.ShapeDtypeStruct((n, nh, SAMPLE_ROWS, LANES), F32)],
            functools.partial(_attn_scores, n_new=n_new),
            functools.partial(_attn_weights, n_new=n_new),
            active,
        )

    def values_rider(w, aux_w):
        def make(tiles, steps):
            small, cache, weights, aux, active = specs(tiles, steps)
            return _Rider(
                "attn_values",
                [weights, aux, small, cache],
                [w, aux_w, per_head(2), rows_on_lanes(cache_v)],
                [small],
                [jax.ShapeDtypeStruct((n, nh, SAMPLE_ROWS, hd), F32)],
                _attn_value_products,
                functools.partial(_attn_outputs, n_new=n_new),
                active,
            )
        return make

    finish = lambda o: o[:, :, :n_new].transpose(0, 2, 1, 3).reshape(n, n_new, ATT_WIDTH)
    return weights_rider, values_rider, finish


def _gla_rows(q, k, v, li, states, causal, same, group_of_row):
    c = q.shape[0]
    hi, lo = _split_bf16(li)
    tri = causal.astype(BF16)
    b = _dot(tri, hi) + _dot(tri, lo)
    if len(states) == 1:
        b_end = b[c - 1:c, :]
    else:
        full = same.astype(BF16)
        b_end = _dot(full, hi) + _dot(full, lo)
    q_t = (q * (GLA_HEAD_K ** -0.5) * jnp.exp(b)).astype(BF16)
    k_n = (k * jnp.exp(-b)).astype(BF16)
    k_w = k * jnp.exp(b_end - b)
    vb = v.astype(BF16)
    a = jnp.where(causal, _dot_nt(q_t, k_n), 0.0)
    o = _dot(a.astype(BF16), vb)
    ones = jnp.ones((c, LANES), BF16)
    new_states = []
    for gi, s in enumerate(states):
        if len(states) == 1:
            o = o + _dot(q_t, s.astype(BF16))
            kw_g, hi_g, lo_g = k_w.astype(BF16), hi, lo
        else:
            mine = group_of_row == gi
            o = o + jnp.where(mine, _dot(q_t, s.astype(BF16)), 0.0)
            kw_g = jnp.where(mine, k_w, 0.0).astype(BF16)
            hi_g = jnp.where(mine, hi, jnp.zeros_like(hi))
            lo_g = jnp.where(mine, lo, jnp.zeros_like(lo))
        decay = jnp.exp(_dot_tn(hi_g, ones) + _dot_tn(lo_g, ones))
        decay = jnp.concatenate([decay] * (s.shape[1] // LANES), axis=1)
        new_states.append(decay * s + _dot_tn(kw_g, vb))
    return o, new_states


def _gla_finish(o, g_out, rg):
    o = o * lax.rsqrt(jnp.mean(o * o, axis=-1, keepdims=True) + EPS)
    return (o * g_out) * (rg * jax.nn.sigmoid(rg))


GLA_HEADS_PER_STEP = 4
GLA_SEQ_TILE = 1024
GLA_CHUNK_UNROLL = 4


def _gla_prompt_body(q_ref, k_ref, v_ref, rg_ref, la_ref, g_ref, o_ref, s_ref, *, rows, chunk, heads):
    hk, hv = GLA_HEAD_K, GLA_HEAD_V
    r = lax.broadcasted_iota(jnp.int32, (chunk, chunk), 0)
    c = lax.broadcasted_iota(jnp.int32, (chunk, chunk), 1)
    causal = c <= r

    @pl.when(pl.program_id(2) == 0)
    def _():
        s_ref[...] = jnp.zeros_like(s_ref)

    def step(i, carry):
        rr = pl.ds(pl.multiple_of(i * chunk, chunk), chunk)
        for hh in range(heads):
            ks = slice(hh * hk, (hh + 1) * hk)
            vs = slice(hh * hv, (hh + 1) * hv)
            o, (s_new,) = _gla_rows(q_ref[0, rr, ks], k_ref[0, rr, ks], v_ref[0, rr, vs], la_ref[0, rr, ks],
                                    [s_ref[0, hh]], causal, None, None)
            s_ref[0, hh] = s_new
            o_ref[0, rr, vs] = _gla_finish(o, g_ref[:, vs], rg_ref[0, rr, vs]).astype(o_ref.dtype)
        return carry

    lax.fori_loop(0, rows // chunk, step, 0, unroll=GLA_CHUNK_UNROLL)


def _gla_prompt(z3, la3, g_out):
    b, seq, _ = z3.shape
    hk, hv, nh = GLA_HEAD_K, GLA_HEAD_V, GLA_HEADS
    chunk = int(np.gcd(seq, GLA_CHUNK))
    hs = GLA_HEADS_PER_STEP
    rows = min(GLA_SEQ_TILE, seq)
    assert seq % rows == 0 and (rows // chunk) % GLA_CHUNK_UNROLL == 0 and nh % hs == 0
    q0 = 3 * ATT_WIDTH // (hs * hk)
    k0 = q0 + nh // hs
    v0 = (3 * ATT_WIDTH + 2 * GLA_KEY_WIDTH) // (hs * hv)
    r0 = v0 + nh // hs
    return pl.pallas_call(
        functools.partial(_gla_prompt_body, rows=rows, chunk=chunk, heads=hs),
        grid=(b, nh // hs, seq // rows),
        in_specs=[
            pl.BlockSpec((1, rows, hs * hk), lambda i, h, t: (i, t, q0 + h)),
            pl.BlockSpec((1, rows, hs * hk), lambda i, h, t: (i, t, k0 + h)),
            pl.BlockSpec((1, rows, hs * hv), lambda i, h, t: (i, t, v0 + h)),
            pl.BlockSpec((1, rows, hs * hv), lambda i, h, t: (i, t, r0 + h)),
            pl.BlockSpec((1, rows, hs * hk), lambda i, h, t: (i, t, h)),
            pl.BlockSpec((1, hs * hv), lambda i, h, t: (0, h)),
        ],
        out_specs=[
            pl.BlockSpec((1, rows, hs * hv), lambda i, h, t: (i, t, h)),
            pl.BlockSpec((1, hs, hk, hv), lambda i, h, t: (i, h, 0, 0)),
        ],
        out_shape=[
            jax.ShapeDtypeStruct((b, seq, GLA_VALUE_WIDTH), BF16),
            jax.ShapeDtypeStruct((b, nh, hk, hv), F32),
        ],
        compiler_params=_params("parallel", "parallel", "arbitrary"),
        name="gla_prompt",
    )(z3, z3, z3, z3, la3, g_out.reshape(1, -1))


def _gla_sample_body(q_ref, k_ref, v_ref, rg_ref, la_ref, g_ref, s0_ref, o_ref, s_ref, *, n_new, groups):
    rows = n_new * groups
    r = lax.broadcasted_iota(jnp.int32, (rows, rows), 0)
    c = lax.broadcasted_iota(jnp.int32, (rows, rows), 1)
    same = (r // n_new) == (c // n_new)
    causal = same & (c <= r)
    group_of_row = lax.broadcasted_iota(jnp.int32, (rows, 1), 0) // n_new
    o, new_states = _gla_rows(q_ref[...], k_ref[...], v_ref[...], la_ref[...],
                              [s0_ref[gi, 0] for gi in range(groups)], causal, same, group_of_row)
    for gi in range(groups):
        s_ref[gi, 0] = new_states[gi]
    o_ref[...] = _gla_finish(o, g_ref[...], rg_ref[...]).astype(o_ref.dtype)


def _gla_sample(z, la, g_out, state, n_new):
    m = z.shape[0]
    n = m // n_new
    hk, hv, nh = GLA_HEAD_K, GLA_HEAD_V, GLA_HEADS
    groups = min(SAMPLE_GROUP, n)
    rows = groups * n_new
    q0 = 3 * ATT_WIDTH // hk
    k0 = q0 + nh
    v0 = (3 * ATT_WIDTH + 2 * GLA_KEY_WIDTH) // hv
    r0 = v0 + nh
    state_spec = pl.BlockSpec((groups, 1, hk, hv), lambda i, h: (i, h, 0, 0))
    return pl.pallas_call(
        functools.partial(_gla_sample_body, n_new=n_new, groups=groups),
        grid=(n // groups, nh),
        in_specs=[
            pl.BlockSpec((rows, hk), lambda i, h: (i, q0 + h)),
            pl.BlockSpec((rows, hk), lambda i, h: (i, k0 + h)),
            pl.BlockSpec((rows, hv), lambda i, h: (i, v0 + h)),
            pl.BlockSpec((rows, hv), lambda i, h: (i, r0 + h)),
            pl.BlockSpec((rows, hk), lambda i, h: (i, h)),
            pl.BlockSpec((1, hv), lambda i, h: (0, h)),
            state_spec,
        ],
        out_specs=[pl.BlockSpec((rows, hv), lambda i, h: (i, h)), state_spec],
        out_shape=[
            jax.ShapeDtypeStruct((m, GLA_VALUE_WIDTH), BF16),
            jax.ShapeDtypeStruct(state.shape, F32),
        ],
        compiler_params=_params("parallel", "parallel"),
        name="gla_sample",
    )(z, z, z, z, la, g_out.reshape(1, -1), state)


def _out_cross_q_body(x_ref, oa_ref, og_ref, wa_ref, wg_ref, g_ref, wq_ref, x2_ref, qc_ref):
    oa = oa_ref[...].astype(BF16)
    x2 = x_ref[...] + _dot(oa, wa_ref[...]) + _dot(og_ref[...], wg_ref[...])
    x2_ref[...] = x2
    h = _rms(x2, g_ref[...]).astype(BF16)
    qc_ref[...] = (_dot(h, wq_ref[...]) * (MEM_HEAD_DIM ** -0.5)).astype(qc_ref.dtype)


def _out_cross_q(x, o_att, o_gla, w_out_att, w_out_gla, g_cross, w_q, *, tm=512):
    m, d = x.shape
    tm = min(tm, m)
    const = lambda shape: pl.BlockSpec(shape, lambda i: (0, 0))
    return pl.pallas_call(
        _out_cross_q_body,
        grid=(m // tm,),
        in_specs=[
            pl.BlockSpec((tm, d), lambda i: (i, 0)),
            pl.BlockSpec((tm, ATT_WIDTH), lambda i: (i, 0)),
            pl.BlockSpec((tm, GLA_VALUE_WIDTH), lambda i: (i, 0)),
            const(w_out_att.shape),
            const(w_out_gla.shape),
            const((1, d)),
            const(w_q.shape),
        ],
        out_specs=[pl.BlockSpec((tm, d), lambda i: (i, 0)), pl.BlockSpec((tm, MEM_WIDTH), lambda i: (i, 0))],
        out_shape=[jax.ShapeDtypeStruct((m, d), F32), jax.ShapeDtypeStruct((m, MEM_WIDTH), BF16)],
        compiler_params=_params("parallel"),
        name="out_cross_q",
    )(x, o_att, o_gla, w_out_att, w_out_gla, g_cross.reshape(1, d), w_q)


def _cross_heads(q, mk_ref, mv_ref):
    hd = MEM_HEAD_DIM
    heads = []
    for h in range(MEM_HEADS):
        s = _dot_nt(q[:, h * hd:(h + 1) * hd], mk_ref[0, h].astype(BF16))
        p = jnp.exp(s - jnp.max(s, axis=-1, keepdims=True))
        heads.append(_dot(p.astype(BF16), mv_ref[0, h].astype(BF16)) / jnp.sum(p, axis=-1, keepdims=True))
    return jnp.concatenate(heads, axis=1).astype(BF16)


def _cross_body(q_ref, mk_ref, mv_ref, x_ref, w_ref, o_ref, *, groups):
    rows = q_ref.shape[0]
    keys = groups * MEM_LEN
    hd = MEM_HEAD_DIM
    col = lax.broadcasted_iota(jnp.int32, (1, keys), 1)
    group_of_row = lax.broadcasted_iota(jnp.int32, (rows, 1), 0) // (rows // groups)
    own = (col // MEM_LEN) == group_of_row
    head_rows = lambda ref, h: jnp.concatenate(
        [ref[gi, pl.ds(h, MEM_LEN, stride=MEM_HEADS), :] for gi in range(groups)], axis=0).astype(BF16)
    heads = []
    for h in range(MEM_HEADS):
        s = jnp.where(own, _dot_nt(q_ref[:, h * hd:(h + 1) * hd], head_rows(mk_ref, h)), -jnp.inf)
        p = jnp.exp(s - jnp.max(s, axis=-1, keepdims=True))
        heads.append(_dot(p.astype(BF16), head_rows(mv_ref, h)) / jnp.sum(p, axis=-1, keepdims=True))
    o = jnp.concatenate(heads, axis=1).astype(BF16)
    o_ref[...] = x_ref[...] + _dot(o, w_ref[...])


def _post_mix_body(x_ref, oa_ref, og_ref, wa_ref, wg_ref, g_ref, wq_ref, mk_ref, mv_ref, wo_ref, o_ref):
    x2 = x_ref[...] + _dot(oa_ref[...].astype(BF16), wa_ref[...]) + _dot(og_ref[...], wg_ref[...])
    h = _rms(x2, g_ref[...]).astype(BF16)
    q = (_dot(h, wq_ref[...]) * (MEM_HEAD_DIM ** -0.5)).astype(BF16)
    o_ref[...] = x2 + _dot(_cross_heads(q, mk_ref, mv_ref), wo_ref[...])


def _post_mix(x, o_att, o_gla, w_out_att, w_out_gla, g_cross, w_q, mem_k, mem_v, w_o, *, seq, tm=512):
    m, d = x.shape
    tm = min(tm, seq)
    assert seq % tm == 0
    const = lambda a: pl.BlockSpec(a.shape, lambda i: (0,) * a.ndim, pipeline_mode=pl.Buffered(1))
    rows = lambda width: pl.BlockSpec((tm, width), lambda i: (i, 0))
    mem = pl.BlockSpec((1, MEM_HEADS, MEM_LEN, MEM_HEAD_DIM), lambda i: (i // (seq // tm), 0, 0, 0))
    g = g_cross.reshape(1, d)
    return pl.pallas_call(
        _post_mix_body,
        grid=(m // tm,),
        in_specs=[rows(d), rows(ATT_WIDTH), rows(GLA_VALUE_WIDTH), const(w_out_att), const(w_out_gla), const(g),
                  const(w_q), mem, mem, const(w_o)],
        out_specs=rows(d),
        out_shape=jax.ShapeDtypeStruct((m, d), F32),
        compiler_params=_params("parallel"),
        name="post_mix",
    )(x, o_att, o_gla, w_out_att, w_out_gla, g, w_q, mem_k, mem_v, w_o)


def _cross(qc, mem_k, mem_v, x, w_o, *, rows, groups):
    m, d = x.shape
    mem_k, mem_v = (a.reshape(a.shape[0], MEM_LEN * MEM_HEADS, MEM_HEAD_DIM) for a in (mem_k, mem_v))
    mem = pl.BlockSpec((groups, MEM_LEN * MEM_HEADS, MEM_HEAD_DIM), lambda i: (i, 0, 0))
    return pl.pallas_call(
        functools.partial(_cross_body, groups=groups),
        grid=(m // rows,),
        in_specs=[
            pl.BlockSpec((rows, MEM_WIDTH), lambda i: (i, 0)),
            mem,
            mem,
            pl.BlockSpec((rows, d), lambda i: (i, 0)),
            pl.BlockSpec(w_o.shape, lambda i: (0, 0)),
        ],
        out_specs=pl.BlockSpec((rows, d), lambda i: (i, 0)),
        out_shape=jax.ShapeDtypeStruct((m, d), F32),
        compiler_params=_params("parallel"),
        name="cross",
    )(qc, mem_k, mem_v, x, w_o)


def kernel(x_prompt, x_sample, mem_prompt, cache_win_k, cache_win_v, state_gla, cache_mem_k, cache_mem_v, g_ffn1, w1_gate, w1_up, w1_down, g_mix, w_in, w_gla_a2, b_gla_a, g_gla_out, w_out, g_mem, w_mem_k, w_mem_v, g_cross, w_cross_q, w_cross_o, g_ffn2, w2_gate, w2_up, w2_down, g_final):
    b, seq, d = x_prompt.shape
    n, n_new, _ = x_sample.shape
    depth = g_ffn1.shape[0]
    past_len = cache_win_k.shape[2]
    n_keep = min(MAX_WINDOW, seq)
    bf = lambda a: a.astype(BF16)

    xp = x_prompt.reshape(b * seq, d)
    xs = x_sample.reshape(n * n_new, d)
    rope_p = _rope_tables(jnp.arange(seq))
    rope_s = _rope_tables(jnp.tile(past_len + jnp.arange(n_new), n))
    outs = [[] for _ in range(8)]
    cols = np.cumsum([0, ATT_WIDTH, ATT_WIDTH, ATT_WIDTH, GLA_KEY_WIDTH, GLA_KEY_WIDTH, GLA_VALUE_WIDTH, GLA_GATE_RANK])
    for l in range(depth):
        w_main = bf(jnp.concatenate([w_in[l][:, :cols[6]], w_in[l][:, cols[7]:]], axis=1))
        w_lr = bf(jnp.pad(w_in[l][:, cols[6]:cols[7]], ((0, 0), (0, LANES - GLA_GATE_RANK))))
        w_a2 = bf(jnp.pad(w_gla_a2[l], ((0, LANES - GLA_GATE_RANK), (0, 0))))
        w_out_att, w_out_gla = bf(w_out[l][:ATT_WIDTH]), bf(w_out[l][ATT_WIDTH:])
        w_mem = bf(jnp.concatenate([w_mem_k[l], w_mem_v[l]], axis=1))
        w_cq, w_co = bf(w_cross_q[l]), bf(w_cross_o[l])
        gf = g_final if l == depth - 1 else None
        mix_in = lambda x, rope: _mix_in(x, g_mix[l], w_main, w_lr, w_a2, b_gla_a[l], rope)

        w2 = (bf(w2_gate[l]), bf(w2_up[l]), bf(w2_down[l]))
        xs, *w1 = _ffn(xs, g_ffn1[l], w1_gate[l], w1_up[l], w1_down[l])
        zs, la_s = mix_in(xs, rope_s)
        weights_rider, values_rider, attn_finish = _attn_sample_riders(
            zs.reshape(n, n_new, Z_COLS), cache_win_k[l], cache_win_v[l])

        xp, att_w, att_aux = _ffn(xp, g_ffn1[l], *w1, rider=weights_rider)
        mem_kv = _norm_proj(mem_prompt.reshape(b * MEM_LEN, d), g_mem[l], w_mem).reshape(b, MEM_LEN, 2 * MEM_WIDTH)
        mem_k = mem_kv[:, :, :MEM_WIDTH].reshape(b, MEM_LEN, MEM_HEADS, MEM_HEAD_DIM)
        mem_v = mem_kv[:, :, MEM_WIDTH:].reshape(b, MEM_LEN, MEM_HEADS, MEM_HEAD_DIM)
        zp, la_p = mix_in(xp, rope_p)
        zp3 = zp.reshape(b, seq, Z_COLS)
        o_att_p = _attn_prompt(zp3).reshape(b * seq, ATT_WIDTH)
        o_gla_p, st_p = _gla_prompt(zp3, la_p.reshape(b, seq, GLA_KEY_WIDTH), g_gla_out[l])
        xp = _post_mix(xp, o_att_p, o_gla_p.reshape(b * seq, GLA_VALUE_WIDTH), w_out_att, w_out_gla, g_cross[l], w_cq,
                       mem_k.transpose(0, 2, 1, 3), mem_v.transpose(0, 2, 1, 3), w_co, seq=seq)
        xp, o_att_s = _ffn(xp, g_ffn2[l], *w2, g_final=gf, rider=values_rider(att_w, att_aux))

        o_att_s = attn_finish(o_att_s).reshape(n * n_new, ATT_WIDTH)
        o_gla_s, st_s = _gla_sample(zs, la_s, g_gla_out[l], state_gla[l], n_new)
        xs, qc = _out_cross_q(xs, o_att_s, o_gla_s, w_out_att, w_out_gla, g_cross[l], w_cq)
        groups = min(SAMPLE_GROUP, n)
        xs = _cross(qc, cache_mem_k[l], cache_mem_v[l], xs, w_co, rows=groups * n_new, groups=groups)
        xs = _ffn(xs, g_ffn2[l], *w2, g_final=gf)

        heads = lambda a, t: a.reshape(-1, t, ATT_HEADS, ATT_HEAD_DIM)
        outs[0].append(heads(zp3[:, seq - n_keep:, ATT_WIDTH:2 * ATT_WIDTH], n_keep))
        outs[1].append(heads(zp3[:, seq - n_keep:, 2 * ATT_WIDTH:3 * ATT_WIDTH], n_keep))
        outs[2].append(st_p)
        outs[3].append(mem_k)
        outs[4].append(mem_v)
        outs[5].append(heads(zs[:, ATT_WIDTH:2 * ATT_WIDTH], n_new))
        outs[6].append(heads(zs[:, 2 * ATT_WIDTH:3 * ATT_WIDTH], n_new))
        outs[7].append(st_s)

    return (xp.reshape(b, seq, d), xs.reshape(n, n_new, d)) + tuple(jnp.stack(o) for o in outs)
```

```python
import functools

import jax
import jax.numpy as jnp
import numpy as np
from jax import lax
from jax.experimental import pallas as pl
from jax.experimental.pallas import tpu as pltpu

F32 = jnp.float32
BF16 = jnp.bfloat16

EPS = 1e-6
ATT_HEADS = 16
ATT_HEAD_DIM = 64
ATT_WIDTH = ATT_HEADS * ATT_HEAD_DIM
DILATIONS = (1, 4, 16)
N_BACK = 128
BAND = 128
ROT_DIM = 16
ROPE_THETA = 500000.0
GLA_HEADS = 4
GLA_HEAD_K = 128
GLA_HEAD_V = 256
GLA_KEY_WIDTH = GLA_HEADS * GLA_HEAD_K
GLA_VALUE_WIDTH = GLA_HEADS * GLA_HEAD_V
GLA_GATE_RANK = 16
GLA_GATE_TAU = 16.0
GLA_CHUNK = 64
MEM_LEN = 256
MEM_HEADS = 4
MEM_HEAD_DIM = 128
MEM_WIDTH = MEM_HEADS * MEM_HEAD_DIM
MAX_WINDOW = 2048
LANES = 128
SAMPLE_GROUP = 8
Z_COLS = 3 * ATT_WIDTH + 2 * GLA_KEY_WIDTH + 2 * GLA_VALUE_WIDTH
Z_TILE = 1024
VMEM_LIMIT_BYTES = 56 * 1024 * 1024


def _params(*semantics, vmem_limit_bytes=VMEM_LIMIT_BYTES):
    return pltpu.CompilerParams(dimension_semantics=semantics, vmem_limit_bytes=vmem_limit_bytes)


def _rms(x, g):
    return x * lax.rsqrt(jnp.mean(x * x, axis=-1, keepdims=True) + EPS) * g


def _dot(a, b):
    return jnp.dot(a, b, preferred_element_type=F32)


def _dot_nt(a, b):
    return lax.dot_general(a, b, (((1,), (1,)), ((), ())), preferred_element_type=F32)


def _dot_tn(a, b):
    return lax.dot_general(a, b, (((0,), (0,)), ((), ())), preferred_element_type=F32)


def _split_bf16(x):
    hi = x.astype(BF16)
    lo = (x - hi.astype(F32)).astype(BF16)
    return hi, lo


RIDER_HOST_ROWS = 1024
RIDER_VMEM_LIMIT_BYTES = 62 * 1024 * 1024


def _ffn_body(*refs, nf, final, emit, rider):
    x_ref, g_ref, wg_ref, wu_ref, wd_ref = refs[:5]
    refs = refs[5:]
    if final:
        gf_ref, refs = refs[0], refs[1:]
    if rider is not None:
        rider_in, refs = refs[:len(rider.args)], refs[len(rider.args):]
        rider_out = refs[-1 - len(rider.out_specs):-1]
    o_ref, h_ref = refs[0], refs[-1]
    f = pl.program_id(1)

    @pl.when(f == 0)
    def _():
        h_ref[...] = _rms(x_ref[...], g_ref[...]).astype(BF16)
        o_ref[...] = jnp.zeros_like(o_ref)

    def step(with_rider):
        wg, wu, wd = wg_ref[...].astype(BF16), wu_ref[...].astype(BF16), wd_ref[...].astype(BF16)
        if emit:
            for w_out_ref, w in zip(refs[1:4], (wg, wu, wd)):
                w_out_ref[...] = w
        h = h_ref[...]
        gate = _dot(h, wg)
        if with_rider:
            carried = rider.first(*rider_in)
        up = _dot(h, wu)
        if with_rider:
            rider.second(*rider_in, *rider_out, carried)
        act = (gate * jax.nn.sigmoid(gate) * up).astype(BF16)
        o_ref[...] += _dot(act, wd)

    if rider is None:
        step(False)
    else:
        active = rider.active(f)
        pl.when(active)(functools.partial(step, True))
        pl.when(jnp.logical_not(active))(functools.partial(step, False))

    @pl.when(f == nf - 1)
    def _():
        y = x_ref[...] + 0.5 * o_ref[...]
        if final:
            y = _rms(y, gf_ref[...])
        o_ref[...] = y


def _ffn(x, g, wg, wu, wd, g_final=None, *, tm=512, tf=None, rider=None):
    m, d = x.shape
    dff = wg.shape[1]
    if rider is not None:
        tm = RIDER_HOST_ROWS
    tm = min(tm, m)
    emit = wg.dtype != BF16
    if tf is None:
        tf = 256 if emit or rider is not None else 512
    nf = dff // tf
    assert dff % tf == 0 and m % tm == 0
    assert not emit or m == tm
    final = g_final is not None
    row = pl.BlockSpec((1, d), lambda i, f: (0, 0))
    w_specs = [
        pl.BlockSpec((d, tf), lambda i, f: (0, f)),
        pl.BlockSpec((d, tf), lambda i, f: (0, f)),
        pl.BlockSpec((tf, d), lambda i, f: (f, 0)),
    ]
    x_mode = {} if rider is None else dict(pipeline_mode=pl.Buffered(1))
    in_specs = [pl.BlockSpec((tm, d), lambda i, f: (i, 0), **x_mode), row] + w_specs
    args = [x, g.reshape(1, d), wg, wu, wd]
    if final:
        in_specs.append(row)
        args.append(g_final.reshape(1, d))
    out_specs = [pl.BlockSpec((tm, d), lambda i, f: (i, 0))]
    out_shape = [jax.ShapeDtypeStruct((m, d), F32)]
    if emit:
        out_specs += w_specs
        out_shape += [jax.ShapeDtypeStruct(w.shape, BF16) for w in (wg, wu, wd)]
    if rider is not None:
        rider = rider(m // tm, nf)
        in_specs += rider.in_specs
        args += rider.args
        out_specs += rider.out_specs
        out_shape += rider.out_shape
    out = pl.pallas_call(
        functools.partial(_ffn_body, nf=nf, final=final, emit=emit, rider=rider),
        grid=(m // tm, nf),
        in_specs=in_specs,
        out_specs=out_specs,
        out_shape=out_shape,
        scratch_shapes=[pltpu.VMEM((tm, d), BF16)],
        compiler_params=(_params("parallel", "arbitrary") if rider is None else
                         _params("arbitrary", "arbitrary", vmem_limit_bytes=RIDER_VMEM_LIMIT_BYTES)),
        name=("ffn_final" if final else "ffn") + ("_f32w" if emit else "") + ("" if rider is None else "_" + rider.name),
    )(*args)
    return out if len(out) > 1 else out[0]


def _mix_in_body(x_ref, g_ref, w_ref, wlr_ref, wa2_ref, ba_ref, rope_ref, z_ref, la_ref, h_ref):
    n = pl.program_id(1)

    @pl.when(n == 0)
    def _():
        h = _rms(x_ref[...], g_ref[...]).astype(BF16)
        h_ref[...] = h
        g_lr = _dot(h, wlr_ref[...])
        pre = _dot(g_lr.astype(BF16), wa2_ref[...]) + ba_ref[...]
        la_ref[...] = (jnp.minimum(pre, 0.0) - jnp.log1p(jnp.exp(-jnp.abs(pre)))) * (1.0 / GLA_GATE_TAU)

    z = _dot(h_ref[...], w_ref[...])

    @pl.when(n >= 2)
    def _():
        z_ref[...] = z

    @pl.when(n < 2)
    def _():
        cos, sin_up, sin_dn = rope_ref[0], rope_ref[1], rope_ref[2]
        scale = jnp.where(n == 0, ATT_HEAD_DIM ** -0.5, 1.0).astype(F32)
        half = ROT_DIM // 2
        for j in range(Z_TILE // LANES):
            zc = z[:, j * LANES:(j + 1) * LANES]
            rot = zc * cos + pltpu.roll(zc, half, axis=1) * sin_up + pltpu.roll(zc, LANES - half, axis=1) * sin_dn
            z_ref[:, j * LANES:(j + 1) * LANES] = rot * scale


def _mix_in(x, g, w_main, w_lr, w_a2, b_a, rope, *, tm=512):
    m, d = x.shape
    tm = min(tm, m)
    n_rope = rope.shape[1] // tm
    return pl.pallas_call(
        _mix_in_body,
        grid=(m // tm, Z_COLS // Z_TILE),
        in_specs=[
            pl.BlockSpec((tm, d), lambda i, n: (i, 0)),
            pl.BlockSpec((1, d), lambda i, n: (0, 0)),
            pl.BlockSpec((d, Z_TILE), lambda i, n: (0, n)),
            pl.BlockSpec((d, LANES), lambda i, n: (0, 0)),
            pl.BlockSpec((LANES, GLA_KEY_WIDTH), lambda i, n: (0, 0)),
            pl.BlockSpec((1, GLA_KEY_WIDTH), lambda i, n: (0, 0)),
            pl.BlockSpec((3, tm, LANES), lambda i, n: (0, i % n_rope, 0)),
        ],
        out_specs=[
            pl.BlockSpec((tm, Z_TILE), lambda i, n: (i, n)),
            pl.BlockSpec((tm, GLA_KEY_WIDTH), lambda i, n: (i, 0)),
        ],
        out_shape=[
            jax.ShapeDtypeStruct((m, Z_COLS), F32),
            jax.ShapeDtypeStruct((m, GLA_KEY_WIDTH), F32),
        ],
        scratch_shapes=[pltpu.VMEM((tm, d), BF16)],
        compiler_params=_params("parallel", "arbitrary"),
        name="mix_in",
    )(x, g.reshape(1, d), w_main, w_lr, w_a2, b_a.reshape(1, -1), rope)


def _rope_tables(pos):
    half = ROT_DIM // 2
    inv = ROPE_THETA ** (-jnp.arange(half, dtype=F32) * 2.0 / ROT_DIM)
    ang = pos.astype(F32)[:, None] * inv[None, :]
    cos, sin = jnp.cos(ang), jnp.sin(ang)
    t = pos.shape[0]
    zeros = jnp.zeros((t, half), F32)
    rest0 = jnp.zeros((t, ATT_HEAD_DIM - ROT_DIM), F32)
    c = jnp.concatenate([cos, cos, jnp.ones_like(rest0)], axis=1)
    s_up = jnp.concatenate([zeros, sin, rest0], axis=1)
    s_dn = jnp.concatenate([-sin, zeros, rest0], axis=1)
    reps = LANES // ATT_HEAD_DIM
    return jnp.stack([jnp.tile(c, (1, reps)), jnp.tile(s_up, (1, reps)), jnp.tile(s_dn, (1, reps))])


def _norm_proj_body(x_ref, g_ref, w_ref, o_ref):
    h = _rms(x_ref[...], g_ref[...]).astype(BF16)
    o_ref[...] = _dot(h, w_ref[...])


def _norm_proj(x, g, w, *, tm=512):
    m, d = x.shape
    n = w.shape[1]
    tm = min(tm, m)
    return pl.pallas_call(
        _norm_proj_body,
        grid=(m // tm,),
        in_specs=[
            pl.BlockSpec((tm, d), lambda i: (i, 0)),
            pl.BlockSpec((1, d), lambda i: (0, 0)),
            pl.BlockSpec((d, n), lambda i: (0, 0)),
        ],
        out_specs=pl.BlockSpec((tm, n), lambda i: (i, 0)),
        out_shape=jax.ShapeDtypeStruct((m, n), F32),
        compiler_params=_params("parallel"),
        name="memory_kv",
    )(x, g.reshape(1, d), w)


ATTN_UNROLL = 32
MERGE_ROWS = 256


def _band_bias():
    i = np.arange(BAND)[:, None]
    j = np.arange(2 * BAND)[None, :]
    out = []
    for off in (0, BAND):
        dist = off + i - j
        out.append(np.where((dist >= 0) & (dist <= N_BACK), 0.0, -np.inf))
    return jnp.asarray(np.stack(out), F32)


def _attn_prompt_body(q_ref, k_ref, v_ref, bias_ref, o_ref, acc_ref, m_ref, l_ref, *, seq):
    hd = ATT_HEAD_DIM
    assert LANES == 2 * hd
    blocks = seq // BAND
    first_head = lax.broadcasted_iota(jnp.int32, (1, LANES), 1) < hd
    for ci, dil in enumerate(DILATIONS):
        nb = blocks // dil
        shift = nb.bit_length() - 1
        assert nb == 1 << shift

        def step(it, carry, ci=ci, dil=dil, nb=nb, shift=shift):
            loaded = []
            for u in range(ATTN_UNROLL):
                w = it * ATTN_UNROLL + u
                r = lax.shift_right_logical(w, shift)
                n = w & (nb - 1)
                base = jnp.maximum(n - 1, 0) * BAND
                if dil == 1:
                    q_rows = pl.ds(pl.multiple_of(n * BAND, BAND), BAND)
                    k_rows = pl.ds(pl.multiple_of(base, BAND), 2 * BAND)
                else:
                    q_rows = pl.ds(r + dil * BAND * n, BAND, stride=dil)
                    k_rows = pl.ds(r + dil * base, 2 * BAND, stride=dil)
                bias = bias_ref[jnp.minimum(n, 1)]
                loaded.append((q_rows, q_ref[0, q_rows, :], k_ref[0, k_rows, :], v_ref[0, k_rows, :], bias))
            results = []
            for q_rows, q, k, v, bias in loaded:
                kb = k.astype(BF16)
                qs = (jnp.where(first_head, q, 0.0), jnp.where(first_head, 0.0, q))
                vs = (jnp.where(first_head, v, 1.0), jnp.where(first_head, 1.0, v))
                out, mx = [], []
                for hh in range(LANES // hd):
                    s = _dot_nt(qs[hh].astype(BF16), kb) + bias
                    m_c = jnp.max(s, axis=1, keepdims=True)
                    p = jnp.exp(s - m_c)
                    out.append(_dot(p.astype(BF16), vs[hh].astype(BF16)))
                    mx.append(m_c)
                results.append((q_rows, jnp.where(first_head, out[0], out[1]),
                                jnp.where(first_head, mx[0], mx[1]), jnp.where(first_head, out[1], out[0])))
            for q_rows, pv, mx, den_swapped in results:
                acc_ref[ci, q_rows, :] = pv
                m_ref[ci, q_rows, :] = mx
                l_ref[ci, q_rows, :] = den_swapped
            return carry

        lax.fori_loop(0, blocks // ATTN_UNROLL, step, 0)

    def merge(i, carry):
        rows = pl.ds(pl.multiple_of(i * MERGE_ROWS, MERGE_ROWS), MERGE_ROWS)
        ms = [m_ref[ci, rows, :] for ci in range(len(DILATIONS))]
        m_all = functools.reduce(jnp.maximum, ms)
        ws = [jnp.exp(m - m_all) for m in ms]
        num = sum(w * acc_ref[ci, rows, :] for ci, w in enumerate(ws))
        den = sum(w * pltpu.roll(l_ref[ci, rows, :], hd, axis=1) for ci, w in enumerate(ws))
        o_ref[0, rows, :] = (num / den).astype(o_ref.dtype)
        return carry

    lax.fori_loop(0, seq // MERGE_ROWS, merge, 0)


def _attn_prompt(z3):
    b, seq, _ = z3.shape
    pairs = ATT_WIDTH // LANES
    assert seq % (DILATIONS[-1] * BAND) == 0 and seq // DILATIONS[-1] >= 2 * BAND
    assert (seq // BAND) % ATTN_UNROLL == 0 and seq % MERGE_ROWS == 0
    blk = lambda off: pl.BlockSpec((1, seq, LANES), lambda i, p: (i, 0, off + p))
    return pl.pallas_call(
        functools.partial(_attn_prompt_body, seq=seq),
        grid=(b, pairs),
        in_specs=[blk(0), blk(pairs), blk(2 * pairs),
                  pl.BlockSpec((2, BAND, 2 * BAND), lambda i, p: (0, 0, 0))],
        out_specs=pl.BlockSpec((1, seq, LANES), lambda i, p: (i, 0, p)),
        out_shape=jax.ShapeDtypeStruct((b, seq, ATT_WIDTH), BF16),
        scratch_shapes=[pltpu.VMEM((len(DILATIONS), seq, LANES), F32)] * 3,
        compiler_params=_params("parallel", "parallel"),
        name="attn_prompt",
    )(z3, z3, z3, _band_bias())


SAMPLE_ROWS = 8


def _sample_multiplicity(n_buf, n_new):
    t = (np.arange(SAMPLE_ROWS) % n_new)[:, None]
    r = np.arange(n_buf)[None, :]
    mult = np.zeros((SAMPLE_ROWS, n_buf), np.float32)
    for dil in DILATIONS:
        back = n_buf + t - r
        mult += (back % dil == 0) & (back >= dil) & (back <= dil * N_BACK)
    return jnp.asarray(mult, F32)


def _new_row_multiplicity(n_new):
    t_row = lax.broadcasted_iota(jnp.int32, (SAMPLE_ROWS, 1), 0) % n_new
    return [(t_row >= t).astype(F32) + (len(DILATIONS) - 1.0) * (t_row == t).astype(F32) for t in range(n_new)]


def _attn_scores(q_ref, kn_ref, kt_ref, mult_ref, *, n_new):
    live = mult_ref[...] > 0.0
    mult_new = _new_row_multiplicity(n_new)
    scores = []
    for h in range(ATT_HEADS):
        q = q_ref[0, h]
        kn = kn_ref[0, h]
        s = jnp.where(live, _dot(q.astype(BF16), kt_ref[0, h].astype(BF16)), -jnp.inf)
        s_new = [jnp.where(mult_new[t] > 0.0, jnp.sum(q * kn[t:t + 1], axis=1, keepdims=True), -jnp.inf)
                 for t in range(n_new)]
        scores.append((s, s_new))
    return scores


def _attn_weights(q_ref, kn_ref, kt_ref, mult_ref, w_ref, aux_ref, scores, *, n_new):
    mult = mult_ref[...]
    mult_new = _new_row_multiplicity(n_new)
    lane = lax.broadcasted_iota(jnp.int32, (1, LANES), 1)
    for h, (s, s_new) in enumerate(scores):
        m = functools.reduce(jnp.maximum, s_new, jnp.max(s, axis=1, keepdims=True))
        w = mult * jnp.exp(s - m)
        w_new = [mult_new[t] * jnp.exp(s_new[t] - m) for t in range(n_new)]
        den = jnp.sum(w, axis=1, keepdims=True) + sum(w_new)
        w_ref[0, h] = w
        aux_ref[0, h] = sum(jnp.where(lane == t, w_new[t], 0.0) for t in range(n_new)) + jnp.where(lane == n_new, den, 0.0)


def _attn_value_products(w_ref, aux_ref, vn_ref, vt_ref):
    return [_dot_nt(w_ref[0, h].astype(BF16), vt_ref[0, h].astype(BF16)) for h in range(ATT_HEADS)]


def _attn_outputs(w_ref, aux_ref, vn_ref, vt_ref, o_ref, products, *, n_new):
    for h, o in enumerate(products):
        aux = aux_ref[0, h]
        vn = vn_ref[0, h]
        o = o + sum(aux[:, t:t + 1] * vn[t:t + 1] for t in range(n_new))
        o_ref[0, h] = o / aux[:, n_new:n_new + 1]


class _Rider:
    def __init__(self, name, in_specs, args, out_specs, out_shape, first, second, active):
        self.name, self.in_specs, self.args, self.out_specs, self.out_shape = name, in_specs, args, out_specs, out_shape
        self.first, self.second = first, second
        self.active = active


def _attn_sample_riders(z3, cache_k, cache_v):
    n, n_new, _ = z3.shape
    n_buf = cache_k.shape[1]
    nh, hd = ATT_HEADS, ATT_HEAD_DIM
    assert n_buf >= DILATIONS[-1] * N_BACK and n_new < SAMPLE_ROWS and n_new < LANES

    def per_head(col):
        a = z3[:, :, col * ATT_WIDTH:(col + 1) * ATT_WIDTH].reshape(n, n_new, nh, hd).transpose(0, 2, 1, 3)
        return jnp.pad(a, ((0, 0), (0, 0), (0, SAMPLE_ROWS - n_new), (0, 0)))

    rows_on_lanes = lambda c: c.transpose(0, 2, 3, 1)

    def specs(tiles, steps):
        per_tile = n // tiles
        assert per_tile <= steps and per_tile * tiles == n
        where = lambda i, f: (i * per_tile + jnp.minimum(f, per_tile - 1), 0, 0, 0)
        block = lambda minor: pl.BlockSpec((1, nh) + minor, where)
        active = lambda f: f < per_tile
        return (block((SAMPLE_ROWS, hd)), block((hd, n_buf)), block((SAMPLE_ROWS, n_buf)), block((SAMPLE_ROWS, LANES)),
                active)

    def weights_rider(tiles, steps):
        small, cache, weights, aux, active = specs(tiles, steps)
        return _Rider(
            "attn_weights",
            [small, small, cache, pl.BlockSpec((SAMPLE_ROWS, n_buf), lambda i, f: (0, 0))],
            [per_head(0), per_head(1), rows_on_lanes(cache_k), _sample_multiplicity(n_buf, n_new)],
            [weights, aux],
            [jax.ShapeDtypeStruct((n, nh, SAMPLE_ROWS, n_buf), F32), jax.ShapeDtypeStruct((n, nh, SAMPLE_ROWS, LANES), F32)],
            functools.partial(_attn_scores, n_new=n_new),
            functools.partial(_attn_weights, n_new=n_new),
            active,
        )

    def values_rider(w, aux_w):
        def make(tiles, steps):
            small, cache, weights, aux, active = specs(tiles, steps)
            return _Rider(
                "attn_values",
                [weights, aux, small, cache],
                [w, aux_w, per_head(2), rows_on_lanes(cache_v)],
                [small],
                [jax.ShapeDtypeStruct((n, nh, SAMPLE_ROWS, hd), F32)],
                _attn_value_products,
                functools.partial(_attn_outputs, n_new=n_new),
                active,
            )
        return make

    finish = lambda o: o[:, :, :n_new].transpose(0, 2, 1, 3).reshape(n, n_new, ATT_WIDTH)
    return weights_rider, values_rider, finish


def _gla_rows(q, k, v, li, states, causal, same, group_of_row):
    c = q.shape[0]
    hi, lo = _split_bf16(li)
    tri = causal.astype(BF16)
    b = _dot(tri, hi) + _dot(tri, lo)
    if len(states) == 1:
        b_end = b[c - 1:c, :]
    else:
        full = same.astype(BF16)
        b_end = _dot(full, hi) + _dot(full, lo)
    q_t = (q * (GLA_HEAD_K ** -0.5) * jnp.exp(b)).astype(BF16)
    k_n = (k * jnp.exp(-b)).astype(BF16)
    k_w = k * jnp.exp(b_end - b)
    vb = v.astype(BF16)
    a = jnp.where(causal, _dot_nt(q_t, k_n), 0.0)
    o = _dot(a.astype(BF16), vb)
    ones = jnp.ones((c, LANES), BF16)
    new_states = []
    for gi, s in enumerate(states):
        if len(states) == 1:
            o = o + _dot(q_t, s.astype(BF16))
            kw_g, hi_g, lo_g = k_w.astype(BF16), hi, lo
        else:
            mine = group_of_row == gi
            o = o + jnp.where(mine, _dot(q_t, s.astype(BF16)), 0.0)
            kw_g = jnp.where(mine, k_w, 0.0).astype(BF16)
            hi_g = jnp.where(mine, hi, jnp.zeros_like(hi))
            lo_g = jnp.where(mine, lo, jnp.zeros_like(lo))
        decay = jnp.exp(_dot_tn(hi_g, ones) + _dot_tn(lo_g, ones))
        decay = jnp.concatenate([decay] * (s.shape[1] // LANES), axis=1)
        new_states.append(decay * s + _dot_tn(kw_g, vb))
    return o, new_states


def _gla_finish(o, g_out, rg):
    o = o * lax.rsqrt(jnp.mean(o * o, axis=-1, keepdims=True) + EPS)
    return (o * g_out) * (rg * jax.nn.sigmoid(rg))


GLA_HEADS_PER_STEP = 4
GLA_SEQ_TILE = 1024
GLA_CHUNK_UNROLL = 4
GLA_SAMPLE_HEADS_PER_STEP = 2


def _gla_prompt_body(q_ref, k_ref, v_ref, rg_ref, la_ref, g_ref, o_ref, s_ref, *, rows, chunk, heads):
    hk, hv = GLA_HEAD_K, GLA_HEAD_V
    r = lax.broadcasted_iota(jnp.int32, (chunk, chunk), 0)
    c = lax.broadcasted_iota(jnp.int32, (chunk, chunk), 1)
    causal = c <= r

    @pl.when(pl.program_id(2) == 0)
    def _():
        s_ref[...] = jnp.zeros_like(s_ref)

    def step(i, carry):
        rr = pl.ds(pl.multiple_of(i * chunk, chunk), chunk)
        for hh in range(heads):
            ks = slice(hh * hk, (hh + 1) * hk)
            vs = slice(hh * hv, (hh + 1) * hv)
            o, (s_new,) = _gla_rows(q_ref[0, rr, ks], k_ref[0, rr, ks], v_ref[0, rr, vs], la_ref[0, rr, ks],
                                    [s_ref[0, hh]], causal, None, None)
            s_ref[0, hh] = s_new
            o_ref[0, rr, vs] = _gla_finish(o, g_ref[:, vs], rg_ref[0, rr, vs]).astype(o_ref.dtype)
        return carry

    lax.fori_loop(0, rows // chunk, step, 0, unroll=GLA_CHUNK_UNROLL)


def _gla_prompt(z3, la3, g_out):
    b, seq, _ = z3.shape
    hk, hv, nh = GLA_HEAD_K, GLA_HEAD_V, GLA_HEADS
    chunk = int(np.gcd(seq, GLA_CHUNK))
    hs = GLA_HEADS_PER_STEP
    rows = min(GLA_SEQ_TILE, seq)
    assert seq % rows == 0 and (rows // chunk) % GLA_CHUNK_UNROLL == 0 and nh % hs == 0
    q0 = 3 * ATT_WIDTH // (hs * hk)
    k0 = q0 + nh // hs
    v0 = (3 * ATT_WIDTH + 2 * GLA_KEY_WIDTH) // (hs * hv)
    r0 = v0 + nh // hs
    return pl.pallas_call(
        functools.partial(_gla_prompt_body, rows=rows, chunk=chunk, heads=hs),
        grid=(b, nh // hs, seq // rows),
        in_specs=[
            pl.BlockSpec((1, rows, hs * hk), lambda i, h, t: (i, t, q0 + h)),
            pl.BlockSpec((1, rows, hs * hk), lambda i, h, t: (i, t, k0 + h)),
            pl.BlockSpec((1, rows, hs * hv), lambda i, h, t: (i, t, v0 + h)),
            pl.BlockSpec((1, rows, hs * hv), lambda i, h, t: (i, t, r0 + h)),
            pl.BlockSpec((1, rows, hs * hk), lambda i, h, t: (i, t, h)),
            pl.BlockSpec((1, hs * hv), lambda i, h, t: (0, h)),
        ],
        out_specs=[
            pl.BlockSpec((1, rows, hs * hv), lambda i, h, t: (i, t, h)),
            pl.BlockSpec((1, hs, hk, hv), lambda i, h, t: (i, h, 0, 0)),
        ],
        out_shape=[
            jax.ShapeDtypeStruct((b, seq, GLA_VALUE_WIDTH), BF16),
            jax.ShapeDtypeStruct((b, nh, hk, hv), F32),
        ],
        compiler_params=_params("parallel", "parallel", "arbitrary"),
        name="gla_prompt",
    )(z3, z3, z3, z3, la3, g_out.reshape(1, -1))


def _gla_sample_body(q_ref, k_ref, v_ref, rg_ref, la_ref, g_ref, s0_ref, o_ref, s_ref, *, n_new, groups, heads):
    hk, hv = GLA_HEAD_K, GLA_HEAD_V
    rows = n_new * groups
    r = lax.broadcasted_iota(jnp.int32, (rows, rows), 0)
    c = lax.broadcasted_iota(jnp.int32, (rows, rows), 1)
    same = (r // n_new) == (c // n_new)
    causal = same & (c <= r)
    group_of_row = lax.broadcasted_iota(jnp.int32, (rows, 1), 0) // n_new
    for hh in range(heads):
        ks = slice(hh * hk, (hh + 1) * hk)
        vs = slice(hh * hv, (hh + 1) * hv)
        o, new_states = _gla_rows(q_ref[:, ks], k_ref[:, ks], v_ref[:, vs], la_ref[:, ks],
                                  [s0_ref[gi, hh] for gi in range(groups)], causal, same, group_of_row)
        for gi in range(groups):
            s_ref[gi, hh] = new_states[gi]
        o_ref[:, vs] = _gla_finish(o, g_ref[:, vs], rg_ref[:, vs]).astype(o_ref.dtype)


def _gla_sample(z, la, g_out, state, n_new):
    m = z.shape[0]
    n = m // n_new
    hk, hv, nh = GLA_HEAD_K, GLA_HEAD_V, GLA_HEADS
    groups = min(SAMPLE_GROUP, n)
    rows = groups * n_new
    hs = GLA_SAMPLE_HEADS_PER_STEP
    q0 = 3 * ATT_WIDTH // (hs * hk)
    k0 = q0 + nh // hs
    v0 = (3 * ATT_WIDTH + 2 * GLA_KEY_WIDTH) // (hs * hv)
    r0 = v0 + nh // hs
    state_spec = pl.BlockSpec((groups, hs, hk, hv), lambda i, h: (i, h, 0, 0))
    return pl.pallas_call(
        functools.partial(_gla_sample_body, n_new=n_new, groups=groups, heads=hs),
        grid=(n // groups, nh // hs),
        in_specs=[
            pl.BlockSpec((rows, hs * hk), lambda i, h: (i, q0 + h)),
            pl.BlockSpec((rows, hs * hk), lambda i, h: (i, k0 + h)),
            pl.BlockSpec((rows, hs * hv), lambda i, h: (i, v0 + h)),
            pl.BlockSpec((rows, hs * hv), lambda i, h: (i, r0 + h)),
            pl.BlockSpec((rows, hs * hk), lambda i, h: (i, h)),
            pl.BlockSpec((1, hs * hv), lambda i, h: (0, h)),
            state_spec,
        ],
        out_specs=[pl.BlockSpec((rows, hs * hv), lambda i, h: (i, h)), state_spec],
        out_shape=[
            jax.ShapeDtypeStruct((m, GLA_VALUE_WIDTH), BF16),
            jax.ShapeDtypeStruct(state.shape, F32),
        ],
        compiler_params=_params("parallel", "parallel"),
        name="gla_sample",
    )(z, z, z, z, la, g_out.reshape(1, -1), state)


def _out_cross_q_body(x_ref, oa_ref, og_ref, wa_ref, wg_ref, g_ref, wq_ref, x2_ref, qc_ref):
    oa = oa_ref[...].astype(BF16)
    x2 = x_ref[...] + _dot(oa, wa_ref[...]) + _dot(og_ref[...], wg_ref[...])
    x2_ref[...] = x2
    h = _rms(x2, g_ref[...]).astype(BF16)
    qc_ref[...] = (_dot(h, wq_ref[...]) * (MEM_HEAD_DIM ** -0.5)).astype(qc_ref.dtype)


def _out_cross_q(x, o_att, o_gla, w_out_att, w_out_gla, g_cross, w_q, *, tm=512):
    m, d = x.shape
    tm = min(tm, m)
    const = lambda shape: pl.BlockSpec(shape, lambda i: (0, 0))
    return pl.pallas_call(
        _out_cross_q_body,
        grid=(m // tm,),
        in_specs=[
            pl.BlockSpec((tm, d), lambda i: (i, 0)),
            pl.BlockSpec((tm, ATT_WIDTH), lambda i: (i, 0)),
            pl.BlockSpec((tm, GLA_VALUE_WIDTH), lambda i: (i, 0)),
            const(w_out_att.shape),
            const(w_out_gla.shape),
            const((1, d)),
            const(w_q.shape),
        ],
        out_specs=[pl.BlockSpec((tm, d), lambda i: (i, 0)), pl.BlockSpec((tm, MEM_WIDTH), lambda i: (i, 0))],
        out_shape=[jax.ShapeDtypeStruct((m, d), F32), jax.ShapeDtypeStruct((m, MEM_WIDTH), BF16)],
        compiler_params=_params("parallel"),
        name="out_cross_q",
    )(x, o_att, o_gla, w_out_att, w_out_gla, g_cross.reshape(1, d), w_q)


def _cross_heads(q, mk_ref, mv_ref):
    hd = MEM_HEAD_DIM
    heads = []
    for h in range(MEM_HEADS):
        s = _dot_nt(q[:, h * hd:(h + 1) * hd], mk_ref[0, h].astype(BF16))
        p = jnp.exp(s - jnp.max(s, axis=-1, keepdims=True))
        heads.append(_dot(p.astype(BF16), mv_ref[0, h].astype(BF16)) / jnp.sum(p, axis=-1, keepdims=True))
    return jnp.concatenate(heads, axis=1).astype(BF16)


def _cross_body(q_ref, mk_ref, mv_ref, x_ref, w_ref, o_ref, *, groups):
    rows = q_ref.shape[0]
    keys = groups * MEM_LEN
    hd = MEM_HEAD_DIM
    col = lax.broadcasted_iota(jnp.int32, (1, keys), 1)
    group_of_row = lax.broadcasted_iota(jnp.int32, (rows, 1), 0) // (rows // groups)
    own = (col // MEM_LEN) == group_of_row
    head_rows = lambda ref, h: jnp.concatenate(
        [ref[gi, pl.ds(h, MEM_LEN, stride=MEM_HEADS), :] for gi in range(groups)], axis=0).astype(BF16)
    heads = []
    for h in range(MEM_HEADS):
        s = jnp.where(own, _dot_nt(q_ref[:, h * hd:(h + 1) * hd], head_rows(mk_ref, h)), -jnp.inf)
        p = jnp.exp(s - jnp.max(s, axis=-1, keepdims=True))
        heads.append(_dot(p.astype(BF16), head_rows(mv_ref, h)) / jnp.sum(p, axis=-1, keepdims=True))
    o = jnp.concatenate(heads, axis=1).astype(BF16)
    o_ref[...] = x_ref[...] + _dot(o, w_ref[...])


def _post_mix_body(x_ref, oa_ref, og_ref, wa_ref, wg_ref, g_ref, wq_ref, mk_ref, mv_ref, wo_ref, o_ref):
    x2 = x_ref[...] + _dot(oa_ref[...].astype(BF16), wa_ref[...]) + _dot(og_ref[...], wg_ref[...])
    h = _rms(x2, g_ref[...]).astype(BF16)
    q = (_dot(h, wq_ref[...]) * (MEM_HEAD_DIM ** -0.5)).astype(BF16)
    o_ref[...] = x2 + _dot(_cross_heads(q, mk_ref, mv_ref), wo_ref[...])


def _post_mix(x, o_att, o_gla, w_out_att, w_out_gla, g_cross, w_q, mem_k, mem_v, w_o, *, seq, tm=512):
    m, d = x.shape
    tm = min(tm, seq)
    assert seq % tm == 0
    const = lambda a: pl.BlockSpec(a.shape, lambda i: (0,) * a.ndim, pipeline_mode=pl.Buffered(1))
    rows = lambda width: pl.BlockSpec((tm, width), lambda i: (i, 0))
    mem = pl.BlockSpec((1, MEM_HEADS, MEM_LEN, MEM_HEAD_DIM), lambda i: (i // (seq // tm), 0, 0, 0))
    g = g_cross.reshape(1, d)
    return pl.pallas_call(
        _post_mix_body,
        grid=(m // tm,),
        in_specs=[rows(d), rows(ATT_WIDTH), rows(GLA_VALUE_WIDTH), const(w_out_att), const(w_out_gla), const(g),
                  const(w_q), mem, mem, const(w_o)],
        out_specs=rows(d),
        out_shape=jax.ShapeDtypeStruct((m, d), F32),
        compiler_params=_params("parallel"),
        name="post_mix",
    )(x, o_att, o_gla, w_out_att, w_out_gla, g, w_q, mem_k, mem_v, w_o)


def _cross(qc, mem_k, mem_v, x, w_o, *, rows, groups):
    m, d = x.shape
    mem_k, mem_v = (a.reshape(a.shape[0], MEM_LEN * MEM_HEADS, MEM_HEAD_DIM) for a in (mem_k, mem_v))
    mem = pl.BlockSpec((groups, MEM_LEN * MEM_HEADS, MEM_HEAD_DIM), lambda i: (i, 0, 0))
    return pl.pallas_call(
        functools.partial(_cross_body, groups=groups),
        grid=(m // rows,),
        in_specs=[
            pl.BlockSpec((rows, MEM_WIDTH), lambda i: (i, 0)),
            mem,
            mem,
            pl.BlockSpec((rows, d), lambda i: (i, 0)),
            pl.BlockSpec(w_o.shape, lambda i: (0, 0)),
        ],
        out_specs=pl.BlockSpec((rows, d), lambda i: (i, 0)),
        out_shape=jax.ShapeDtypeStruct((m, d), F32),
        compiler_params=_params("parallel"),
        name="cross",
    )(qc, mem_k, mem_v, x, w_o)


def kernel(x_prompt, x_sample, mem_prompt, cache_win_k, cache_win_v, state_gla, cache_mem_k, cache_mem_v, g_ffn1, w1_gate, w1_up, w1_down, g_mix, w_in, w_gla_a2, b_gla_a, g_gla_out, w_out, g_mem, w_mem_k, w_mem_v, g_cross, w_cross_q, w_cross_o, g_ffn2, w2_gate, w2_up, w2_down, g_final):
    b, seq, d = x_prompt.shape
    n, n_new, _ = x_sample.shape
    depth = g_ffn1.shape[0]
    past_len = cache_win_k.shape[2]
    n_keep = min(MAX_WINDOW, seq)
    bf = lambda a: a.astype(BF16)

    xp = x_prompt.reshape(b * seq, d)
    xs = x_sample.reshape(n * n_new, d)
    rope_p = _rope_tables(jnp.arange(seq))
    rope_s = _rope_tables(jnp.tile(past_len + jnp.arange(n_new), n))
    outs = [[] for _ in range(8)]
    cols = np.cumsum([0, ATT_WIDTH, ATT_WIDTH, ATT_WIDTH, GLA_KEY_WIDTH, GLA_KEY_WIDTH, GLA_VALUE_WIDTH, GLA_GATE_RANK])
    for l in range(depth):
        w_main = bf(jnp.concatenate([w_in[l][:, :cols[6]], w_in[l][:, cols[7]:]], axis=1))
        w_lr = bf(jnp.pad(w_in[l][:, cols[6]:cols[7]], ((0, 0), (0, LANES - GLA_GATE_RANK))))
        w_a2 = bf(jnp.pad(w_gla_a2[l], ((0, LANES - GLA_GATE_RANK), (0, 0))))
        w_out_att, w_out_gla = bf(w_out[l][:ATT_WIDTH]), bf(w_out[l][ATT_WIDTH:])
        w_mem = bf(jnp.concatenate([w_mem_k[l], w_mem_v[l]], axis=1))
        w_cq, w_co = bf(w_cross_q[l]), bf(w_cross_o[l])
        gf = g_final if l == depth - 1 else None
        mix_in = lambda x, rope: _mix_in(x, g_mix[l], w_main, w_lr, w_a2, b_gla_a[l], rope)

        w2 = (bf(w2_gate[l]), bf(w2_up[l]), bf(w2_down[l]))
        xs, *w1 = _ffn(xs, g_ffn1[l], w1_gate[l], w1_up[l], w1_down[l])
        zs, la_s = mix_in(xs, rope_s)
        weights_rider, values_rider, attn_finish = _attn_sample_riders(
            zs.reshape(n, n_new, Z_COLS), cache_win_k[l], cache_win_v[l])

        xp, att_w, att_aux = _ffn(xp, g_ffn1[l], *w1, rider=weights_rider)
        mem_kv = _norm_proj(mem_prompt.reshape(b * MEM_LEN, d), g_mem[l], w_mem).reshape(b, MEM_LEN, 2 * MEM_WIDTH)
        mem_k = mem_kv[:, :, :MEM_WIDTH].reshape(b, MEM_LEN, MEM_HEADS, MEM_HEAD_DIM)
        mem_v = mem_kv[:, :, MEM_WIDTH:].reshape(b, MEM_LEN, MEM_HEADS, MEM_HEAD_DIM)
        zp, la_p = mix_in(xp, rope_p)
        zp3 = zp.reshape(b, seq, Z_COLS)
        o_att_p = _attn_prompt(zp3).reshape(b * seq, ATT_WIDTH)
        o_gla_p, st_p = _gla_prompt(zp3, la_p.reshape(b, seq, GLA_KEY_WIDTH), g_gla_out[l])
        xp = _post_mix(xp, o_att_p, o_gla_p.reshape(b * seq, GLA_VALUE_WIDTH), w_out_att, w_out_gla, g_cross[l], w_cq,
                       mem_k.transpose(0, 2, 1, 3), mem_v.transpose(0, 2, 1, 3), w_co, seq=seq)
        xp, o_att_s = _ffn(xp, g_ffn2[l], *w2, g_final=gf, rider=values_rider(att_w, att_aux))

        o_att_s = attn_finish(o_att_s).reshape(n * n_new, ATT_WIDTH)
        o_gla_s, st_s = _gla_sample(zs, la_s, g_gla_out[l], state_gla[l], n_new)
        xs, qc = _out_cross_q(xs, o_att_s, o_gla_s, w_out_att, w_out_gla, g_cross[l], w_cq)
        groups = min(SAMPLE_GROUP, n)
        xs = _cross(qc, cache_mem_k[l], cache_mem_v[l], xs, w_co, rows=groups * n_new, groups=groups)
        xs = _ffn(xs, g_ffn2[l], *w2, g_final=gf)

        heads = lambda a, t: a.reshape(-1, t, ATT_HEADS, ATT_HEAD_DIM)
        outs[0].append(heads(zp3[:, seq - n_keep:, ATT_WIDTH:2 * ATT_WIDTH], n_keep))
        outs[1].append(heads(zp3[:, seq - n_keep:, 2 * ATT_WIDTH:3 * ATT_WIDTH], n_keep))
        outs[2].append(st_p)
        outs[3].append(mem_k)
        outs[4].append(mem_v)
        outs[5].append(heads(zs[:, ATT_WIDTH:2 * ATT_WIDTH], n_new))
        outs[6].append(heads(zs[:, 2 * ATT_WIDTH:3 * ATT_WIDTH], n_new))
        outs[7].append(st_s)

    return (xp.reshape(b, seq, d), xs.reshape(n, n_new, d)) + tuple(jnp.stack(o) for o in outs)
```

```python
import functools

import jax
import jax.numpy as jnp
import numpy as np
from jax import lax
from jax.experimental import pallas as pl
from jax.experimental.pallas import tpu as pltpu

F32 = jnp.float32
BF16 = jnp.bfloat16

EPS = 1e-6
ATT_HEADS = 16
ATT_HEAD_DIM = 64
ATT_WIDTH = ATT_HEADS * ATT_HEAD_DIM
DILATIONS = (1, 4, 16)
N_BACK = 128
BAND = 128
ROT_DIM = 16
ROPE_THETA = 500000.0
GLA_HEADS = 4
GLA_HEAD_K = 128
GLA_HEAD_V = 256
GLA_KEY_WIDTH = GLA_HEADS * GLA_HEAD_K
GLA_VALUE_WIDTH = GLA_HEADS * GLA_HEAD_V
GLA_GATE_RANK = 16
GLA_GATE_TAU = 16.0
GLA_CHUNK = 64
MEM_LEN = 256
MEM_HEADS = 4
MEM_HEAD_DIM = 128
MEM_WIDTH = MEM_HEADS * MEM_HEAD_DIM
MAX_WINDOW = 2048
LANES = 128
SAMPLE_GROUP = 8
Z_COLS = 3 * ATT_WIDTH + 2 * GLA_KEY_WIDTH + 2 * GLA_VALUE_WIDTH
Z_TILE = 2 * ATT_WIDTH
VMEM_LIMIT_BYTES = 56 * 1024 * 1024


def _params(*semantics, vmem_limit_bytes=VMEM_LIMIT_BYTES):
    return pltpu.CompilerParams(dimension_semantics=semantics, vmem_limit_bytes=vmem_limit_bytes)


def _rms(x, g):
    return x * lax.rsqrt(jnp.mean(x * x, axis=-1, keepdims=True) + EPS) * g


def _dot(a, b):
    return jnp.dot(a, b, preferred_element_type=F32)


def _dot_nt(a, b):
    return lax.dot_general(a, b, (((1,), (1,)), ((), ())), preferred_element_type=F32)


def _dot_tn(a, b):
    return lax.dot_general(a, b, (((0,), (0,)), ((), ())), preferred_element_type=F32)


def _split_bf16(x):
    hi = x.astype(BF16)
    lo = (x - hi.astype(F32)).astype(BF16)
    return hi, lo


RIDER_HOST_ROWS = 1024
RIDER_VMEM_LIMIT_BYTES = 62 * 1024 * 1024


def _ffn_body(*refs, nf, final, emit, rider):
    x_ref, g_ref, wg_ref, wu_ref, wd_ref = refs[:5]
    refs = refs[5:]
    if final:
        gf_ref, refs = refs[0], refs[1:]
    if rider is not None:
        rider_in, refs = refs[:len(rider.args)], refs[len(rider.args):]
        rider_out = refs[-1 - len(rider.out_specs):-1]
    o_ref, h_ref = refs[0], refs[-1]
    f = pl.program_id(1)

    @pl.when(f == 0)
    def _():
        h_ref[...] = _rms(x_ref[...], g_ref[...]).astype(BF16)
        o_ref[...] = jnp.zeros_like(o_ref)

    def step(with_rider):
        wg, wu, wd = wg_ref[...].astype(BF16), wu_ref[...].astype(BF16), wd_ref[...].astype(BF16)
        if emit:
            for w_out_ref, w in zip(refs[1:4], (wg, wu, wd)):
                w_out_ref[...] = w
        h = h_ref[...]
        gate = _dot(h, wg)
        if with_rider:
            carried = rider.first(*rider_in)
        up = _dot(h, wu)
        if with_rider:
            rider.second(*rider_in, *rider_out, carried)
        act = (gate * jax.nn.sigmoid(gate) * up).astype(BF16)
        o_ref[...] += _dot(act, wd)

    if rider is None:
        step(False)
    else:
        active = rider.active(f)
        pl.when(active)(functools.partial(step, True))
        pl.when(jnp.logical_not(active))(functools.partial(step, False))

    @pl.when(f == nf - 1)
    def _():
        y = x_ref[...] + 0.5 * o_ref[...]
        if final:
            y = _rms(y, gf_ref[...])
        o_ref[...] = y


def _ffn(x, g, wg, wu, wd, g_final=None, *, tm=512, tf=None, rider=None):
    m, d = x.shape
    dff = wg.shape[1]
    if rider is not None:
        tm = RIDER_HOST_ROWS
    tm = min(tm, m)
    emit = wg.dtype != BF16
    if tf is None:
        tf = 256 if emit or rider is not None else 512
    nf = dff // tf
    assert dff % tf == 0 and m % tm == 0
    assert not emit or m == tm
    final = g_final is not None
    row = pl.BlockSpec((1, d), lambda i, f: (0, 0))
    w_specs = [
        pl.BlockSpec((d, tf), lambda i, f: (0, f)),
        pl.BlockSpec((d, tf), lambda i, f: (0, f)),
        pl.BlockSpec((tf, d), lambda i, f: (f, 0)),
    ]
    x_mode = {} if rider is None else dict(pipeline_mode=pl.Buffered(1))
    in_specs = [pl.BlockSpec((tm, d), lambda i, f: (i, 0), **x_mode), row] + w_specs
    args = [x, g.reshape(1, d), wg, wu, wd]
    if final:
        in_specs.append(row)
        args.append(g_final.reshape(1, d))
    out_specs = [pl.BlockSpec((tm, d), lambda i, f: (i, 0))]
    out_shape = [jax.ShapeDtypeStruct((m, d), F32)]
    if emit:
        out_specs += w_specs
        out_shape += [jax.ShapeDtypeStruct(w.shape, BF16) for w in (wg, wu, wd)]
    if rider is not None:
        rider = rider(m // tm, nf)
        in_specs += rider.in_specs
        args += rider.args
        out_specs += rider.out_specs
        out_shape += rider.out_shape
    out = pl.pallas_call(
        functools.partial(_ffn_body, nf=nf, final=final, emit=emit, rider=rider),
        grid=(m // tm, nf),
        in_specs=in_specs,
        out_specs=out_specs,
        out_shape=out_shape,
        scratch_shapes=[pltpu.VMEM((tm, d), BF16)],
        compiler_params=(_params("parallel", "arbitrary") if rider is None else
                         _params("arbitrary", "arbitrary", vmem_limit_bytes=RIDER_VMEM_LIMIT_BYTES)),
        name=("ffn_final" if final else "ffn") + ("_f32w" if emit else "") + ("" if rider is None else "_" + rider.name),
    )(*args)
    return out if len(out) > 1 else out[0]


def _mix_in_body(x_ref, g_ref, w_ref, wlr_ref, wa2_ref, ba_ref, rope_ref, z_ref, la_ref, h_ref):
    n = pl.program_id(1)

    @pl.when(n == 0)
    def _():
        h = _rms(x_ref[...], g_ref[...]).astype(BF16)
        h_ref[...] = h
        g_lr = _dot(h, wlr_ref[...])
        pre = _dot(g_lr.astype(BF16), wa2_ref[...]) + ba_ref[...]
        la_ref[...] = (jnp.minimum(pre, 0.0) - jnp.log1p(jnp.exp(-jnp.abs(pre)))) * (1.0 / GLA_GATE_TAU)

    z = _dot(h_ref[...], w_ref[...])

    @pl.when(n > 0)
    def _():
        z_ref[...] = z

    @pl.when(n == 0)
    def _():
        cos, sin_up, sin_dn = rope_ref[0], rope_ref[1], rope_ref[2]
        half = ROT_DIM // 2
        for j in range(Z_TILE // LANES):
            zc = z[:, j * LANES:(j + 1) * LANES]
            rot = zc * cos + pltpu.roll(zc, half, axis=1) * sin_up + pltpu.roll(zc, LANES - half, axis=1) * sin_dn
            z_ref[:, j * LANES:(j + 1) * LANES] = rot * (ATT_HEAD_DIM ** -0.5) if j * LANES < ATT_WIDTH else rot


def _mix_in(x, g, w_main, w_lr, w_a2, b_a, rope, *, tm=512):
    m, d = x.shape
    tm = min(tm, m)
    n_rope = rope.shape[1] // tm
    return pl.pallas_call(
        _mix_in_body,
        grid=(m // tm, Z_COLS // Z_TILE),
        in_specs=[
            pl.BlockSpec((tm, d), lambda i, n: (i, 0)),
            pl.BlockSpec((1, d), lambda i, n: (0, 0)),
            pl.BlockSpec((d, Z_TILE), lambda i, n: (0, n)),
            pl.BlockSpec((d, LANES), lambda i, n: (0, 0)),
            pl.BlockSpec((LANES, GLA_KEY_WIDTH), lambda i, n: (0, 0)),
            pl.BlockSpec((1, GLA_KEY_WIDTH), lambda i, n: (0, 0)),
            pl.BlockSpec((3, tm, LANES), lambda i, n: (0, i % n_rope, 0)),
        ],
        out_specs=[
            pl.BlockSpec((tm, Z_TILE), lambda i, n: (i, n)),
            pl.BlockSpec((tm, GLA_KEY_WIDTH), lambda i, n: (i, 0)),
        ],
        out_shape=[
            jax.ShapeDtypeStruct((m, Z_COLS), F32),
            jax.ShapeDtypeStruct((m, GLA_KEY_WIDTH), F32),
        ],
        scratch_shapes=[pltpu.VMEM((tm, d), BF16)],
        compiler_params=_params("parallel", "arbitrary"),
        name="mix_in",
    )(x, g.reshape(1, d), w_main, w_lr, w_a2, b_a.reshape(1, -1), rope)


def _rope_tables(pos):
    half = ROT_DIM // 2
    inv = ROPE_THETA ** (-jnp.arange(half, dtype=F32) * 2.0 / ROT_DIM)
    ang = pos.astype(F32)[:, None] * inv[None, :]
    cos, sin = jnp.cos(ang), jnp.sin(ang)
    t = pos.shape[0]
    zeros = jnp.zeros((t, half), F32)
    rest0 = jnp.zeros((t, ATT_HEAD_DIM - ROT_DIM), F32)
    c = jnp.concatenate([cos, cos, jnp.ones_like(rest0)], axis=1)
    s_up = jnp.concatenate([zeros, sin, rest0], axis=1)
    s_dn = jnp.concatenate([-sin, zeros, rest0], axis=1)
    reps = LANES // ATT_HEAD_DIM
    return jnp.stack([jnp.tile(c, (1, reps)), jnp.tile(s_up, (1, reps)), jnp.tile(s_dn, (1, reps))])


def _norm_proj_body(x_ref, g_ref, w_ref, o_ref):
    h = _rms(x_ref[...], g_ref[...]).astype(BF16)
    o_ref[...] = _dot(h, w_ref[...])


def _norm_proj(x, g, w, *, tm=512):
    m, d = x.shape
    n = w.shape[1]
    tm = min(tm, m)
    return pl.pallas_call(
        _norm_proj_body,
        grid=(m // tm,),
        in_specs=[
            pl.BlockSpec((tm, d), lambda i: (i, 0)),
            pl.BlockSpec((1, d), lambda i: (0, 0)),
            pl.BlockSpec((d, n), lambda i: (0, 0)),
        ],
        out_specs=pl.BlockSpec((tm, n), lambda i: (i, 0)),
        out_shape=jax.ShapeDtypeStruct((m, n), F32),
        compiler_params=_params("parallel"),
        name="memory_kv",
    )(x, g.reshape(1, d), w)


ATTN_UNROLL = 32
MERGE_ROWS = 256


def _band_bias():
    i = np.arange(BAND)[:, None]
    j = np.arange(2 * BAND)[None, :]
    out = []
    for off in (0, BAND):
        dist = off + i - j
        out.append(np.where((dist >= 0) & (dist <= N_BACK), 0.0, -np.inf))
    return jnp.asarray(np.stack(out), F32)


def _attn_prompt_body(q_ref, k_ref, v_ref, bias_ref, o_ref, acc_ref, m_ref, l_ref, *, seq):
    hd = ATT_HEAD_DIM
    assert LANES == 2 * hd
    blocks = seq // BAND
    first_head = lax.broadcasted_iota(jnp.int32, (1, LANES), 1) < hd
    for ci, dil in enumerate(DILATIONS):
        nb = blocks // dil
        shift = nb.bit_length() - 1
        assert nb == 1 << shift

        def step(it, carry, ci=ci, dil=dil, nb=nb, shift=shift):
            loaded = []
            for u in range(ATTN_UNROLL):
                w = it * ATTN_UNROLL + u
                r = lax.shift_right_logical(w, shift)
                n = w & (nb - 1)
                base = jnp.maximum(n - 1, 0) * BAND
                if dil == 1:
                    q_rows = pl.ds(pl.multiple_of(n * BAND, BAND), BAND)
                    k_rows = pl.ds(pl.multiple_of(base, BAND), 2 * BAND)
                else:
                    q_rows = pl.ds(r + dil * BAND * n, BAND, stride=dil)
                    k_rows = pl.ds(r + dil * base, 2 * BAND, stride=dil)
                bias = bias_ref[jnp.minimum(n, 1)]
                loaded.append((q_rows, q_ref[0, q_rows, :], k_ref[0, k_rows, :], v_ref[0, k_rows, :], bias))
            results = []
            for q_rows, q, k, v, bias in loaded:
                kb = k.astype(BF16)
                qs = (jnp.where(first_head, q, 0.0), jnp.where(first_head, 0.0, q))
                vs = (jnp.where(first_head, v, 1.0), jnp.where(first_head, 1.0, v))
                out, mx = [], []
                for hh in range(LANES // hd):
                    s = _dot_nt(qs[hh].astype(BF16), kb) + bias
                    m_c = jnp.max(s, axis=1, keepdims=True)
                    p = jnp.exp(s - m_c)
                    out.append(_dot(p.astype(BF16), vs[hh].astype(BF16)))
                    mx.append(m_c)
                results.append((q_rows, jnp.where(first_head, out[0], out[1]),
                                jnp.where(first_head, mx[0], mx[1]), jnp.where(first_head, out[1], out[0])))
            for q_rows, pv, mx, den_swapped in results:
                acc_ref[ci, q_rows, :] = pv
                m_ref[ci, q_rows, :] = mx
                l_ref[ci, q_rows, :] = den_swapped
            return carry

        lax.fori_loop(0, blocks // ATTN_UNROLL, step, 0)

    def merge(i, carry):
        rows = pl.ds(pl.multiple_of(i * MERGE_ROWS, MERGE_ROWS), MERGE_ROWS)
        ms = [m_ref[ci, rows, :] for ci in range(len(DILATIONS))]
        m_all = functools.reduce(jnp.maximum, ms)
        ws = [jnp.exp(m - m_all) for m in ms]
        num = sum(w * acc_ref[ci, rows, :] for ci, w in enumerate(ws))
        den = sum(w * pltpu.roll(l_ref[ci, rows, :], hd, axis=1) for ci, w in enumerate(ws))
        o_ref[0, rows, :] = (num / den).astype(o_ref.dtype)
        return carry

    lax.fori_loop(0, seq // MERGE_ROWS, merge, 0)


def _attn_prompt(z3):
    b, seq, _ = z3.shape
    pairs = ATT_WIDTH // LANES
    assert seq % (DILATIONS[-1] * BAND) == 0 and seq // DILATIONS[-1] >= 2 * BAND
    assert (seq // BAND) % ATTN_UNROLL == 0 and seq % MERGE_ROWS == 0
    blk = lambda off: pl.BlockSpec((1, seq, LANES), lambda i, p: (i, 0, off + p))
    return pl.pallas_call(
        functools.partial(_attn_prompt_body, seq=seq),
        grid=(b, pairs),
        in_specs=[blk(0), blk(pairs), blk(2 * pairs),
                  pl.BlockSpec((2, BAND, 2 * BAND), lambda i, p: (0, 0, 0))],
        out_specs=pl.BlockSpec((1, seq, LANES), lambda i, p: (i, 0, p)),
        out_shape=jax.ShapeDtypeStruct((b, seq, ATT_WIDTH), BF16),
        scratch_shapes=[pltpu.VMEM((len(DILATIONS), seq, LANES), F32)] * 3,
        compiler_params=_params("parallel", "parallel"),
        name="attn_prompt",
    )(z3, z3, z3, _band_bias())


SAMPLE_ROWS = 8


def _sample_multiplicity(n_buf, n_new):
    t = (np.arange(SAMPLE_ROWS) % n_new)[:, None]
    r = np.arange(n_buf)[None, :]
    mult = np.zeros((SAMPLE_ROWS, n_buf), np.float32)
    for dil in DILATIONS:
        back = n_buf + t - r
        mult += (back % dil == 0) & (back >= dil) & (back <= dil * N_BACK)
    return jnp.asarray(mult, F32)


def _new_row_multiplicity(n_new):
    t_row = lax.broadcasted_iota(jnp.int32, (SAMPLE_ROWS, 1), 0) % n_new
    return [(t_row >= t).astype(F32) + (len(DILATIONS) - 1.0) * (t_row == t).astype(F32) for t in range(n_new)]


def _attn_scores(q_ref, kn_ref, kt_ref, mult_ref, *, n_new):
    live = mult_ref[...] > 0.0
    mult_new = _new_row_multiplicity(n_new)
    scores = []
    for h in range(ATT_HEADS):
        q = q_ref[0, h]
        kn = kn_ref[0, h]
        s = jnp.where(live, _dot(q.astype(BF16), kt_ref[0, h].astype(BF16)), -jnp.inf)
        s_new = [jnp.where(mult_new[t] > 0.0, jnp.sum(q * kn[t:t + 1], axis=1, keepdims=True), -jnp.inf)
                 for t in range(n_new)]
        scores.append((s, s_new))
    return scores


def _attn_weights(q_ref, kn_ref, kt_ref, mult_ref, w_ref, aux_ref, scores, *, n_new):
    mult = mult_ref[...]
    mult_new = _new_row_multiplicity(n_new)
    lane = lax.broadcasted_iota(jnp.int32, (1, LANES), 1)
    for h, (s, s_new) in enumerate(scores):
        m = functools.reduce(jnp.maximum, s_new, jnp.max(s, axis=1, keepdims=True))
        w = mult * jnp.exp(s - m)
        w_new = [mult_new[t] * jnp.exp(s_new[t] - m) for t in range(n_new)]
        den = jnp.sum(w, axis=1, keepdims=True) + sum(w_new)
        w_ref[0, h] = w
        aux_ref[0, h] = sum(jnp.where(lane == t, w_new[t], 0.0) for t in range(n_new)) + jnp.where(lane == n_new, den, 0.0)


def _attn_value_products(w_ref, aux_ref, vn_ref, vt_ref):
    return [_dot_nt(w_ref[0, h].astype(BF16), vt_ref[0, h].astype(BF16)) for h in range(ATT_HEADS)]


def _attn_outputs(w_ref, aux_ref, vn_ref, vt_ref, o_ref, products, *, n_new):
    for h, o in enumerate(products):
        aux = aux_ref[0, h]
        vn = vn_ref[0, h]
        o = o + sum(aux[:, t:t + 1] * vn[t:t + 1] for t in range(n_new))
        o_ref[0, h] = o / aux[:, n_new:n_new + 1]


class _Rider:
    def __init__(self, name, in_specs, args, out_specs, out_shape, first, second, active):
        self.name, self.in_specs, self.args, self.out_specs, self.out_shape = name, in_specs, args, out_specs, out_shape
        self.first, self.second = first, second
        self.active = active


def _attn_sample_riders(z3, cache_k, cache_v):
    n, n_new, _ = z3.shape
    n_buf = cache_k.shape[1]
    nh, hd = ATT_HEADS, ATT_HEAD_DIM
    assert n_buf >= DILATIONS[-1] * N_BACK and n_new < SAMPLE_ROWS and n_new < LANES

    def per_head(col):
        a = z3[:, :, col * ATT_WIDTH:(col + 1) * ATT_WIDTH].reshape(n, n_new, nh, hd).transpose(0, 2, 1, 3)
        return jnp.pad(a, ((0, 0), (0, 0), (0, SAMPLE_ROWS - n_new), (0, 0)))

    rows_on_lanes = lambda c: c.transpose(0, 2, 3, 1)

    def specs(tiles, steps):
        per_tile = n // tiles
        assert per_tile <= steps and per_tile * tiles == n
        where = lambda i, f: (i * per_tile + jnp.minimum(f, per_tile - 1), 0, 0, 0)
        block = lambda minor: pl.BlockSpec((1, nh) + minor, where)
        active = lambda f: f < per_tile
        return (block((SAMPLE_ROWS, hd)), block((hd, n_buf)), block((SAMPLE_ROWS, n_buf)), block((SAMPLE_ROWS, LANES)),
                active)

    def weights_rider(tiles, steps):
        small, cache, weights, aux, active = specs(tiles, steps)
        return _Rider(
            "attn_weights",
            [small, small, cache, pl.BlockSpec((SAMPLE_ROWS, n_buf), lambda i, f: (0, 0))],
            [per_head(0), per_head(1), rows_on_lanes(cache_k), _sample_multiplicity(n_buf, n_new)],
            [weights, aux],
            [jax.ShapeDtypeStruct((n, nh, SAMPLE_ROWS, n_buf), F32), jax.ShapeDtypeStruct((n, nh, SAMPLE_ROWS, LANES), F32)],
            functools.partial(_attn_scores, n_new=n_new),
            functools.partial(_attn_weights, n_new=n_new),
            active,
        )

    def values_rider(w, aux_w):
        def make(tiles, steps):
            small, cache, weights, aux, active = specs(tiles, steps)
            return _Rider(
                "attn_values",
                [weights, aux, small, cache],
                [w, aux_w, per_head(2), rows_on_lanes(cache_v)],
                [small],
                [jax.ShapeDtypeStruct((n, nh, SAMPLE_ROWS, hd), F32)],
                _attn_value_products,
                functools.partial(_attn_outputs, n_new=n_new),
                active,
            )
        return make

    finish = lambda o: o[:, :, :n_new].transpose(0, 2, 1, 3).reshape(n, n_new, ATT_WIDTH)
    return weights_rider, values_rider, finish


def _gla_rows(q, k, v, li, states, causal, same, group_of_row):
    c = q.shape[0]
    hi, lo = _split_bf16(li)
    tri = causal.astype(BF16)
    b = _dot(tri, hi) + _dot(tri, lo)
    if len(states) == 1:
        b_end = b[c - 1:c, :]
    else:
        full = same.astype(BF16)
        b_end = _dot(full, hi) + _dot(full, lo)
    q_t = (q * (GLA_HEAD_K ** -0.5) * jnp.exp(b)).astype(BF16)
    k_n = (k * jnp.exp(-b)).astype(BF16)
    k_w = k * jnp.exp(b_end - b)
    vb = v.astype(BF16)
    a = jnp.where(causal, _dot_nt(q_t, k_n), 0.0)
    o = _dot(a.astype(BF16), vb)
    ones = jnp.ones((c, LANES), BF16)
    new_states = []
    for gi, s in enumerate(states):
        if len(states) == 1:
            o = o + _dot(q_t, s.astype(BF16))
            kw_g, hi_g, lo_g = k_w.astype(BF16), hi, lo
        else:
            mine = group_of_row == gi
            o = o + jnp.where(mine, _dot(q_t, s.astype(BF16)), 0.0)
            kw_g = jnp.where(mine, k_w, 0.0).astype(BF16)
            hi_g = jnp.where(mine, hi, jnp.zeros_like(hi))
            lo_g = jnp.where(mine, lo, jnp.zeros_like(lo))
        decay = jnp.exp(_dot_tn(hi_g, ones) + _dot_tn(lo_g, ones))
        decay = jnp.concatenate([decay] * (s.shape[1] // LANES), axis=1)
        new_states.append(decay * s + _dot_tn(kw_g, vb))
    return o, new_states


def _gla_finish(o, g_out, rg):
    o = o * lax.rsqrt(jnp.mean(o * o, axis=-1, keepdims=True) + EPS)
    return (o * g_out) * (rg * jax.nn.sigmoid(rg))


GLA_HEADS_PER_STEP = 4
GLA_SEQ_TILE = 1024
GLA_CHUNK_UNROLL = 4
GLA_SAMPLE_HEADS_PER_STEP = 2


def _gla_prompt_body(q_ref, k_ref, v_ref, rg_ref, la_ref, g_ref, o_ref, s_ref, *, rows, chunk, heads):
    hk, hv = GLA_HEAD_K, GLA_HEAD_V
    r = lax.broadcasted_iota(jnp.int32, (chunk, chunk), 0)
    c = lax.broadcasted_iota(jnp.int32, (chunk, chunk), 1)
    causal = c <= r

    @pl.when(pl.program_id(2) == 0)
    def _():
        s_ref[...] = jnp.zeros_like(s_ref)

    def step(i, carry):
        rr = pl.ds(pl.multiple_of(i * chunk, chunk), chunk)
        for hh in range(heads):
            ks = slice(hh * hk, (hh + 1) * hk)
            vs = slice(hh * hv, (hh + 1) * hv)
            o, (s_new,) = _gla_rows(q_ref[0, rr, ks], k_ref[0, rr, ks], v_ref[0, rr, vs], la_ref[0, rr, ks],
                                    [s_ref[0, hh]], causal, None, None)
            s_ref[0, hh] = s_new
            o_ref[0, rr, vs] = _gla_finish(o, g_ref[:, vs], rg_ref[0, rr, vs]).astype(o_ref.dtype)
        return carry

    lax.fori_loop(0, rows // chunk, step, 0, unroll=GLA_CHUNK_UNROLL)


def _gla_prompt(z3, la3, g_out):
    b, seq, _ = z3.shape
    hk, hv, nh = GLA_HEAD_K, GLA_HEAD_V, GLA_HEADS
    chunk = int(np.gcd(seq, GLA_CHUNK))
    hs = GLA_HEADS_PER_STEP
    rows = min(GLA_SEQ_TILE, seq)
    assert seq % rows == 0 and (rows // chunk) % GLA_CHUNK_UNROLL == 0 and nh % hs == 0
    q0 = 3 * ATT_WIDTH // (hs * hk)
    k0 = q0 + nh // hs
    v0 = (3 * ATT_WIDTH + 2 * GLA_KEY_WIDTH) // (hs * hv)
    r0 = v0 + nh // hs
    return pl.pallas_call(
        functools.partial(_gla_prompt_body, rows=rows, chunk=chunk, heads=hs),
        grid=(b, nh // hs, seq // rows),
        in_specs=[
            pl.BlockSpec((1, rows, hs * hk), lambda i, h, t: (i, t, q0 + h)),
            pl.BlockSpec((1, rows, hs * hk), lambda i, h, t: (i, t, k0 + h)),
            pl.BlockSpec((1, rows, hs * hv), lambda i, h, t: (i, t, v0 + h)),
            pl.BlockSpec((1, rows, hs * hv), lambda i, h, t: (i, t, r0 + h)),
            pl.BlockSpec((1, rows, hs * hk), lambda i, h, t: (i, t, h)),
            pl.BlockSpec((1, hs * hv), lambda i, h, t: (0, h)),
        ],
        out_specs=[
            pl.BlockSpec((1, rows, hs * hv), lambda i, h, t: (i, t, h)),
            pl.BlockSpec((1, hs, hk, hv), lambda i, h, t: (i, h, 0, 0)),
        ],
        out_shape=[
            jax.ShapeDtypeStruct((b, seq, GLA_VALUE_WIDTH), BF16),
            jax.ShapeDtypeStruct((b, nh, hk, hv), F32),
        ],
        compiler_params=_params("parallel", "parallel", "arbitrary"),
        name="gla_prompt",
    )(z3, z3, z3, z3, la3, g_out.reshape(1, -1))


def _gla_sample_body(q_ref, k_ref, v_ref, rg_ref, la_ref, g_ref, s0_ref, o_ref, s_ref, *, n_new, groups, heads):
    hk, hv = GLA_HEAD_K, GLA_HEAD_V
    rows = n_new * groups
    r = lax.broadcasted_iota(jnp.int32, (rows, rows), 0)
    c = lax.broadcasted_iota(jnp.int32, (rows, rows), 1)
    same = (r // n_new) == (c // n_new)
    causal = same & (c <= r)
    group_of_row = lax.broadcasted_iota(jnp.int32, (rows, 1), 0) // n_new
    for hh in range(heads):
        ks = slice(hh * hk, (hh + 1) * hk)
        vs = slice(hh * hv, (hh + 1) * hv)
        o, new_states = _gla_rows(q_ref[:, ks], k_ref[:, ks], v_ref[:, vs], la_ref[:, ks],
                                  [s0_ref[gi, hh] for gi in range(groups)], causal, same, group_of_row)
        for gi in range(groups):
            s_ref[gi, hh] = new_states[gi]
        o_ref[:, vs] = _gla_finish(o, g_ref[:, vs], rg_ref[:, vs]).astype(o_ref.dtype)


def _gla_sample(z, la, g_out, state, n_new):
    m = z.shape[0]
    n = m // n_new
    hk, hv, nh = GLA_HEAD_K, GLA_HEAD_V, GLA_HEADS
    groups = min(SAMPLE_GROUP, n)
    rows = groups * n_new
    hs = GLA_SAMPLE_HEADS_PER_STEP
    q0 = 3 * ATT_WIDTH // (hs * hk)
    k0 = q0 + nh // hs
    v0 = (3 * ATT_WIDTH + 2 * GLA_KEY_WIDTH) // (hs * hv)
    r0 = v0 + nh // hs
    state_spec = pl.BlockSpec((groups, hs, hk, hv), lambda i, h: (i, h, 0, 0))
    return pl.pallas_call(
        functools.partial(_gla_sample_body, n_new=n_new, groups=groups, heads=hs),
        grid=(n // groups, nh // hs),
        in_specs=[
            pl.BlockSpec((rows, hs * hk), lambda i, h: (i, q0 + h)),
            pl.BlockSpec((rows, hs * hk), lambda i, h: (i, k0 + h)),
            pl.BlockSpec((rows, hs * hv), lambda i, h: (i, v0 + h)),
            pl.BlockSpec((rows, hs * hv), lambda i, h: (i, r0 + h)),
            pl.BlockSpec((rows, hs * hk), lambda i, h: (i, h)),
            pl.BlockSpec((1, hs * hv), lambda i, h: (0, h)),
            state_spec,
        ],
        out_specs=[pl.BlockSpec((rows, hs * hv), lambda i, h: (i, h)), state_spec],
        out_shape=[
            jax.ShapeDtypeStruct((m, GLA_VALUE_WIDTH), BF16),
            jax.ShapeDtypeStruct(state.shape, F32),
        ],
        compiler_params=_params("parallel", "parallel"),
        name="gla_sample",
    )(z, z, z, z, la, g_out.reshape(1, -1), state)


def _out_cross_q_body(x_ref, oa_ref, og_ref, wa_ref, wg_ref, g_ref, wq_ref, x2_ref, qc_ref):
    oa = oa_ref[...].astype(BF16)
    x2 = x_ref[...] + _dot(oa, wa_ref[...]) + _dot(og_ref[...], wg_ref[...])
    x2_ref[...] = x2
    h = _rms(x2, g_ref[...]).astype(BF16)
    qc_ref[...] = (_dot(h, wq_ref[...]) * (MEM_HEAD_DIM ** -0.5)).astype(qc_ref.dtype)


def _out_cross_q(x, o_att, o_gla, w_out_att, w_out_gla, g_cross, w_q, *, tm=512):
    m, d = x.shape
    tm = min(tm, m)
    const = lambda shape: pl.BlockSpec(shape, lambda i: (0, 0))
    return pl.pallas_call(
        _out_cross_q_body,
        grid=(m // tm,),
        in_specs=[
            pl.BlockSpec((tm, d), lambda i: (i, 0)),
            pl.BlockSpec((tm, ATT_WIDTH), lambda i: (i, 0)),
            pl.BlockSpec((tm, GLA_VALUE_WIDTH), lambda i: (i, 0)),
            const(w_out_att.shape),
            const(w_out_gla.shape),
            const((1, d)),
            const(w_q.shape),
        ],
        out_specs=[pl.BlockSpec((tm, d), lambda i: (i, 0)), pl.BlockSpec((tm, MEM_WIDTH), lambda i: (i, 0))],
        out_shape=[jax.ShapeDtypeStruct((m, d), F32), jax.ShapeDtypeStruct((m, MEM_WIDTH), BF16)],
        compiler_params=_params("parallel"),
        name="out_cross_q",
    )(x, o_att, o_gla, w_out_att, w_out_gla, g_cross.reshape(1, d), w_q)


def _cross_heads(q, mk_ref, mv_ref):
    hd = MEM_HEAD_DIM
    heads = []
    for h in range(MEM_HEADS):
        s = _dot_nt(q[:, h * hd:(h + 1) * hd], mk_ref[0, h].astype(BF16))
        p = jnp.exp(s - jnp.max(s, axis=-1, keepdims=True))
        heads.append(_dot(p.astype(BF16), mv_ref[0, h].astype(BF16)) / jnp.sum(p, axis=-1, keepdims=True))
    return jnp.concatenate(heads, axis=1).astype(BF16)


def _cross_body(q_ref, mk_ref, mv_ref, x_ref, w_ref, o_ref, *, groups):
    rows = q_ref.shape[0]
    keys = groups * MEM_LEN
    hd = MEM_HEAD_DIM
    col = lax.broadcasted_iota(jnp.int32, (1, keys), 1)
    group_of_row = lax.broadcasted_iota(jnp.int32, (rows, 1), 0) // (rows // groups)
    own = (col // MEM_LEN) == group_of_row
    head_rows = lambda ref, h: jnp.concatenate(
        [ref[gi, pl.ds(h, MEM_LEN, stride=MEM_HEADS), :] for gi in range(groups)], axis=0).astype(BF16)
    heads = []
    for h in range(MEM_HEADS):
        s = jnp.where(own, _dot_nt(q_ref[:, h * hd:(h + 1) * hd], head_rows(mk_ref, h)), -jnp.inf)
        p = jnp.exp(s - jnp.max(s, axis=-1, keepdims=True))
        heads.append(_dot(p.astype(BF16), head_rows(mv_ref, h)) / jnp.sum(p, axis=-1, keepdims=True))
    o = jnp.concatenate(heads, axis=1).astype(BF16)
    o_ref[...] = x_ref[...] + _dot(o, w_ref[...])


def _post_mix_body(x_ref, oa_ref, og_ref, wa_ref, wg_ref, g_ref, wq_ref, mk_ref, mv_ref, wo_ref, o_ref):
    x2 = x_ref[...] + _dot(oa_ref[...].astype(BF16), wa_ref[...]) + _dot(og_ref[...], wg_ref[...])
    h = _rms(x2, g_ref[...]).astype(BF16)
    q = (_dot(h, wq_ref[...]) * (MEM_HEAD_DIM ** -0.5)).astype(BF16)
    o_ref[...] = x2 + _dot(_cross_heads(q, mk_ref, mv_ref), wo_ref[...])


def _post_mix(x, o_att, o_gla, w_out_att, w_out_gla, g_cross, w_q, mem_k, mem_v, w_o, *, seq, tm=512):
    m, d = x.shape
    tm = min(tm, seq)
    assert seq % tm == 0
    const = lambda a: pl.BlockSpec(a.shape, lambda i: (0,) * a.ndim, pipeline_mode=pl.Buffered(1))
    rows = lambda width: pl.BlockSpec((tm, width), lambda i: (i, 0))
    mem = pl.BlockSpec((1, MEM_HEADS, MEM_LEN, MEM_HEAD_DIM), lambda i: (i // (seq // tm), 0, 0, 0))
    g = g_cross.reshape(1, d)
    return pl.pallas_call(
        _post_mix_body,
        grid=(m // tm,),
        in_specs=[rows(d), rows(ATT_WIDTH), rows(GLA_VALUE_WIDTH), const(w_out_att), const(w_out_gla), const(g),
                  const(w_q), mem, mem, const(w_o)],
        out_specs=rows(d),
        out_shape=jax.ShapeDtypeStruct((m, d), F32),
        compiler_params=_params("parallel"),
        name="post_mix",
    )(x, o_att, o_gla, w_out_att, w_out_gla, g, w_q, mem_k, mem_v, w_o)


def _cross(qc, mem_k, mem_v, x, w_o, *, rows, groups):
    m, d = x.shape
    mem_k, mem_v = (a.reshape(a.shape[0], MEM_LEN * MEM_HEADS, MEM_HEAD_DIM) for a in (mem_k, mem_v))
    mem = pl.BlockSpec((groups, MEM_LEN * MEM_HEADS, MEM_HEAD_DIM), lambda i: (i, 0, 0))
    return pl.pallas_call(
        functools.partial(_cross_body, groups=groups),
        grid=(m // rows,),
        in_specs=[
            pl.BlockSpec((rows, MEM_WIDTH), lambda i: (i, 0)),
            mem,
            mem,
            pl.BlockSpec((rows, d), lambda i: (i, 0)),
            pl.BlockSpec(w_o.shape, lambda i: (0, 0)),
        ],
        out_specs=pl.BlockSpec((rows, d), lambda i: (i, 0)),
        out_shape=jax.ShapeDtypeStruct((m, d), F32),
        compiler_params=_params("parallel"),
        name="cross",
    )(qc, mem_k, mem_v, x, w_o)


def kernel(x_prompt, x_sample, mem_prompt, cache_win_k, cache_win_v, state_gla, cache_mem_k, cache_mem_v, g_ffn1, w1_gate, w1_up, w1_down, g_mix, w_in, w_gla_a2, b_gla_a, g_gla_out, w_out, g_mem, w_mem_k, w_mem_v, g_cross, w_cross_q, w_cross_o, g_ffn2, w2_gate, w2_up, w2_down, g_final):
    b, seq, d = x_prompt.shape
    n, n_new, _ = x_sample.shape
    depth = g_ffn1.shape[0]
    past_len = cache_win_k.shape[2]
    n_keep = min(MAX_WINDOW, seq)
    bf = lambda a: a.astype(BF16)

    xp = x_prompt.reshape(b * seq, d)
    xs = x_sample.reshape(n * n_new, d)
    rope_p = _rope_tables(jnp.arange(seq))
    rope_s = _rope_tables(jnp.tile(past_len + jnp.arange(n_new), n))
    outs = [[] for _ in range(8)]
    cols = np.cumsum([0, ATT_WIDTH, ATT_WIDTH, ATT_WIDTH, GLA_KEY_WIDTH, GLA_KEY_WIDTH, GLA_VALUE_WIDTH, GLA_GATE_RANK])
    for l in range(depth):
        w_main = bf(jnp.concatenate([w_in[l][:, :cols[6]], w_in[l][:, cols[7]:]], axis=1))
        w_lr = bf(jnp.pad(w_in[l][:, cols[6]:cols[7]], ((0, 0), (0, LANES - GLA_GATE_RANK))))
        w_a2 = bf(jnp.pad(w_gla_a2[l], ((0, LANES - GLA_GATE_RANK), (0, 0))))
        w_out_att, w_out_gla = bf(w_out[l][:ATT_WIDTH]), bf(w_out[l][ATT_WIDTH:])
        w_mem = bf(jnp.concatenate([w_mem_k[l], w_mem_v[l]], axis=1))
        w_cq, w_co = bf(w_cross_q[l]), bf(w_cross_o[l])
        gf = g_final if l == depth - 1 else None
        mix_in = lambda x, rope: _mix_in(x, g_mix[l], w_main, w_lr, w_a2, b_gla_a[l], rope)

        w2 = (bf(w2_gate[l]), bf(w2_up[l]), bf(w2_down[l]))
        xs, *w1 = _ffn(xs, g_ffn1[l], w1_gate[l], w1_up[l], w1_down[l])
        zs, la_s = mix_in(xs, rope_s)
        weights_rider, values_rider, attn_finish = _attn_sample_riders(
            zs.reshape(n, n_new, Z_COLS), cache_win_k[l], cache_win_v[l])

        xp, att_w, att_aux = _ffn(xp, g_ffn1[l], *w1, rider=weights_rider)
        mem_kv = _norm_proj(mem_prompt.reshape(b * MEM_LEN, d), g_mem[l], w_mem).reshape(b, MEM_LEN, 2 * MEM_WIDTH)
        mem_k = mem_kv[:, :, :MEM_WIDTH].reshape(b, MEM_LEN, MEM_HEADS, MEM_HEAD_DIM)
        mem_v = mem_kv[:, :, MEM_WIDTH:].reshape(b, MEM_LEN, MEM_HEADS, MEM_HEAD_DIM)
        zp, la_p = mix_in(xp, rope_p)
        zp3 = zp.reshape(b, seq, Z_COLS)
        o_att_p = _attn_prompt(zp3).reshape(b * seq, ATT_WIDTH)
        o_gla_p, st_p = _gla_prompt(zp3, la_p.reshape(b, seq, GLA_KEY_WIDTH), g_gla_out[l])
        xp = _post_mix(xp, o_att_p, o_gla_p.reshape(b * seq, GLA_VALUE_WIDTH), w_out_att, w_out_gla, g_cross[l], w_cq,
                       mem_k.transpose(0, 2, 1, 3), mem_v.transpose(0, 2, 1, 3), w_co, seq=seq)
        xp, o_att_s = _ffn(xp, g_ffn2[l], *w2, g_final=gf, rider=values_rider(att_w, att_aux))

        o_att_s = attn_finish(o_att_s).reshape(n * n_new, ATT_WIDTH)
        o_gla_s, st_s = _gla_sample(zs, la_s, g_gla_out[l], state_gla[l], n_new)
        xs, qc = _out_cross_q(xs, o_att_s, o_gla_s, w_out_att, w_out_gla, g_cross[l], w_cq)
        groups = min(SAMPLE_GROUP, n)
        xs = _cross(qc, cache_mem_k[l], cache_mem_v[l], xs, w_co, rows=groups * n_new, groups=groups)
        xs = _ffn(xs, g_ffn2[l], *w2, g_final=gf)

        heads = lambda a, t: a.reshape(-1, t, ATT_HEADS, ATT_HEAD_DIM)
        outs[0].append(heads(zp3[:, seq - n_keep:, ATT_WIDTH:2 * ATT_WIDTH], n_keep))
        outs[1].append(heads(zp3[:, seq - n_keep:, 2 * ATT_WIDTH:3 * ATT_WIDTH], n_keep))
        outs[2].append(st_p)
        outs[3].append(mem_k)
        outs[4].append(mem_v)
        outs[5].append(heads(zs[:, ATT_WIDTH:2 * ATT_WIDTH], n_new))
        outs[6].append(heads(zs[:, 2 * ATT_WIDTH:3 * ATT_WIDTH], n_new))
        outs[7].append(st_s)

    return (xp.reshape(b, seq, d), xs.reshape(n, n_new, d)) + tuple(jnp.stack(o) for o in outs)
```

```python
import functools

import jax
import jax.numpy as jnp
import numpy as np
from jax import lax
from jax.experimental import pallas as pl
from jax.experimental.pallas import tpu as pltpu

F32 = jnp.float32
BF16 = jnp.bfloat16

EPS = 1e-6
ATT_HEADS = 16
ATT_HEAD_DIM = 64
ATT_WIDTH = ATT_HEADS * ATT_HEAD_DIM
DILATIONS = (1, 4, 16)
N_BACK = 128
BAND = 128
ROT_DIM = 16
ROPE_THETA = 500000.0
GLA_HEADS = 4
GLA_HEAD_K = 128
GLA_HEAD_V = 256
GLA_KEY_WIDTH = GLA_HEADS * GLA_HEAD_K
GLA_VALUE_WIDTH = GLA_HEADS * GLA_HEAD_V
GLA_GATE_RANK = 16
GLA_GATE_TAU = 16.0
GLA_CHUNK = 64
MEM_LEN = 256
MEM_HEADS = 4
MEM_HEAD_DIM = 128
MEM_WIDTH = MEM_HEADS * MEM_HEAD_DIM
MAX_WINDOW = 2048
LANES = 128
SAMPLE_GROUP = 8
Z_COLS = 3 * ATT_WIDTH + 2 * GLA_KEY_WIDTH + 2 * GLA_VALUE_WIDTH
Z_TILE = 2 * ATT_WIDTH
VMEM_LIMIT_BYTES = 56 * 1024 * 1024


def _params(*semantics, vmem_limit_bytes=VMEM_LIMIT_BYTES):
    return pltpu.CompilerParams(dimension_semantics=semantics, vmem_limit_bytes=vmem_limit_bytes)


def _rms(x, g):
    return x * lax.rsqrt(jnp.mean(x * x, axis=-1, keepdims=True) + EPS) * g


def _dot(a, b):
    return jnp.dot(a, b, preferred_element_type=F32)


def _dot_nt(a, b):
    return lax.dot_general(a, b, (((1,), (1,)), ((), ())), preferred_element_type=F32)


def _dot_tn(a, b):
    return lax.dot_general(a, b, (((0,), (0,)), ((), ())), preferred_element_type=F32)


def _split_bf16(x):
    hi = x.astype(BF16)
    lo = (x - hi.astype(F32)).astype(BF16)
    return hi, lo


RIDER_HOST_ROWS = 1024
RIDER_VMEM_LIMIT_BYTES = 62 * 1024 * 1024


def _ffn_body(*refs, nf, final, emit, rider):
    x_ref, g_ref, wg_ref, wu_ref, wd_ref = refs[:5]
    refs = refs[5:]
    if final:
        gf_ref, refs = refs[0], refs[1:]
    if rider is not None:
        rider_in, refs = refs[:len(rider.args)], refs[len(rider.args):]
        rider_out = refs[-1 - len(rider.out_specs):-1]
    o_ref, h_ref = refs[0], refs[-1]
    f = pl.program_id(1)

    @pl.when(f == 0)
    def _():
        h_ref[...] = _rms(x_ref[...], g_ref[...]).astype(BF16)
        o_ref[...] = jnp.zeros_like(o_ref)

    def step(with_rider):
        wg, wu, wd = wg_ref[...].astype(BF16), wu_ref[...].astype(BF16), wd_ref[...].astype(BF16)
        if emit:
            for w_out_ref, w in zip(refs[1:4], (wg, wu, wd)):
                w_out_ref[...] = w
        h = h_ref[...]
        gate = _dot(h, wg)
        if with_rider:
            carried = rider.first(*rider_in)
        up = _dot(h, wu)
        if with_rider:
            rider.second(*rider_in, *rider_out, carried)
        act = (gate * jax.nn.sigmoid(gate) * up).astype(BF16)
        o_ref[...] += _dot(act, wd)

    if rider is None:
        step(False)
    else:
        active = rider.active(f)
        pl.when(active)(functools.partial(step, True))
        pl.when(jnp.logical_not(active))(functools.partial(step, False))

    @pl.when(f == nf - 1)
    def _():
        y = x_ref[...] + 0.5 * o_ref[...]
        if final:
            y = _rms(y, gf_ref[...])
        o_ref[...] = y


def _ffn(x, g, wg, wu, wd, g_final=None, *, tm=512, tf=None, rider=None):
    m, d = x.shape
    dff = wg.shape[1]
    if rider is not None:
        tm = RIDER_HOST_ROWS
    tm = min(tm, m)
    emit = wg.dtype != BF16
    if tf is None:
        tf = 256 if emit or rider is not None else 512
    nf = dff // tf
    assert dff % tf == 0 and m % tm == 0
    assert not emit or m == tm
    final = g_final is not None
    row = pl.BlockSpec((1, d), lambda i, f: (0, 0))
    w_specs = [
        pl.BlockSpec((d, tf), lambda i, f: (0, f)),
        pl.BlockSpec((d, tf), lambda i, f: (0, f)),
        pl.BlockSpec((tf, d), lambda i, f: (f, 0)),
    ]
    x_mode = {} if rider is None else dict(pipeline_mode=pl.Buffered(1))
    in_specs = [pl.BlockSpec((tm, d), lambda i, f: (i, 0), **x_mode), row] + w_specs
    args = [x, g.reshape(1, d), wg, wu, wd]
    if final:
        in_specs.append(row)
        args.append(g_final.reshape(1, d))
    out_specs = [pl.BlockSpec((tm, d), lambda i, f: (i, 0))]
    out_shape = [jax.ShapeDtypeStruct((m, d), F32)]
    if emit:
        out_specs += w_specs
        out_shape += [jax.ShapeDtypeStruct(w.shape, BF16) for w in (wg, wu, wd)]
    if rider is not None:
        rider = rider(m // tm, nf)
        in_specs += rider.in_specs
        args += rider.args
        out_specs += rider.out_specs
        out_shape += rider.out_shape
    out = pl.pallas_call(
        functools.partial(_ffn_body, nf=nf, final=final, emit=emit, rider=rider),
        grid=(m // tm, nf),
        in_specs=in_specs,
        out_specs=out_specs,
        out_shape=out_shape,
        scratch_shapes=[pltpu.VMEM((tm, d), BF16)],
        compiler_params=(_params("parallel", "arbitrary") if rider is None else
                         _params("arbitrary", "arbitrary", vmem_limit_bytes=RIDER_VMEM_LIMIT_BYTES)),
        name=("ffn_final" if final else "ffn") + ("_f32w" if emit else "") + ("" if rider is None else "_" + rider.name),
    )(*args)
    return out if len(out) > 1 else out[0]


def _mix_in_body(x_ref, g_ref, w_ref, wlr_ref, wa2_ref, ba_ref, rope_ref, z_ref, la_ref, h_ref):
    n = pl.program_id(1)

    @pl.when(n == 0)
    def _():
        h = _rms(x_ref[...], g_ref[...]).astype(BF16)
        h_ref[...] = h
        g_lr = _dot(h, wlr_ref[...])
        pre = _dot(g_lr.astype(BF16), wa2_ref[...]) + ba_ref[...]
        la_ref[...] = (jnp.minimum(pre, 0.0) - jnp.log1p(jnp.exp(-jnp.abs(pre)))) * (1.0 / GLA_GATE_TAU)

    z = _dot(h_ref[...], w_ref[...])

    @pl.when(n > 0)
    def _():
        z_ref[...] = z

    @pl.when(n == 0)
    def _():
        cos, sin_up, sin_dn = rope_ref[0], rope_ref[1], rope_ref[2]
        half = ROT_DIM // 2
        for j in range(Z_TILE // LANES):
            zc = z[:, j * LANES:(j + 1) * LANES]
            rot = zc * cos + pltpu.roll(zc, half, axis=1) * sin_up + pltpu.roll(zc, LANES - half, axis=1) * sin_dn
            z_ref[:, j * LANES:(j + 1) * LANES] = rot * (ATT_HEAD_DIM ** -0.5) if j * LANES < ATT_WIDTH else rot


def _mix_in(x, g, w_main, w_lr, w_a2, b_a, rope, *, tm=512):
    m, d = x.shape
    tm = min(tm, m)
    n_rope = rope.shape[1] // tm
    return pl.pallas_call(
        _mix_in_body,
        grid=(m // tm, Z_COLS // Z_TILE),
        in_specs=[
            pl.BlockSpec((tm, d), lambda i, n: (i, 0)),
            pl.BlockSpec((1, d), lambda i, n: (0, 0)),
            pl.BlockSpec((d, Z_TILE), lambda i, n: (0, n)),
            pl.BlockSpec((d, LANES), lambda i, n: (0, 0)),
            pl.BlockSpec((LANES, GLA_KEY_WIDTH), lambda i, n: (0, 0)),
            pl.BlockSpec((1, GLA_KEY_WIDTH), lambda i, n: (0, 0)),
            pl.BlockSpec((3, tm, LANES), lambda i, n: (0, i % n_rope, 0)),
        ],
        out_specs=[
            pl.BlockSpec((tm, Z_TILE), lambda i, n: (i, n)),
            pl.BlockSpec((tm, GLA_KEY_WIDTH), lambda i, n: (i, 0)),
        ],
        out_shape=[
            jax.ShapeDtypeStruct((m, Z_COLS), F32),
            jax.ShapeDtypeStruct((m, GLA_KEY_WIDTH), F32),
        ],
        scratch_shapes=[pltpu.VMEM((tm, d), BF16)],
        compiler_params=_params("parallel", "arbitrary"),
        name="mix_in",
    )(x, g.reshape(1, d), w_main, w_lr, w_a2, b_a.reshape(1, -1), rope)


def _rope_tables(pos):
    half = ROT_DIM // 2
    inv = ROPE_THETA ** (-jnp.arange(half, dtype=F32) * 2.0 / ROT_DIM)
    ang = pos.astype(F32)[:, None] * inv[None, :]
    cos, sin = jnp.cos(ang), jnp.sin(ang)
    t = pos.shape[0]
    zeros = jnp.zeros((t, half), F32)
    rest0 = jnp.zeros((t, ATT_HEAD_DIM - ROT_DIM), F32)
    c = jnp.concatenate([cos, cos, jnp.ones_like(rest0)], axis=1)
    s_up = jnp.concatenate([zeros, sin, rest0], axis=1)
    s_dn = jnp.concatenate([-sin, zeros, rest0], axis=1)
    reps = LANES // ATT_HEAD_DIM
    return jnp.stack([jnp.tile(c, (1, reps)), jnp.tile(s_up, (1, reps)), jnp.tile(s_dn, (1, reps))])


def _norm_proj_body(x_ref, g_ref, w_ref, o_ref):
    h = _rms(x_ref[...], g_ref[...]).astype(BF16)
    o_ref[...] = _dot(h, w_ref[...])


def _norm_proj(x, g, w, *, tm=512):
    m, d = x.shape
    n = w.shape[1]
    tm = min(tm, m)
    return pl.pallas_call(
        _norm_proj_body,
        grid=(m // tm,),
        in_specs=[
            pl.BlockSpec((tm, d), lambda i: (i, 0)),
            pl.BlockSpec((1, d), lambda i: (0, 0)),
            pl.BlockSpec((d, n), lambda i: (0, 0)),
        ],
        out_specs=pl.BlockSpec((tm, n), lambda i: (i, 0)),
        out_shape=jax.ShapeDtypeStruct((m, n), F32),
        compiler_params=_params("parallel"),
        name="memory_kv",
    )(x, g.reshape(1, d), w)


ATTN_UNROLL = 32
MERGE_ROWS = 256
MERGE_UNROLL = 4


def _band_bias():
    i = np.arange(BAND)[:, None]
    j = np.arange(2 * BAND)[None, :]
    out = []
    for off in (0, BAND):
        dist = off + i - j
        out.append(np.where((dist >= 0) & (dist <= N_BACK), 0.0, -np.inf))
    return jnp.asarray(np.stack(out), F32)


def _attn_prompt_body(q_ref, k_ref, v_ref, bias_ref, o_ref, acc_ref, m_ref, l_ref, *, seq):
    hd = ATT_HEAD_DIM
    assert LANES == 2 * hd
    blocks = seq // BAND
    first_head = lax.broadcasted_iota(jnp.int32, (1, LANES), 1) < hd
    for ci, dil in enumerate(DILATIONS):
        nb = blocks // dil
        shift = nb.bit_length() - 1
        assert nb == 1 << shift

        def step(it, carry, ci=ci, dil=dil, nb=nb, shift=shift):
            loaded = []
            for u in range(ATTN_UNROLL):
                w = it * ATTN_UNROLL + u
                r = lax.shift_right_logical(w, shift)
                n = w & (nb - 1)
                base = jnp.maximum(n - 1, 0) * BAND
                if dil == 1:
                    q_rows = pl.ds(pl.multiple_of(n * BAND, BAND), BAND)
                    k_rows = pl.ds(pl.multiple_of(base, BAND), 2 * BAND)
                else:
                    q_rows = pl.ds(r + dil * BAND * n, BAND, stride=dil)
                    k_rows = pl.ds(r + dil * base, 2 * BAND, stride=dil)
                bias = bias_ref[jnp.minimum(n, 1)]
                loaded.append((q_rows, q_ref[0, q_rows, :], k_ref[0, k_rows, :], v_ref[0, k_rows, :], bias))
            results = []
            for q_rows, q, k, v, bias in loaded:
                kb = k.astype(BF16)
                qs = (jnp.where(first_head, q, 0.0), jnp.where(first_head, 0.0, q))
                vs = (jnp.where(first_head, v, 1.0), jnp.where(first_head, 1.0, v))
                out, mx = [], []
                for hh in range(LANES // hd):
                    s = _dot_nt(qs[hh].astype(BF16), kb) + bias
                    m_c = jnp.max(s, axis=1, keepdims=True)
                    p = jnp.exp(s - m_c)
                    out.append(_dot(p.astype(BF16), vs[hh].astype(BF16)))
                    mx.append(m_c)
                results.append((q_rows, jnp.where(first_head, out[0], out[1]),
                                jnp.where(first_head, mx[0], mx[1]), jnp.where(first_head, out[1], out[0])))
            for q_rows, pv, mx, den_swapped in results:
                acc_ref[ci, q_rows, :] = pv
                m_ref[ci, q_rows, :] = mx
                l_ref[ci, q_rows, :] = den_swapped
            return carry

        lax.fori_loop(0, blocks // ATTN_UNROLL, step, 0)

    def merge(i, carry):
        rows = pl.ds(pl.multiple_of(i * MERGE_ROWS, MERGE_ROWS), MERGE_ROWS)
        ms = [m_ref[ci, rows, :] for ci in range(len(DILATIONS))]
        m_all = functools.reduce(jnp.maximum, ms)
        ws = [jnp.exp(m - m_all) for m in ms]
        num = sum(w * acc_ref[ci, rows, :] for ci, w in enumerate(ws))
        den = sum(w * pltpu.roll(l_ref[ci, rows, :], hd, axis=1) for ci, w in enumerate(ws))
        o_ref[0, rows, :] = (num / den).astype(o_ref.dtype)
        return carry

    lax.fori_loop(0, seq // MERGE_ROWS, merge, 0, unroll=MERGE_UNROLL)


def _attn_prompt(z3):
    b, seq, _ = z3.shape
    pairs = ATT_WIDTH // LANES
    assert seq % (DILATIONS[-1] * BAND) == 0 and seq // DILATIONS[-1] >= 2 * BAND
    assert (seq // BAND) % ATTN_UNROLL == 0 and seq % MERGE_ROWS == 0
    blk = lambda off: pl.BlockSpec((1, seq, LANES), lambda i, p: (i, 0, off + p))
    return pl.pallas_call(
        functools.partial(_attn_prompt_body, seq=seq),
        grid=(b, pairs),
        in_specs=[blk(0), blk(pairs), blk(2 * pairs),
                  pl.BlockSpec((2, BAND, 2 * BAND), lambda i, p: (0, 0, 0))],
        out_specs=pl.BlockSpec((1, seq, LANES), lambda i, p: (i, 0, p)),
        out_shape=jax.ShapeDtypeStruct((b, seq, ATT_WIDTH), BF16),
        scratch_shapes=[pltpu.VMEM((len(DILATIONS), seq, LANES), F32)] * 3,
        compiler_params=_params("parallel", "parallel"),
        name="attn_prompt",
    )(z3, z3, z3, _band_bias())


SAMPLE_ROWS = 8


def _sample_multiplicity(n_buf, n_new):
    t = (np.arange(SAMPLE_ROWS) % n_new)[:, None]
    r = np.arange(n_buf)[None, :]
    mult = np.zeros((SAMPLE_ROWS, n_buf), np.float32)
    for dil in DILATIONS:
        back = n_buf + t - r
        mult += (back % dil == 0) & (back >= dil) & (back <= dil * N_BACK)
    return jnp.asarray(mult, F32)


def _new_row_multiplicity(n_new):
    t_row = lax.broadcasted_iota(jnp.int32, (SAMPLE_ROWS, 1), 0) % n_new
    return [(t_row >= t).astype(F32) + (len(DILATIONS) - 1.0) * (t_row == t).astype(F32) for t in range(n_new)]


def _attn_scores(q_ref, kn_ref, kt_ref, mult_ref, *, n_new):
    live = mult_ref[...] > 0.0
    mult_new = _new_row_multiplicity(n_new)
    scores = []
    for h in range(ATT_HEADS):
        q = q_ref[0, h]
        kn = kn_ref[0, h]
        s = jnp.where(live, _dot(q.astype(BF16), kt_ref[0, h].astype(BF16)), -jnp.inf)
        s_new = [jnp.where(mult_new[t] > 0.0, jnp.sum(q * kn[t:t + 1], axis=1, keepdims=True), -jnp.inf)
                 for t in range(n_new)]
        scores.append((s, s_new))
    return scores


def _attn_weights(q_ref, kn_ref, kt_ref, mult_ref, w_ref, aux_ref, scores, *, n_new):
    mult = mult_ref[...]
    mult_new = _new_row_multiplicity(n_new)
    lane = lax.broadcasted_iota(jnp.int32, (1, LANES), 1)
    for h, (s, s_new) in enumerate(scores):
        m = functools.reduce(jnp.maximum, s_new, jnp.max(s, axis=1, keepdims=True))
        w = mult * jnp.exp(s - m)
        w_new = [mult_new[t] * jnp.exp(s_new[t] - m) for t in range(n_new)]
        den = jnp.sum(w, axis=1, keepdims=True) + sum(w_new)
        w_ref[0, h] = w
        aux_ref[0, h] = sum(jnp.where(lane == t, w_new[t], 0.0) for t in range(n_new)) + jnp.where(lane == n_new, den, 0.0)


def _attn_value_products(w_ref, aux_ref, vn_ref, vt_ref):
    return [_dot_nt(w_ref[0, h].astype(BF16), vt_ref[0, h].astype(BF16)) for h in range(ATT_HEADS)]


def _attn_outputs(w_ref, aux_ref, vn_ref, vt_ref, o_ref, products, *, n_new):
    for h, o in enumerate(products):
        aux = aux_ref[0, h]
        vn = vn_ref[0, h]
        o = o + sum(aux[:, t:t + 1] * vn[t:t + 1] for t in range(n_new))
        o_ref[0, h] = o / aux[:, n_new:n_new + 1]


class _Rider:
    def __init__(self, name, in_specs, args, out_specs, out_shape, first, second, active):
        self.name, self.in_specs, self.args, self.out_specs, self.out_shape = name, in_specs, args, out_specs, out_shape
        self.first, self.second = first, second
        self.active = active


def _attn_sample_riders(z3, cache_k, cache_v):
    n, n_new, _ = z3.shape
    n_buf = cache_k.shape[1]
    nh, hd = ATT_HEADS, ATT_HEAD_DIM
    assert n_buf >= DILATIONS[-1] * N_BACK and n_new < SAMPLE_ROWS and n_new < LANES

    def per_head(col):
        a = z3[:, :, col * ATT_WIDTH:(col + 1) * ATT_WIDTH].reshape(n, n_new, nh, hd).transpose(0, 2, 1, 3)
        return jnp.pad(a, ((0, 0), (0, 0), (0, SAMPLE_ROWS - n_new), (0, 0)))

    rows_on_lanes = lambda c: c.transpose(0, 2, 3, 1)

    def specs(tiles, steps):
        per_tile = n // tiles
        assert per_tile <= steps and per_tile * tiles == n
        where = lambda i, f: (i * per_tile + jnp.minimum(f, per_tile - 1), 0, 0, 0)
        block = lambda minor: pl.BlockSpec((1, nh) + minor, where)
        active = lambda f: f < per_tile
        return (block((SAMPLE_ROWS, hd)), block((hd, n_buf)), block((SAMPLE_ROWS, n_buf)), block((SAMPLE_ROWS, LANES)),
                active)

    def weights_rider(tiles, steps):
        small, cache, weights, aux, active = specs(tiles, steps)
        return _Rider(
            "attn_weights",
            [small, small, cache, pl.BlockSpec((SAMPLE_ROWS, n_buf), lambda i, f: (0, 0))],
            [per_head(0), per_head(1), rows_on_lanes(cache_k), _sample_multiplicity(n_buf, n_new)],
            [weights, aux],
            [jax.ShapeDtypeStruct((n, nh, SAMPLE_ROWS, n_buf), F32), jax.ShapeDtypeStruct((n, nh, SAMPLE_ROWS, LANES), F32)],
            functools.partial(_attn_scores, n_new=n_new),
            functools.partial(_attn_weights, n_new=n_new),
            active,
        )

    def values_rider(w, aux_w):
        def make(tiles, steps):
            small, cache, weights, aux, active = specs(tiles, steps)
            return _Rider(
                "attn_values",
                [weights, aux, small, cache],
                [w, aux_w, per_head(2), rows_on_lanes(cache_v)],
                [small],
                [jax.ShapeDtypeStruct((n, nh, SAMPLE_ROWS, hd), F32)],
                _attn_value_products,
                functools.partial(_attn_outputs, n_new=n_new),
                active,
            )
        return make

    finish = lambda o: o[:, :, :n_new].transpose(0, 2, 1, 3).reshape(n, n_new, ATT_WIDTH)
    return weights_rider, values_rider, finish


def _gla_rows(q, k, v, li, states, causal, same, group_of_row):
    c = q.shape[0]
    hi, lo = _split_bf16(li)
    tri = causal.astype(BF16)
    b = _dot(tri, hi) + _dot(tri, lo)
    if len(states) == 1:
        b_end = b[c - 1:c, :]
    else:
        full = same.astype(BF16)
        b_end = _dot(full, hi) + _dot(full, lo)
    q_t = (q * (GLA_HEAD_K ** -0.5) * jnp.exp(b)).astype(BF16)
    k_n = (k * jnp.exp(-b)).astype(BF16)
    k_w = k * jnp.exp(b_end - b)
    vb = v.astype(BF16)
    a = jnp.where(causal, _dot_nt(q_t, k_n), 0.0)
    o = _dot(a.astype(BF16), vb)
    ones = jnp.ones((c, LANES), BF16)
    new_states = []
    for gi, s in enumerate(states):
        if len(states) == 1:
            o = o + _dot(q_t, s.astype(BF16))
            kw_g, hi_g, lo_g = k_w.astype(BF16), hi, lo
        else:
            mine = group_of_row == gi
            o = o + jnp.where(mine, _dot(q_t, s.astype(BF16)), 0.0)
            kw_g = jnp.where(mine, k_w, 0.0).astype(BF16)
            hi_g = jnp.where(mine, hi, jnp.zeros_like(hi))
            lo_g = jnp.where(mine, lo, jnp.zeros_like(lo))
        decay = jnp.exp(_dot_tn(hi_g, ones) + _dot_tn(lo_g, ones))
        decay = jnp.concatenate([decay] * (s.shape[1] // LANES), axis=1)
        new_states.append(decay * s + _dot_tn(kw_g, vb))
    return o, new_states


def _gla_finish(o, g_out, rg):
    o = o * lax.rsqrt(jnp.mean(o * o, axis=-1, keepdims=True) + EPS)
    return (o * g_out) * (rg * jax.nn.sigmoid(rg))


GLA_HEADS_PER_STEP = 4
GLA_SEQ_TILE = 1024
GLA_CHUNK_UNROLL = 4
GLA_SAMPLE_HEADS_PER_STEP = 2


def _gla_prompt_body(q_ref, k_ref, v_ref, rg_ref, la_ref, g_ref, o_ref, s_ref, *, rows, chunk, heads):
    hk, hv = GLA_HEAD_K, GLA_HEAD_V
    r = lax.broadcasted_iota(jnp.int32, (chunk, chunk), 0)
    c = lax.broadcasted_iota(jnp.int32, (chunk, chunk), 1)
    causal = c <= r

    @pl.when(pl.program_id(2) == 0)
    def _():
        s_ref[...] = jnp.zeros_like(s_ref)

    def step(i, carry):
        rr = pl.ds(pl.multiple_of(i * chunk, chunk), chunk)
        for hh in range(heads):
            ks = slice(hh * hk, (hh + 1) * hk)
            vs = slice(hh * hv, (hh + 1) * hv)
            o, (s_new,) = _gla_rows(q_ref[0, rr, ks], k_ref[0, rr, ks], v_ref[0, rr, vs], la_ref[0, rr, ks],
                                    [s_ref[0, hh]], causal, None, None)
            s_ref[0, hh] = s_new
            o_ref[0, rr, vs] = _gla_finish(o, g_ref[:, vs], rg_ref[0, rr, vs]).astype(o_ref.dtype)
        return carry

    lax.fori_loop(0, rows // chunk, step, 0, unroll=GLA_CHUNK_UNROLL)


def _gla_prompt(z3, la3, g_out):
    b, seq, _ = z3.shape
    hk, hv, nh = GLA_HEAD_K, GLA_HEAD_V, GLA_HEADS
    chunk = int(np.gcd(seq, GLA_CHUNK))
    hs = GLA_HEADS_PER_STEP
    rows = min(GLA_SEQ_TILE, seq)
    assert seq % rows == 0 and (rows // chunk) % GLA_CHUNK_UNROLL == 0 and nh % hs == 0
    q0 = 3 * ATT_WIDTH // (hs * hk)
    k0 = q0 + nh // hs
    v0 = (3 * ATT_WIDTH + 2 * GLA_KEY_WIDTH) // (hs * hv)
    r0 = v0 + nh // hs
    return pl.pallas_call(
        functools.partial(_gla_prompt_body, rows=rows, chunk=chunk, heads=hs),
        grid=(b, nh // hs, seq // rows),
        in_specs=[
            pl.BlockSpec((1, rows, hs * hk), lambda i, h, t: (i, t, q0 + h)),
            pl.BlockSpec((1, rows, hs * hk), lambda i, h, t: (i, t, k0 + h)),
            pl.BlockSpec((1, rows, hs * hv), lambda i, h, t: (i, t, v0 + h)),
            pl.BlockSpec((1, rows, hs * hv), lambda i, h, t: (i, t, r0 + h)),
            pl.BlockSpec((1, rows, hs * hk), lambda i, h, t: (i, t, h)),
            pl.BlockSpec((1, hs * hv), lambda i, h, t: (0, h)),
        ],
        out_specs=[
            pl.BlockSpec((1, rows, hs * hv), lambda i, h, t: (i, t, h)),
            pl.BlockSpec((1, hs, hk, hv), lambda i, h, t: (i, h, 0, 0)),
        ],
        out_shape=[
            jax.ShapeDtypeStruct((b, seq, GLA_VALUE_WIDTH), BF16),
            jax.ShapeDtypeStruct((b, nh, hk, hv), F32),
        ],
        compiler_params=_params("parallel", "parallel", "arbitrary"),
        name="gla_prompt",
    )(z3, z3, z3, z3, la3, g_out.reshape(1, -1))


def _gla_sample_body(q_ref, k_ref, v_ref, rg_ref, la_ref, g_ref, s0_ref, o_ref, s_ref, *, n_new, groups, heads):
    hk, hv = GLA_HEAD_K, GLA_HEAD_V
    rows = n_new * groups
    r = lax.broadcasted_iota(jnp.int32, (rows, rows), 0)
    c = lax.broadcasted_iota(jnp.int32, (rows, rows), 1)
    same = (r // n_new) == (c // n_new)
    causal = same & (c <= r)
    group_of_row = lax.broadcasted_iota(jnp.int32, (rows, 1), 0) // n_new
    for hh in range(heads):
        ks = slice(hh * hk, (hh + 1) * hk)
        vs = slice(hh * hv, (hh + 1) * hv)
        o, new_states = _gla_rows(q_ref[:, ks], k_ref[:, ks], v_ref[:, vs], la_ref[:, ks],
                                  [s0_ref[gi, hh] for gi in range(groups)], causal, same, group_of_row)
        for gi in range(groups):
            s_ref[gi, hh] = new_states[gi]
        o_ref[:, vs] = _gla_finish(o, g_ref[:, vs], rg_ref[:, vs]).astype(o_ref.dtype)


def _gla_sample(z, la, g_out, state, n_new):
    m = z.shape[0]
    n = m // n_new
    hk, hv, nh = GLA_HEAD_K, GLA_HEAD_V, GLA_HEADS
    groups = min(SAMPLE_GROUP, n)
    rows = groups * n_new
    hs = GLA_SAMPLE_HEADS_PER_STEP
    q0 = 3 * ATT_WIDTH // (hs * hk)
    k0 = q0 + nh // hs
    v0 = (3 * ATT_WIDTH + 2 * GLA_KEY_WIDTH) // (hs * hv)
    r0 = v0 + nh // hs
    state_spec = pl.BlockSpec((groups, hs, hk, hv), lambda i, h: (i, h, 0, 0))
    return pl.pallas_call(
        functools.partial(_gla_sample_body, n_new=n_new, groups=groups, heads=hs),
        grid=(n // groups, nh // hs),
        in_specs=[
            pl.BlockSpec((rows, hs * hk), lambda i, h: (i, q0 + h)),
            pl.BlockSpec((rows, hs * hk), lambda i, h: (i, k0 + h)),
            pl.BlockSpec((rows, hs * hv), lambda i, h: (i, v0 + h)),
            pl.BlockSpec((rows, hs * hv), lambda i, h: (i, r0 + h)),
            pl.BlockSpec((rows, hs * hk), lambda i, h: (i, h)),
            pl.BlockSpec((1, hs * hv), lambda i, h: (0, h)),
            state_spec,
        ],
        out_specs=[pl.BlockSpec((rows, hs * hv), lambda i, h: (i, h)), state_spec],
        out_shape=[
            jax.ShapeDtypeStruct((m, GLA_VALUE_WIDTH), BF16),
            jax.ShapeDtypeStruct(state.shape, F32),
        ],
        compiler_params=_params("parallel", "parallel"),
        name="gla_sample",
    )(z, z, z, z, la, g_out.reshape(1, -1), state)


def _out_cross_q_body(x_ref, oa_ref, og_ref, wa_ref, wg_ref, g_ref, wq_ref, x2_ref, qc_ref):
    oa = oa_ref[...].astype(BF16)
    x2 = x_ref[...] + _dot(oa, wa_ref[...]) + _dot(og_ref[...], wg_ref[...])
    x2_ref[...] = x2
    h = _rms(x2, g_ref[...]).astype(BF16)
    qc_ref[...] = (_dot(h, wq_ref[...]) * (MEM_HEAD_DIM ** -0.5)).astype(qc_ref.dtype)


def _out_cross_q(x, o_att, o_gla, w_out_att, w_out_gla, g_cross, w_q, *, tm=512):
    m, d = x.shape
    tm = min(tm, m)
    const = lambda shape: pl.BlockSpec(shape, lambda i: (0, 0))
    return pl.pallas_call(
        _out_cross_q_body,
        grid=(m // tm,),
        in_specs=[
            pl.BlockSpec((tm, d), lambda i: (i, 0)),
            pl.BlockSpec((tm, ATT_WIDTH), lambda i: (i, 0)),
            pl.BlockSpec((tm, GLA_VALUE_WIDTH), lambda i: (i, 0)),
            const(w_out_att.shape),
            const(w_out_gla.shape),
            const((1, d)),
            const(w_q.shape),
        ],
        out_specs=[pl.BlockSpec((tm, d), lambda i: (i, 0)), pl.BlockSpec((tm, MEM_WIDTH), lambda i: (i, 0))],
        out_shape=[jax.ShapeDtypeStruct((m, d), F32), jax.ShapeDtypeStruct((m, MEM_WIDTH), BF16)],
        compiler_params=_params("parallel"),
        name="out_cross_q",
    )(x, o_att, o_gla, w_out_att, w_out_gla, g_cross.reshape(1, d), w_q)


def _cross_heads(q, mk_ref, mv_ref):
    hd = MEM_HEAD_DIM
    heads = []
    for h in range(MEM_HEADS):
        s = _dot_nt(q[:, h * hd:(h + 1) * hd], mk_ref[0, h].astype(BF16))
        p = jnp.exp(s - jnp.max(s, axis=-1, keepdims=True))
        heads.append(_dot(p.astype(BF16), mv_ref[0, h].astype(BF16)) / jnp.sum(p, axis=-1, keepdims=True))
    return jnp.concatenate(heads, axis=1).astype(BF16)


def _cross_body(q_ref, mk_ref, mv_ref, x_ref, w_ref, o_ref, *, groups):
    rows = q_ref.shape[0]
    keys = groups * MEM_LEN
    hd = MEM_HEAD_DIM
    col = lax.broadcasted_iota(jnp.int32, (1, keys), 1)
    group_of_row = lax.broadcasted_iota(jnp.int32, (rows, 1), 0) // (rows // groups)
    own = (col // MEM_LEN) == group_of_row
    head_rows = lambda ref, h: jnp.concatenate(
        [ref[gi, pl.ds(h, MEM_LEN, stride=MEM_HEADS), :] for gi in range(groups)], axis=0).astype(BF16)
    heads = []
    for h in range(MEM_HEADS):
        s = jnp.where(own, _dot_nt(q_ref[:, h * hd:(h + 1) * hd], head_rows(mk_ref, h)), -jnp.inf)
        p = jnp.exp(s - jnp.max(s, axis=-1, keepdims=True))
        heads.append(_dot(p.astype(BF16), head_rows(mv_ref, h)) / jnp.sum(p, axis=-1, keepdims=True))
    o = jnp.concatenate(heads, axis=1).astype(BF16)
    o_ref[...] = x_ref[...] + _dot(o, w_ref[...])


def _post_mix_body(x_ref, oa_ref, og_ref, wa_ref, wg_ref, g_ref, wq_ref, mk_ref, mv_ref, wo_ref, o_ref):
    x2 = x_ref[...] + _dot(oa_ref[...].astype(BF16), wa_ref[...]) + _dot(og_ref[...], wg_ref[...])
    h = _rms(x2, g_ref[...]).astype(BF16)
    q = (_dot(h, wq_ref[...]) * (MEM_HEAD_DIM ** -0.5)).astype(BF16)
    o_ref[...] = x2 + _dot(_cross_heads(q, mk_ref, mv_ref), wo_ref[...])


def _post_mix(x, o_att, o_gla, w_out_att, w_out_gla, g_cross, w_q, mem_k, mem_v, w_o, *, seq, tm=512):
    m, d = x.shape
    tm = min(tm, seq)
    assert seq % tm == 0
    const = lambda a: pl.BlockSpec(a.shape, lambda i: (0,) * a.ndim, pipeline_mode=pl.Buffered(1))
    rows = lambda width: pl.BlockSpec((tm, width), lambda i: (i, 0))
    mem = pl.BlockSpec((1, MEM_HEADS, MEM_LEN, MEM_HEAD_DIM), lambda i: (i // (seq // tm), 0, 0, 0))
    g = g_cross.reshape(1, d)
    return pl.pallas_call(
        _post_mix_body,
        grid=(m // tm,),
        in_specs=[rows(d), rows(ATT_WIDTH), rows(GLA_VALUE_WIDTH), const(w_out_att), const(w_out_gla), const(g),
                  const(w_q), mem, mem, const(w_o)],
        out_specs=rows(d),
        out_shape=jax.ShapeDtypeStruct((m, d), F32),
        compiler_params=_params("parallel"),
        name="post_mix",
    )(x, o_att, o_gla, w_out_att, w_out_gla, g, w_q, mem_k, mem_v, w_o)


def _cross(qc, mem_k, mem_v, x, w_o, *, rows, groups):
    m, d = x.shape
    mem_k, mem_v = (a.reshape(a.shape[0], MEM_LEN * MEM_HEADS, MEM_HEAD_DIM) for a in (mem_k, mem_v))
    mem = pl.BlockSpec((groups, MEM_LEN * MEM_HEADS, MEM_HEAD_DIM), lambda i: (i, 0, 0))
    return pl.pallas_call(
        functools.partial(_cross_body, groups=groups),
        grid=(m // rows,),
        in_specs=[
            pl.BlockSpec((rows, MEM_WIDTH), lambda i: (i, 0)),
            mem,
            mem,
            pl.BlockSpec((rows, d), lambda i: (i, 0)),
            pl.BlockSpec(w_o.shape, lambda i: (0, 0)),
        ],
        out_specs=pl.BlockSpec((rows, d), lambda i: (i, 0)),
        out_shape=jax.ShapeDtypeStruct((m, d), F32),
        compiler_params=_params("parallel"),
        name="cross",
    )(qc, mem_k, mem_v, x, w_o)


def kernel(x_prompt, x_sample, mem_prompt, cache_win_k, cache_win_v, state_gla, cache_mem_k, cache_mem_v, g_ffn1, w1_gate, w1_up, w1_down, g_mix, w_in, w_gla_a2, b_gla_a, g_gla_out, w_out, g_mem, w_mem_k, w_mem_v, g_cross, w_cross_q, w_cross_o, g_ffn2, w2_gate, w2_up, w2_down, g_final):
    b, seq, d = x_prompt.shape
    n, n_new, _ = x_sample.shape
    depth = g_ffn1.shape[0]
    past_len = cache_win_k.shape[2]
    n_keep = min(MAX_WINDOW, seq)
    bf = lambda a: a.astype(BF16)

    xp = x_prompt.reshape(b * seq, d)
    xs = x_sample.reshape(n * n_new, d)
    rope_p = _rope_tables(jnp.arange(seq))
    rope_s = _rope_tables(jnp.tile(past_len + jnp.arange(n_new), n))
    outs = [[] for _ in range(8)]
    cols = np.cumsum([0, ATT_WIDTH, ATT_WIDTH, ATT_WIDTH, GLA_KEY_WIDTH, GLA_KEY_WIDTH, GLA_VALUE_WIDTH, GLA_GATE_RANK])
    for l in range(depth):
        w_main = bf(jnp.concatenate([w_in[l][:, :cols[6]], w_in[l][:, cols[7]:]], axis=1))
        w_lr = bf(jnp.pad(w_in[l][:, cols[6]:cols[7]], ((0, 0), (0, LANES - GLA_GATE_RANK))))
        w_a2 = bf(jnp.pad(w_gla_a2[l], ((0, LANES - GLA_GATE_RANK), (0, 0))))
        w_out_att, w_out_gla = bf(w_out[l][:ATT_WIDTH]), bf(w_out[l][ATT_WIDTH:])
        w_mem = bf(jnp.concatenate([w_mem_k[l], w_mem_v[l]], axis=1))
        w_cq, w_co = bf(w_cross_q[l]), bf(w_cross_o[l])
        gf = g_final if l == depth - 1 else None
        mix_in = lambda x, rope: _mix_in(x, g_mix[l], w_main, w_lr, w_a2, b_gla_a[l], rope)

        w2 = (bf(w2_gate[l]), bf(w2_up[l]), bf(w2_down[l]))
        xs, *w1 = _ffn(xs, g_ffn1[l], w1_gate[l], w1_up[l], w1_down[l])
        zs, la_s = mix_in(xs, rope_s)
        weights_rider, values_rider, attn_finish = _attn_sample_riders(
            zs.reshape(n, n_new, Z_COLS), cache_win_k[l], cache_win_v[l])

        xp, att_w, att_aux = _ffn(xp, g_ffn1[l], *w1, rider=weights_rider)
        mem_kv = _norm_proj(mem_prompt.reshape(b * MEM_LEN, d), g_mem[l], w_mem).reshape(b, MEM_LEN, 2 * MEM_WIDTH)
        mem_k = mem_kv[:, :, :MEM_WIDTH].reshape(b, MEM_LEN, MEM_HEADS, MEM_HEAD_DIM)
        mem_v = mem_kv[:, :, MEM_WIDTH:].reshape(b, MEM_LEN, MEM_HEADS, MEM_HEAD_DIM)
        zp, la_p = mix_in(xp, rope_p)
        zp3 = zp.reshape(b, seq, Z_COLS)
        o_att_p = _attn_prompt(zp3).reshape(b * seq, ATT_WIDTH)
        o_gla_p, st_p = _gla_prompt(zp3, la_p.reshape(b, seq, GLA_KEY_WIDTH), g_gla_out[l])
        xp = _post_mix(xp, o_att_p, o_gla_p.reshape(b * seq, GLA_VALUE_WIDTH), w_out_att, w_out_gla, g_cross[l], w_cq,
                       mem_k.transpose(0, 2, 1, 3), mem_v.transpose(0, 2, 1, 3), w_co, seq=seq)
        xp, o_att_s = _ffn(xp, g_ffn2[l], *w2, g_final=gf, rider=values_rider(att_w, att_aux))

        o_att_s = attn_finish(o_att_s).reshape(n * n_new, ATT_WIDTH)
        o_gla_s, st_s = _gla_sample(zs, la_s, g_gla_out[l], state_gla[l], n_new)
        xs, qc = _out_cross_q(xs, o_att_s, o_gla_s, w_out_att, w_out_gla, g_cross[l], w_cq)
        groups = min(SAMPLE_GROUP, n)
        xs = _cross(qc, cache_mem_k[l], cache_mem_v[l], xs, w_co, rows=groups * n_new, groups=groups)
        xs = _ffn(xs, g_ffn2[l], *w2, g_final=gf)

        heads = lambda a, t: a.reshape(-1, t, ATT_HEADS, ATT_HEAD_DIM)
        outs[0].append(heads(zp3[:, seq - n_keep:, ATT_WIDTH:2 * ATT_WIDTH], n_keep))
        outs[1].append(heads(zp3[:, seq - n_keep:, 2 * ATT_WIDTH:3 * ATT_WIDTH], n_keep))
        outs[2].append(st_p)
        outs[3].append(mem_k)
        outs[4].append(mem_v)
        outs[5].append(heads(zs[:, ATT_WIDTH:2 * ATT_WIDTH], n_new))
        outs[6].append(heads(zs[:, 2 * ATT_WIDTH:3 * ATT_WIDTH], n_new))
        outs[7].append(st_s)

    return (xp.reshape(b, seq, d), xs.reshape(n, n_new, d)) + tuple(jnp.stack(o) for o in outs)
```
